```python
import math
import jax
import jax.numpy as jnp
from jax import lax
import numpy as np

D_MODEL = 1024
BATCH = 4
SEQ = 4096
DEPTH = 4
DEC_BATCH = 32
DEC_SEQ = 4
PAST_LEN = 8192
PAGE_SIZE = 128

N_EVEN = (DEPTH + 1) // 2
N_ODD = DEPTH // 2
HEAD_DIM = 64
A_HEADS = 4
A_V = 2 * HEAD_DIM
B_HEADS = 8
B_KV = 2
B_REP = B_HEADS // B_KV
CMP_BLOCK = 32
CMP_STRIDE = 16
CMP_HIDDEN = 128
SEL_BLOCK = 64
SEL_TOP = 16
WINDOW = 512
C_HEADS = 16
FORGET_BIAS = 2.0
D_FF = 2816
Q_BLOCK = 128
RMS_EPS = 1e-6
NEG_INF = -1e30
EVEN_SIZES = (A_HEADS * 2 * HEAD_DIM, A_HEADS * 2 * HEAD_DIM, A_HEADS * A_V, B_HEADS * HEAD_DIM,
              B_KV * HEAD_DIM, B_KV * HEAD_DIM, B_KV * HEAD_DIM, B_KV * HEAD_DIM, B_KV * HEAD_DIM, B_KV * HEAD_DIM,
              3 * B_HEADS)
EVEN_WIDTH = 2 * A_HEADS * 2 * HEAD_DIM + A_HEADS * A_V + B_HEADS * HEAD_DIM + 6 * B_KV * HEAD_DIM + 3 * B_HEADS
EVEN_MIX = A_HEADS * A_V + B_HEADS * HEAD_DIM
ODD_SIZES = (C_HEADS * HEAD_DIM, C_HEADS * HEAD_DIM, C_HEADS * HEAD_DIM, C_HEADS)
ODD_WIDTH = 3 * C_HEADS * HEAD_DIM + C_HEADS
ODD_MIX = C_HEADS * HEAD_DIM

kernel_name = 'hybrid_diff_nsa_fox_macaron_step'


def rmsnorm(x, g):
    x32 = x.astype(jnp.float32)
    y = x32 * lax.rsqrt(jnp.mean(x32 * x32, axis=-1, keepdims=True) + RMS_EPS)
    return (y * g.astype(jnp.float32)).astype(x.dtype)


def swiglu(x, wg, wu, wd):
    return (jax.nn.silu(x @ wg) * (x @ wu)) @ wd


def split_last(a, sizes):
    cuts, acc = [], 0
    for s in sizes[:-1]:
        acc += s
        cuts.append(acc)
    return jnp.split(a, cuts, axis=-1)


def alibi_slopes(n):
    return jnp.exp2(-8.0 * jnp.arange(1, n + 1, dtype=jnp.float32) / n)


def masked_softmax(s, mask):
    s = jnp.where(mask, s.astype(jnp.float32), NEG_INF)
    e = jnp.exp(s - jnp.max(s, axis=-1, keepdims=True)) * mask
    return e / jnp.maximum(jnp.sum(e, axis=-1, keepdims=True), 1e-30)


def unblock(o):
    o = jnp.swapaxes(o, 0, 1)
    return o.reshape((o.shape[0], o.shape[1] * o.shape[2]) + o.shape[3:])


def concat_rows(past, new):
    return jnp.concatenate([past, new.astype(past.dtype)], axis=1)


def gather_pages(cache, layer, page_table):
    g = cache[layer, page_table]
    return g.reshape((g.shape[0], g.shape[1] * g.shape[2]) + g.shape[3:])


def diff_core(q, k, v, q_pos, k_pos, slopes, lam):
    dist = (q_pos[:, None] - k_pos[None, :]).astype(jnp.float32)
    s = jnp.einsum('bqhcd,bkhcd->bchqk', q, k).astype(jnp.float32) * (HEAD_DIM ** -0.5)
    s = s - slopes[:, None, None] * dist
    p = masked_softmax(s, dist >= 0)
    a = p[:, 0] - lam * p[:, 1]
    return jnp.einsum('bhqk,bkhe->bqhe', a, v.astype(jnp.float32))


def compress(rows, pos_emb, w1, w2):
    b, t = rows.shape[:2]
    ratio = CMP_BLOCK // CMP_STRIDE
    n_chunk = t // CMP_STRIDE
    n_cmp = n_chunk - ratio + 1
    chunks = rows[:, :n_chunk * CMP_STRIDE].reshape(b, n_chunk, CMP_STRIDE, B_KV, HEAD_DIM)
    blocks = jnp.concatenate([chunks[:, r:r + n_cmp] for r in range(ratio)], axis=2)
    blocks = blocks + pos_emb[None, None, :, None, :]
    flat = jnp.swapaxes(blocks, 2, 3).reshape(b, n_cmp, B_KV, CMP_BLOCK * HEAD_DIM)
    return jax.nn.silu(flat @ w1) @ w2


def cmp_end_positions(n_cmp):
    return jnp.arange(n_cmp) * CMP_STRIDE + CMP_BLOCK - 1


def sel_blocks(rows):
    b, t = rows.shape[:2]
    n_sel = -(-t // SEL_BLOCK)
    rows = jnp.pad(rows, ((0, 0), (0, n_sel * SEL_BLOCK - t), (0, 0), (0, 0)))
    return rows.reshape(b, n_sel, SEL_BLOCK, B_KV, HEAD_DIM).transpose(0, 3, 1, 2, 4)


def overlap_matrix(n_cmp, n_sel):
    c0 = jnp.arange(n_cmp)[:, None] * CMP_STRIDE
    s0 = jnp.arange(n_sel)[None, :] * SEL_BLOCK
    return ((c0 < s0 + SEL_BLOCK) & (c0 + CMP_BLOCK > s0)).astype(jnp.float32)


def nsa_core(q, q_pos, kc, vc, c_end, ks_blk, vs_blk, kw, vw, w_pos, gates, slopes):
    b, tq = q.shape[:2]
    scale = HEAD_DIM ** -0.5
    qg = q.reshape(b, tq, B_KV, B_REP, HEAD_DIM)
    sl = slopes.reshape(B_KV, B_REP)
    dc = (q_pos[:, None] - c_end[None, :]).astype(jnp.float32)
    sc = jnp.einsum('bqgrd,bngd->bgrqn', qg, kc).astype(jnp.float32) * scale - sl[:, :, None, None] * dc
    pc = masked_softmax(sc, dc >= 0)
    o_cmp = jnp.einsum('bgrqn,bngd->bqgrd', pc, vc.astype(jnp.float32))
    n_sel = ks_blk.shape[2]
    imp = jnp.sum(pc, axis=2) @ overlap_matrix(kc.shape[1], n_sel)
    blk = jnp.arange(n_sel)[None, :]
    valid = blk * SEL_BLOCK <= q_pos[:, None]
    forced = (blk == q_pos[:, None] // SEL_BLOCK) | (blk == 0)
    imp = jnp.where(forced, jnp.inf, jnp.where(valid, imp, -jnp.inf))
    _, idx = lax.top_k(imp, min(SEL_TOP, n_sel))
    bi = jnp.arange(b)[:, None, None, None]
    gi = jnp.arange(B_KV)[None, :, None, None]
    k_sel = ks_blk[bi, gi, idx]
    v_sel = vs_blk[bi, gi, idx]
    s_pos = idx[..., None] * SEL_BLOCK + jnp.arange(SEL_BLOCK)
    ds = (q_pos[None, None, :, None, None] - s_pos).astype(jnp.float32)
    ss = jnp.einsum('bqgrd,bgqksd->bgrqks', qg, k_sel).astype(jnp.float32) * scale
    ss = ss - sl[None, :, :, None, None, None] * ds[:, :, None]
    ps = masked_softmax(ss.reshape(b, B_KV, B_REP, tq, -1), (ds >= 0).reshape(b, B_KV, 1, tq, -1))
    o_sel = jnp.einsum('bgrqm,bgqmd->bqgrd', ps, v_sel.reshape(b, B_KV, tq, -1, HEAD_DIM).astype(jnp.float32))
    dw = q_pos[:, None] - w_pos[None, :]
    sw = jnp.einsum('bqgrd,bkgd->bgrqk', qg, kw).astype(jnp.float32) * scale - sl[:, :, None, None] * dw.astype(jnp.float32)
    pw = masked_softmax(sw, (dw >= 0) & (dw < WINDOW) & (w_pos[None, :] >= 0))
    o_win = jnp.einsum('bgrqk,bkgd->bqgrd', pw, vw.astype(jnp.float32))
    g = jax.nn.sigmoid(gates.astype(jnp.float32)).reshape(b, tq, B_KV, B_REP, 3)
    o = g[..., 0:1] * o_cmp + g[..., 1:2] * o_sel + g[..., 2:3] * o_win
    return o.reshape(b, tq, B_HEADS, HEAD_DIM)


def project_even(h, w_in):
    b, t, _ = h.shape
    aq, ak, av, bq, ck, cv, sk, sv, wk, wv, bg = split_last(h @ w_in, EVEN_SIZES)
    kvs = (b, t, B_KV, HEAD_DIM)
    return dict(aq=aq.reshape(b, t, A_HEADS, 2, HEAD_DIM), ak=ak.reshape(b, t, A_HEADS, 2, HEAD_DIM),
                av=av.reshape(b, t, A_HEADS, A_V), bq=bq.reshape(b, t, B_HEADS, HEAD_DIM),
                ck=ck.reshape(kvs), cv=cv.reshape(kvs), sk=sk.reshape(kvs), sv=sv.reshape(kvs),
                wk=wk.reshape(kvs), wv=wv.reshape(kvs), bg=bg.reshape(b, t, B_HEADS, 3))


def even_output(oa, ob, subln_g, lam_init, w_out):
    b, t = oa.shape[:2]
    oa = rmsnorm(oa, subln_g) * (1.0 - lam_init)
    o = jnp.concatenate([oa.reshape(b, t, -1), ob.reshape(b, t, -1)], axis=-1)
    return o.astype(w_out.dtype) @ w_out


def even_mix_prompt(h, keep, w_in, w_out, lam, lam_init, subln_g, cmpk, cmpv, slopes_a, slopes_b):
    b, t, _ = h.shape
    pc = project_even(h, w_in)
    pos = jnp.arange(t)
    kc = compress(pc['ck'], *cmpk)
    vc = compress(pc['cv'], *cmpv)
    c_end = cmp_end_positions(kc.shape[1])
    ksb = sel_blocks(pc['sk'])
    vsb = sel_blocks(pc['sv'])
    pad = ((0, 0), (WINDOW, 0), (0, 0), (0, 0))
    wk_pad = jnp.pad(pc['wk'], pad)
    wv_pad = jnp.pad(pc['wv'], pad)

    def block(i):
        s0 = i * Q_BLOCK
        qp = s0 + jnp.arange(Q_BLOCK)
        cut = lambda a: lax.dynamic_slice_in_dim(a, s0, Q_BLOCK, axis=1)
        oa = diff_core(cut(pc['aq']), pc['ak'], pc['av'], qp, pos, slopes_a, lam)
        kw = lax.dynamic_slice_in_dim(wk_pad, s0, WINDOW + Q_BLOCK, axis=1)
        vw = lax.dynamic_slice_in_dim(wv_pad, s0, WINDOW + Q_BLOCK, axis=1)
        w_pos = s0 - WINDOW + jnp.arange(WINDOW + Q_BLOCK)
        ob = nsa_core(cut(pc['bq']), qp, kc, vc, c_end, ksb, vsb, kw, vw, w_pos, cut(pc['bg']), slopes_b)
        return oa, ob

    oa, ob = lax.map(block, jnp.arange(t // Q_BLOCK))
    y = even_output(unblock(oa), unblock(ob), subln_g, lam_init, w_out)

    def last_rows(a):
        a = jnp.pad(a, ((0, 0), (max(0, keep - t), 0), (0, 0), (0, 0)))
        return a[:, a.shape[1] - keep:]

    rows = (pc['ak'], pc['av'], pc['ck'], pc['cv'], pc['sk'], pc['sv'], last_rows(pc['wk']), last_rows(pc['wv']))
    return y, rows


def even_mix_sample(h, past, win_k, win_v, w_in, w_out, lam, lam_init, subln_g, cmpk, cmpv, slopes_a, slopes_b):
    b, t, _ = h.shape
    p_ak, p_av, p_ck, p_cv, p_sk, p_sv = past
    n_past = p_ak.shape[1]
    keep = win_k.shape[1]
    pc = project_even(h, w_in)
    k_pos = jnp.arange(n_past + t)
    q_pos = n_past + jnp.arange(t)
    oa = diff_core(pc['aq'], concat_rows(p_ak, pc['ak']), concat_rows(p_av, pc['av']), q_pos, k_pos, slopes_a, lam)
    kc = compress(concat_rows(p_ck, pc['ck']), *cmpk)
    vc = compress(concat_rows(p_cv, pc['cv']), *cmpv)
    c_end = cmp_end_positions(kc.shape[1])
    ksb = sel_blocks(concat_rows(p_sk, pc['sk']))
    vsb = sel_blocks(concat_rows(p_sv, pc['sv']))
    kw = concat_rows(win_k, pc['wk'])
    vw = concat_rows(win_v, pc['wv'])
    w_pos = n_past - keep + jnp.arange(keep + t)
    ob = nsa_core(pc['bq'], q_pos, kc, vc, c_end, ksb, vsb, kw, vw, w_pos, pc['bg'], slopes_b)
    y = even_output(oa, ob, subln_g, lam_init, w_out)
    rows = (pc['ak'], pc['av'], pc['ck'], pc['cv'], pc['sk'], pc['sv'], kw[:, t:], vw[:, t:])
    return y, rows


def fox_core(q, k, v, f_q, f_k, q_pos, k_pos):
    s = jnp.einsum('bqhd,bkhd->bhqk', q, k).astype(jnp.float32) * (HEAD_DIM ** -0.5)
    s = s + (jnp.swapaxes(f_q, 1, 2)[:, :, :, None] - jnp.swapaxes(f_k, 1, 2)[:, :, None, :])
    p = masked_softmax(s, q_pos[:, None] >= k_pos[None, :])
    return jnp.einsum('bhqk,bkhd->bqhd', p, v.astype(jnp.float32))


def project_odd(h, w_in, b_f):
    b, t, _ = h.shape
    q, k, v, f = split_last(h @ w_in, ODD_SIZES)
    hd = (b, t, C_HEADS, HEAD_DIM)
    logf = jax.nn.log_sigmoid((f + b_f).astype(jnp.float32))
    return q.reshape(hd), k.reshape(hd), v.reshape(hd), logf


def odd_mix_prompt(h, w_in, b_f, w_out):
    b, t, _ = h.shape
    q, k, v, logf = project_odd(h, w_in, b_f)
    cum = jnp.cumsum(logf, axis=1)
    pos = jnp.arange(t)

    def block(i):
        s0 = i * Q_BLOCK
        cut = lambda a: lax.dynamic_slice_in_dim(a, s0, Q_BLOCK, axis=1)
        return fox_core(cut(q), k, v, cut(cum), cum, s0 + jnp.arange(Q_BLOCK), pos)

    o = unblock(lax.map(block, jnp.arange(t // Q_BLOCK)))
    y = o.reshape(b, t, ODD_MIX).astype(w_out.dtype) @ w_out
    return y, (k, v, logf)


def odd_mix_sample(h, p_k, p_v, p_logf, w_in, b_f, w_out):
    b, t, _ = h.shape
    n_past = p_k.shape[1]
    q, k, v, logf = project_odd(h, w_in, b_f)
    cum = jnp.cumsum(jnp.concatenate([p_logf.astype(jnp.float32), logf], axis=1), axis=1)
    o = fox_core(q, concat_rows(p_k, k), concat_rows(p_v, v), cum[:, n_past:], cum,
                 n_past + jnp.arange(t), jnp.arange(n_past + t))
    y = o.reshape(b, t, ODD_MIX).astype(w_out.dtype) @ w_out
    return y, (k, v, logf)


def setup_inputs(seed: int = 0) -> dict:
    key = jax.random.key(seed)
    keys = iter(jax.random.split(key, 64))
    f32 = jnp.float32

    def nrm(shape, scale=1.0):
        return scale * jax.random.normal(next(keys), shape, f32)

    def gain(shape):
        return 1.0 + nrm(shape, 0.02)

    n_pages = PAST_LEN // PAGE_SIZE
    n_used = DEC_BATCH * n_pages
    n_pool = n_used + max(1, n_used // 4)
    win_keep = min(WINDOW, PAST_LEN)
    x_prompt = nrm((BATCH, SEQ, D_MODEL))
    x_sample = nrm((DEC_BATCH, DEC_SEQ, D_MODEL))
    cache_diff_k = nrm((N_EVEN, n_pool, PAGE_SIZE, A_HEADS, 2, HEAD_DIM))
    cache_diff_v = nrm((N_EVEN, n_pool, PAGE_SIZE, A_HEADS, A_V))
    cache_nsa_cmp_k = nrm((N_EVEN, n_pool, PAGE_SIZE, B_KV, HEAD_DIM))
    cache_nsa_cmp_v = nrm((N_EVEN, n_pool, PAGE_SIZE, B_KV, HEAD_DIM))
    cache_nsa_sel_k = nrm((N_EVEN, n_pool, PAGE_SIZE, B_KV, HEAD_DIM))
    cache_nsa_sel_v = nrm((N_EVEN, n_pool, PAGE_SIZE, B_KV, HEAD_DIM))
    state_nsa_win_k = nrm((N_EVEN, DEC_BATCH, win_keep, B_KV, HEAD_DIM))
    state_nsa_win_v = nrm((N_EVEN, DEC_BATCH, win_keep, B_KV, HEAD_DIM))
    cache_fox_k = nrm((N_ODD, n_pool, PAGE_SIZE, C_HEADS, HEAD_DIM))
    cache_fox_v = nrm((N_ODD, n_pool, PAGE_SIZE, C_HEADS, HEAD_DIM))
    cache_fox_logf = jax.nn.log_sigmoid(FORGET_BIAS + nrm((N_ODD, n_pool, PAGE_SIZE, C_HEADS)))
    page_table = jax.random.permutation(next(keys), n_pool)[:n_used].reshape(DEC_BATCH, n_pages).astype(jnp.int32)
    sd, sf = D_MODEL ** -0.5, D_FF ** -0.5
    return {
        'x_prompt': x_prompt, 'x_sample': x_sample,
        'cache_diff_k': cache_diff_k, 'cache_diff_v': cache_diff_v,
        'cache_nsa_cmp_k': cache_nsa_cmp_k, 'cache_nsa_cmp_v': cache_nsa_cmp_v,
        'cache_nsa_sel_k': cache_nsa_sel_k, 'cache_nsa_sel_v': cache_nsa_sel_v,
        'state_nsa_win_k': state_nsa_win_k, 'state_nsa_win_v': state_nsa_win_v,
        'cache_fox_k': cache_fox_k, 'cache_fox_v': cache_fox_v, 'cache_fox_logf': cache_fox_logf,
        'page_table': page_table,
        'norm_ffn1_g': gain((DEPTH, D_MODEL)),
        'ffn1_wg': nrm((DEPTH, D_MODEL, D_FF), sd), 'ffn1_wu': nrm((DEPTH, D_MODEL, D_FF), sd),
        'ffn1_wd': nrm((DEPTH, D_FF, D_MODEL), sf),
        'norm_mix_g': gain((DEPTH, D_MODEL)),
        'norm_ffn2_g': gain((DEPTH, D_MODEL)),
        'ffn2_wg': nrm((DEPTH, D_MODEL, D_FF), sd), 'ffn2_wu': nrm((DEPTH, D_MODEL, D_FF), sd),
        'ffn2_wd': nrm((DEPTH, D_FF, D_MODEL), sf),
        'even_w_in': nrm((N_EVEN, D_MODEL, EVEN_WIDTH), sd),
        'even_w_out': nrm((N_EVEN, EVEN_MIX, D_MODEL), EVEN_MIX ** -0.5),
        'diff_lambda_q1': nrm((N_EVEN, HEAD_DIM), 0.1), 'diff_lambda_k1': nrm((N_EVEN, HEAD_DIM), 0.1),
        'diff_lambda_q2': nrm((N_EVEN, HEAD_DIM), 0.1), 'diff_lambda_k2': nrm((N_EVEN, HEAD_DIM), 0.1),
        'diff_subln_g': gain((N_EVEN, A_V)),
        'cmp_pos_k': nrm((N_EVEN, CMP_BLOCK, HEAD_DIM), 0.1),
        'cmp_w1_k': nrm((N_EVEN, CMP_BLOCK * HEAD_DIM, CMP_HIDDEN), (CMP_BLOCK * HEAD_DIM) ** -0.5),
        'cmp_w2_k': nrm((N_EVEN, CMP_HIDDEN, HEAD_DIM), CMP_HIDDEN ** -0.5),
        'cmp_pos_v': nrm((N_EVEN, CMP_BLOCK, HEAD_DIM), 0.1),
        'cmp_w1_v': nrm((N_EVEN, CMP_BLOCK * HEAD_DIM, CMP_HIDDEN), (CMP_BLOCK * HEAD_DIM) ** -0.5),
        'cmp_w2_v': nrm((N_EVEN, CMP_HIDDEN, HEAD_DIM), CMP_HIDDEN ** -0.5),
        'odd_w_in': nrm((N_ODD, D_MODEL, ODD_WIDTH), sd),
        'odd_b_f': FORGET_BIAS + nrm((N_ODD, C_HEADS), 0.1),
        'odd_w_out': nrm((N_ODD, ODD_MIX, D_MODEL), ODD_MIX ** -0.5),
        'final_norm_g': gain((D_MODEL,)),
    }


def reference(x_prompt, x_sample, cache_diff_k, cache_diff_v, cache_nsa_cmp_k, cache_nsa_cmp_v,
              cache_nsa_sel_k, cache_nsa_sel_v, state_nsa_win_k, state_nsa_win_v, cache_fox_k, cache_fox_v,
              cache_fox_logf, page_table, norm_ffn1_g, ffn1_wg, ffn1_wu, ffn1_wd, norm_mix_g, norm_ffn2_g,
              ffn2_wg, ffn2_wu, ffn2_wd, even_w_in, even_w_out, diff_lambda_q1, diff_lambda_k1, diff_lambda_q2,
              diff_lambda_k2, diff_subln_g, cmp_pos_k, cmp_w1_k, cmp_w2_k, cmp_pos_v, cmp_w1_v, cmp_w2_v,
              odd_w_in, odd_b_f, odd_w_out, final_norm_g):
    slopes_a = alibi_slopes(A_HEADS)
    slopes_b = alibi_slopes(B_HEADS)
    keep = state_nsa_win_k.shape[2]
    even_p = [[] for _ in range(8)]
    even_s = [[] for _ in range(8)]
    odd_p = [[] for _ in range(3)]
    odd_s = [[] for _ in range(3)]
    xp, xs = x_prompt, x_sample
    for li in range(DEPTH):
        xp = xp + 0.5 * swiglu(rmsnorm(xp, norm_ffn1_g[li]), ffn1_wg[li], ffn1_wu[li], ffn1_wd[li])
        xs = xs + 0.5 * swiglu(rmsnorm(xs, norm_ffn1_g[li]), ffn1_wg[li], ffn1_wu[li], ffn1_wd[li])
        hp = rmsnorm(xp, norm_mix_g[li])
        hs = rmsnorm(xs, norm_mix_g[li])
        if li % 2 == 0:
            e = li // 2
            lam_init = 0.8 - 0.6 * math.exp(-0.3 * li)
            lam = (jnp.exp(jnp.sum(diff_lambda_q1[e] * diff_lambda_k1[e]).astype(jnp.float32))
                   - jnp.exp(jnp.sum(diff_lambda_q2[e] * diff_lambda_k2[e]).astype(jnp.float32)) + lam_init)
            cmpk = (cmp_pos_k[e], cmp_w1_k[e], cmp_w2_k[e])
            cmpv = (cmp_pos_v[e], cmp_w1_v[e], cmp_w2_v[e])
            mp, rows_p = even_mix_prompt(hp, keep, even_w_in[e], even_w_out[e], lam, lam_init, diff_subln_g[e],
                                         cmpk, cmpv, slopes_a, slopes_b)
            past = (gather_pages(cache_diff_k, e, page_table), gather_pages(cache_diff_v, e, page_table),
                    gather_pages(cache_nsa_cmp_k, e, page_table), gather_pages(cache_nsa_cmp_v, e, page_table),
                    gather_pages(cache_nsa_sel_k, e, page_table), gather_pages(cache_nsa_sel_v, e, page_table))
            ms, rows_s = even_mix_sample(hs, past, state_nsa_win_k[e], state_nsa_win_v[e], even_w_in[e],
                                         even_w_out[e], lam, lam_init, diff_subln_g[e], cmpk, cmpv,
                                         slopes_a, slopes_b)
            for lst, r in zip(even_p, rows_p):
                lst.append(r)
            for lst, r in zip(even_s, rows_s):
                lst.append(r)
        else:
            o = li // 2
            mp, rows_p = odd_mix_prompt(hp, odd_w_in[o], odd_b_f[o], odd_w_out[o])
            ms, rows_s = odd_mix_sample(hs, gather_pages(cache_fox_k, o, page_table),
                                        gather_pages(cache_fox_v, o, page_table),
                                        gather_pages(cache_fox_logf, o, page_table),
                                        odd_w_in[o], odd_b_f[o], odd_w_out[o])
            for lst, r in zip(odd_p, rows_p):
                lst.append(r)
            for lst, r in zip(odd_s, rows_s):
                lst.append(r)
        xp = xp + mp
        xs = xs + ms
        xp = xp + 0.5 * swiglu(rmsnorm(xp, norm_ffn2_g[li]), ffn2_wg[li], ffn2_wu[li], ffn2_wd[li])
        xs = xs + 0.5 * swiglu(rmsnorm(xs, norm_ffn2_g[li]), ffn2_wg[li], ffn2_wu[li], ffn2_wd[li])
    y_prompt = rmsnorm(xp, final_norm_g)
    y_sample = rmsnorm(xs, final_norm_g)
    (diff_k_p, diff_v_p, cmp_k_p, cmp_v_p, sel_k_p, sel_v_p, win_k_p, win_v_p) = [jnp.stack(l, 0) for l in even_p]
    (diff_k_s, diff_v_s, cmp_k_s, cmp_v_s, sel_k_s, sel_v_s, win_k_s, win_v_s) = [jnp.stack(l, 0) for l in even_s]
    (fox_k_p, fox_v_p, fox_logf_p) = [jnp.stack(l, 0) for l in odd_p]
    (fox_k_s, fox_v_s, fox_logf_s) = [jnp.stack(l, 0) for l in odd_s]
    return (y_prompt, y_sample,
            diff_k_p, diff_k_s, diff_v_p, diff_v_s,
            cmp_k_p, cmp_k_s, cmp_v_p, cmp_v_s,
            sel_k_p, sel_k_s, sel_v_p, sel_v_s,
            win_k_p, win_k_s, win_v_p, win_v_s,
            fox_k_p, fox_k_s, fox_v_p, fox_v_s, fox_logf_p, fox_logf_s)
```

```python
import functools
import math

import jax
import jax.numpy as jnp
from jax import lax
from jax.experimental import pallas as pl
from jax.experimental.pallas import tpu as pltpu

HEAD_DIM = 64
A_HEADS = 4
A_V = 2 * HEAD_DIM
B_HEADS = 8
B_KV = 2
B_REP = B_HEADS // B_KV
CMP_BLOCK = 32
CMP_STRIDE = 16
CMP_HIDDEN = 128
SEL_BLOCK = 64
SEL_TOP = 16
WINDOW = 512
C_HEADS = 16
RMS_EPS = 1e-6
NEG_INF = -1e30
SCALE = HEAD_DIM ** -0.5

LANES = 128
V7X_VMEM_LIMIT_BYTES = 56 * 1024 * 1024

F32 = jnp.float32
BF16 = jnp.bfloat16


def _cparams(sem):
    return pltpu.CompilerParams(dimension_semantics=sem, vmem_limit_bytes=V7X_VMEM_LIMIT_BYTES)


def _dot(a, b):
    return jnp.dot(a, b, preferred_element_type=F32)


def _dot_nt(a, b):
    return lax.dot_general(a, b, (((1,), (1,)), ((), ())), preferred_element_type=F32)


def _split3(x):
    hi = x.astype(BF16)
    r1 = x - hi.astype(F32)
    mid = r1.astype(BF16)
    lo = (r1 - mid.astype(F32)).astype(BF16)
    return hi, mid, lo


def _dot_exact_rhs(x, m_bf16):
    hi, mid, lo = _split3(x)
    return _dot(hi, m_bf16) + _dot(mid, m_bf16) + _dot(lo, m_bf16)


def _dot_nt_precise(a, b):
    ah = a.astype(BF16)
    al = (a - ah.astype(F32)).astype(BF16)
    bh = b.astype(BF16)
    bl = (b - bh.astype(F32)).astype(BF16)
    return _dot_nt(ah, bh) + _dot_nt(ah, bl) + _dot_nt(al, bh)


def _rms(x, g):
    ms = jnp.mean(x * x, axis=-1, keepdims=True)
    return x * lax.rsqrt(ms + RMS_EPS) * g


def _iota(shape, dim):
    return lax.broadcasted_iota(jnp.int32, shape, dim)


def _masked_softmax_rows(s, mask):
    s = jnp.where(mask, s, NEG_INF)
    e = jnp.where(mask, jnp.exp(s - jnp.max(s, axis=-1, keepdims=True)), 0.0)
    return e / jnp.maximum(jnp.sum(e, axis=-1, keepdims=True), 1e-30)


def _online_step(s, v_bf16, m_ref, l_ref, acc_ref):
    m_old = m_ref[...]
    m_new = jnp.maximum(m_old, jnp.max(s, axis=-1, keepdims=True))
    alpha = jnp.exp(m_old - m_new)
    p = jnp.exp(s - m_new)
    l_ref[...] = alpha * l_ref[...] + jnp.sum(p, axis=-1, keepdims=True)
    acc_ref[...] = alpha * acc_ref[...] + _dot(p.astype(BF16), v_bf16)
    m_ref[...] = m_new


def _row_tile(n, pref):
    for t in pref:
        if n % t == 0:
            return t
    return n


def _ffn_kernel(x_ref, g_ref, wg_ref, wu_ref, wd_ref, *rest, fc, nfc, has_final):
    if has_final:
        gf_ref, o_ref = rest
    else:
        (o_ref,) = rest
    x = x_ref[...]
    h = _rms(x, g_ref[...]).astype(BF16)
    acc = jnp.zeros(x.shape, F32)
    for c in range(nfc):
        sl = slice(c * fc, (c + 1) * fc)
        a = _dot(h, wg_ref[:, sl])
        u = _dot(h, wu_ref[:, sl])
        act = (a * jax.nn.sigmoid(a) * u).astype(BF16)
        acc = acc + _dot(act, wd_ref[sl, :])
    y = x + 0.5 * acc
    if has_final:
        y = _rms(y, gf_ref[...])
    o_ref[...] = y


def _ffn(x, g, wg, wu, wd, final_g=None):
    n, d = x.shape
    f = wg.shape[1]
    tm = _row_tile(n, (512, 256, 128))
    fc = _row_tile(f, (256, 128))
    const = lambda i: (0, 0)
    in_specs = [pl.BlockSpec((tm, d), lambda i: (i, 0)), pl.BlockSpec((1, d), const),
                pl.BlockSpec((d, f), const), pl.BlockSpec((d, f), const), pl.BlockSpec((f, d), const)]
    args = [x, g.reshape(1, d), wg, wu, wd]
    if final_g is not None:
        in_specs.append(pl.BlockSpec((1, d), const))
        args.append(final_g.reshape(1, d))
    return pl.pallas_call(
        functools.partial(_ffn_kernel, fc=fc, nfc=f // fc, has_final=final_g is not None),
        grid=(n // tm,), in_specs=in_specs, out_specs=pl.BlockSpec((tm, d), lambda i: (i, 0)),
        out_shape=jax.ShapeDtypeStruct((n, d), F32), compiler_params=_cparams(("arbitrary",)),
        name="ffn")(*args)


def _norm_proj_kernel(x_ref, g_ref, *refs, n_out):
    h = _rms(x_ref[...], g_ref[...]).astype(BF16)
    for w_ref, o_ref in zip(refs[:n_out], refs[n_out:]):
        o_ref[...] = _dot(h, w_ref[...])


def _norm_proj(x, g, ws):
    n, d = x.shape
    tm = _row_tile(n, (512, 256, 128))
    const = lambda i: (0, 0)
    in_specs = [pl.BlockSpec((tm, d), lambda i: (i, 0)), pl.BlockSpec((1, d), const)]
    in_specs += [pl.BlockSpec(w.shape, const) for w in ws]
    return pl.pallas_call(
        functools.partial(_norm_proj_kernel, n_out=len(ws)),
        grid=(n // tm,), in_specs=in_specs,
        out_specs=[pl.BlockSpec((tm, w.shape[1]), lambda i: (i, 0)) for w in ws],
        out_shape=[jax.ShapeDtypeStruct((n, w.shape[1]), F32) for w in ws],
        compiler_params=_cparams(("arbitrary",)), name="norm_proj")(x, g.reshape(1, d), *ws)


def _out_proj_kernel(x_ref, *refs, n_in):
    y = x_ref[...]
    for o_ref, w_ref in zip(refs[:n_in], refs[n_in:2 * n_in]):
        y = y + _dot(o_ref[...].astype(BF16), w_ref[...])
    refs[2 * n_in][...] = y


def _out_proj(x, os_, ws):
    n, d = x.shape
    tm = _row_tile(n, (512, 256, 128))
    const = lambda i: (0, 0)
    in_specs = [pl.BlockSpec((tm, d), lambda i: (i, 0))]
    in_specs += [pl.BlockSpec((tm, o.shape[1]), lambda i: (i, 0)) for o in os_]
    in_specs += [pl.BlockSpec(w.shape, const) for w in ws]
    return pl.pallas_call(
        functools.partial(_out_proj_kernel, n_in=len(os_)),
        grid=(n // tm,), in_specs=in_specs, out_specs=pl.BlockSpec((tm, d), lambda i: (i, 0)),
        out_shape=jax.ShapeDtypeStruct((n, d), F32), compiler_params=_cparams(("arbitrary",)),
        name="out_proj")(x, *os_, *ws)


def _pad_cols(w, width):
    return jnp.pad(w, ((0, 0), (0, width - w.shape[1])))


def _logf_cum_kernel(f_ref, b_ref, lf_ref, cum_ref, carry_ref, *, tc):
    @pl.when(pl.program_id(1) == 0)
    def _():
        carry_ref[...] = jnp.zeros(carry_ref.shape, F32)

    z = f_ref[...] + b_ref[...]
    lf = jnp.minimum(z, 0.0) - jnp.log1p(jnp.exp(-jnp.abs(z)))
    tri = (_iota((tc, tc), 0) >= _iota((tc, tc), 1)).astype(BF16)
    cum = _dot_exact_rhs_left(tri, lf) + carry_ref[...]
    lf_ref[...] = lf[:, :C_HEADS]
    cum_ref[...] = cum[:, :C_HEADS]
    carry_ref[...] = cum[tc - 1:tc, :]


def _dot_exact_rhs_left(m_bf16, x):
    hi, mid, lo = _split3(x)
    return _dot(m_bf16, hi) + _dot(m_bf16, mid) + _dot(m_bf16, lo)


def _logf_cum(f, b_f):
    b, t, _ = f.shape
    tc = _row_tile(t, (256, 128))
    bias = jnp.pad(b_f, (0, LANES - C_HEADS)).reshape(1, LANES)
    out = jax.ShapeDtypeStruct((b, t, C_HEADS), F32)
    return pl.pallas_call(
        functools.partial(_logf_cum_kernel, tc=tc),
        grid=(b, t // tc),
        in_specs=[pl.BlockSpec((None, tc, LANES), lambda i, j: (i, j, 0)), pl.BlockSpec((1, LANES), lambda i, j: (0, 0))],
        out_specs=[pl.BlockSpec((None, tc, C_HEADS), lambda i, j: (i, j, 0))] * 2,
        out_shape=[out, out], scratch_shapes=[pltpu.VMEM((1, LANES), F32)],
        compiler_params=_cparams(("arbitrary", "arbitrary")), name="logf_cum")(f, bias)


def _fox_prompt_kernel(q_ref, k_ref, v_ref, fq_ref, fk_ref, o_ref, m_sc, l_sc, acc_sc, *, tq, tk):
    q0 = pl.program_id(2) * tq
    qpos = q0 + _iota((tq, 1), 0)
    n_k = (q0 + tq + tk - 1) // tk
    for sub in range(2):
        cs = slice(sub * HEAD_DIM, (sub + 1) * HEAD_DIM)
        qs = (q_ref[:, cs] * SCALE).astype(BF16)
        fq = fq_ref[:, sub:sub + 1]
        m_sc[...] = jnp.full(m_sc.shape, NEG_INF, F32)
        l_sc[...] = jnp.zeros(l_sc.shape, F32)
        acc_sc[...] = jnp.zeros(acc_sc.shape, F32)

        def body(kj, carry, cs=cs, qs=qs, fq=fq, sub=sub):
            k0 = pl.multiple_of(kj * tk, tk)
            kt = k_ref[pl.ds(k0, tk), cs].astype(BF16)
            vt = v_ref[pl.ds(k0, tk), cs].astype(BF16)
            fk = fk_ref[kj, sub:sub + 1, :]
            kpos = k0 + _iota((1, tk), 1)
            s = _dot_nt(qs, kt) + (fq - fk)
            s = jnp.where(qpos >= kpos, s, NEG_INF)
            _online_step(s, vt, m_sc, l_sc, acc_sc)
            return carry

        lax.fori_loop(0, n_k, body, 0)
        o_ref[:, cs] = acc_sc[...] / l_sc[...]


def _fox_prompt(q, k, v, cum):
    b, t, w = q.shape
    tq = _row_tile(t, (256, 128))
    tk = _row_tile(t, (512, 256, 128))
    npair = C_HEADS // 2
    fq = cum.reshape(b, t, npair, 2).transpose(0, 2, 1, 3)
    fk = cum.reshape(b, t // tk, tk, npair, 2).transpose(0, 3, 1, 4, 2)
    return pl.pallas_call(
        functools.partial(_fox_prompt_kernel, tq=tq, tk=tk),
        grid=(b, npair, t // tq),
        in_specs=[pl.BlockSpec((None, tq, LANES), lambda i, h, j: (i, j, h)),
                  pl.BlockSpec((None, t, LANES), lambda i, h, j: (i, 0, h)),
                  pl.BlockSpec((None, t, LANES), lambda i, h, j: (i, 0, h)),
                  pl.BlockSpec((None, None, tq, 2), lambda i, h, j: (i, h, j, 0)),
                  pl.BlockSpec((None, None, t // tk, 2, tk), lambda i, h, j: (i, h, 0, 0, 0))],
        out_specs=pl.BlockSpec((None, tq, LANES), lambda i, h, j: (i, j, h)),
        out_shape=jax.ShapeDtypeStruct((b, t, w), F32),
        scratch_shapes=[pltpu.VMEM((tq, 1), F32), pltpu.VMEM((tq, 1), F32), pltpu.VMEM((tq, HEAD_DIM), F32)],
        compiler_params=_cparams(("arbitrary",) * 3), name="fox_prompt")(q, k, v, fq, fk)


def _diff_lambda(lq1, lk1, lq2, lk2, lam_init):
    return (jnp.exp(jnp.sum(lq1 * lk1, axis=-1, keepdims=True))
            - jnp.exp(jnp.sum(lq2 * lk2, axis=-1, keepdims=True)) + lam_init)


def _diff_prompt_kernel(slopes_ref, q_ref, k_ref, v_ref, lq1, lk1, lq2, lk2, subg_ref, o_ref,
                        m_sc, l_sc, acc_sc, *, tq, tk, lam_init):
    slope = slopes_ref[pl.program_id(1)]
    q0 = pl.program_id(2) * tq
    qpos = q0 + _iota((tq, 1), 0)
    n_k = (q0 + tq + tk - 1) // tk
    m_sc[...] = jnp.full(m_sc.shape, NEG_INF, F32)
    l_sc[...] = jnp.zeros(l_sc.shape, F32)
    acc_sc[...] = jnp.zeros(acc_sc.shape, F32)
    qs = [(q_ref[:, c * HEAD_DIM:(c + 1) * HEAD_DIM] * SCALE).astype(BF16) for c in range(2)]

    def body(kj, carry):
        k0 = pl.multiple_of(kj * tk, tk)
        vt = v_ref[pl.ds(k0, tk), :].astype(BF16)
        dist = (qpos - (k0 + _iota((1, tk), 1))).astype(F32)
        bias = slope * dist
        for c in range(2):
            kt = k_ref[pl.ds(k0, tk), c * HEAD_DIM:(c + 1) * HEAD_DIM].astype(BF16)
            s = jnp.where(dist >= 0, _dot_nt(qs[c], kt) - bias, NEG_INF)
            _online_step(s, vt, m_sc.at[c], l_sc.at[c], acc_sc.at[c])
        return carry

    lax.fori_loop(0, n_k, body, 0)
    lam = _diff_lambda(lq1[...], lk1[...], lq2[...], lk2[...], lam_init)
    o = acc_sc[0] / l_sc[0] - lam * (acc_sc[1] / l_sc[1])
    o_ref[...] = _rms(o, subg_ref[...]) * (1.0 - lam_init)


def _diff_prompt(q, k, v, slopes, lams, subg, lam_init):
    b, t, w = q.shape
    tq = _row_tile(t, (256, 128))
    tk = _row_tile(t, (512, 256, 128))
    vec = pl.BlockSpec((1, HEAD_DIM), lambda i, h, j: (0, 0))
    return pl.pallas_call(
        functools.partial(_diff_prompt_kernel, tq=tq, tk=tk, lam_init=lam_init),
        grid=(b, A_HEADS, t // tq),
        in_specs=[pl.BlockSpec(memory_space=pltpu.SMEM),
                  pl.BlockSpec((None, tq, LANES), lambda i, h, j: (i, j, h)),
                  pl.BlockSpec((None, t, LANES), lambda i, h, j: (i, 0, h)),
                  pl.BlockSpec((None, t, LANES), lambda i, h, j: (i, 0, h)),
                  vec, vec, vec, vec, pl.BlockSpec((1, A_V), lambda i, h, j: (0, 0))],
        out_specs=pl.BlockSpec((None, tq, LANES), lambda i, h, j: (i, j, h)),
        out_shape=jax.ShapeDtypeStruct((b, t, w), F32),
        scratch_shapes=[pltpu.VMEM((2, tq, 1), F32), pltpu.VMEM((2, tq, 1), F32), pltpu.VMEM((2, tq, A_V), F32)],
        compiler_params=_cparams(("arbitrary",) * 3), name="diff_prompt")(
            slopes, q, k, v, *[x.reshape(1, HEAD_DIM) for x in lams], subg.reshape(1, A_V))


def _cmp_weights(pos, w1):
    ratio = CMP_BLOCK // CMP_STRIDE
    w1r = w1.reshape(ratio, CMP_STRIDE, 1, HEAD_DIM, 1, CMP_HIDDEN)
    eye = jnp.eye(B_KV, dtype=w1.dtype).reshape(1, 1, B_KV, 1, B_KV, 1)
    big = (w1r * eye).reshape(ratio, CMP_STRIDE * B_KV * HEAD_DIM, B_KV * CMP_HIDDEN)
    w1big = jnp.concatenate([big[r] for r in range(ratio)], axis=1)
    posr = jnp.broadcast_to(pos.reshape(ratio, CMP_STRIDE, 1, HEAD_DIM), (ratio, CMP_STRIDE, B_KV, HEAD_DIM))
    posrows = jnp.pad(posr.reshape(ratio, CMP_STRIDE * B_KV * HEAD_DIM), ((0, 8 - ratio), (0, 0)))
    return w1big.astype(BF16), posrows


def _cmp_a_kernel(*refs, n_pages, n_prefetch=0):
    refs = refs[n_prefetch:]
    w_ref, o_ref = refs[n_pages], refs[n_pages + 1]
    rows = o_ref.shape[0] // n_pages
    for i in range(n_pages):
        o_ref[i * rows:(i + 1) * rows, :] = _dot(refs[i][...].astype(BF16), w_ref[...])


def _cmp_b_kernel(a_ref, posrows_ref, w1_ref, w2_ref, o_ref):
    n = a_ref.shape[0]
    hw = B_KV * CMP_HIDDEN
    pc = _dot(posrows_ref[...].astype(BF16), w1_ref[...])
    posc = pc[0:1, 0:hw] + pc[1:2, hw:2 * hw]
    a = a_ref[...]
    nxt = pltpu.roll(a[:, hw:2 * hw], n - 1, 0)
    pre = a[:, 0:hw] + nxt + posc
    act = (pre * jax.nn.sigmoid(pre)).astype(BF16)
    w2 = w2_ref[...]
    for g in range(B_KV):
        o_ref[:, g * HEAD_DIM:(g + 1) * HEAD_DIM] = _dot(act[:, g * CMP_HIDDEN:(g + 1) * CMP_HIDDEN], w2)


def _cmp_b(a, posrows, w1big, w2):
    b, n, wa = a.shape
    const = lambda i: (0, 0)
    return pl.pallas_call(
        _cmp_b_kernel, grid=(b,),
        in_specs=[pl.BlockSpec((None, n, wa), lambda i: (i, 0, 0)), pl.BlockSpec(posrows.shape, const),
                  pl.BlockSpec(w1big.shape, const), pl.BlockSpec(w2.shape, const)],
        out_specs=pl.BlockSpec((None, n, B_KV * HEAD_DIM), lambda i: (i, 0, 0)),
        out_shape=jax.ShapeDtypeStruct((b, n, B_KV * HEAD_DIM), F32),
        compiler_params=_cparams(("arbitrary",)), name="cmp_b")(a, posrows, w1big, w2)


def _compress_prompt(rows, pos, w1, w2):
    b, t, w = rows.shape
    n = t // CMP_STRIDE
    w1big, posrows = _cmp_weights(pos, w1)
    view = rows.reshape(b * n, CMP_STRIDE * w)
    tm = _row_tile(b * n, (256, 128))
    a = pl.pallas_call(
        functools.partial(_cmp_a_kernel, n_pages=1), grid=(b * n // tm,),
        in_specs=[pl.BlockSpec((tm, CMP_STRIDE * w), lambda i: (i, 0)), pl.BlockSpec(w1big.shape, lambda i: (0, 0))],
        out_specs=pl.BlockSpec((tm, w1big.shape[1]), lambda i: (i, 0)),
        out_shape=jax.ShapeDtypeStruct((b * n, w1big.shape[1]), F32),
        compiler_params=_cparams(("arbitrary",)), name="cmp_a")(view, w1big)
    return _cmp_b(a.reshape(b, n, -1), posrows, w1big, w2.astype(BF16))


def _compress_paged(cache, layer, page_table, pos, w1, w2, pages_per_step):
    _, n_pool, page, w = cache.shape
    b, n_pages = page_table.shape
    cpp = page // CMP_STRIDE
    view = cache.reshape(cache.shape[0], n_pool, cpp, CMP_STRIDE * w)
    w1big, posrows = _cmp_weights(pos, w1)
    p = pages_per_step
    nsteps = n_pages // p
    page_specs = [pl.BlockSpec((None, None, cpp, CMP_STRIDE * w),
                               lambda i, c, pt, j=j: (layer, pt[i, c * p + j], 0, 0)) for j in range(p)]
    a = pl.pallas_call(
        functools.partial(_cmp_a_kernel, n_pages=p, n_prefetch=1),
        grid_spec=pltpu.PrefetchScalarGridSpec(
            num_scalar_prefetch=1, grid=(b, nsteps),
            in_specs=page_specs + [pl.BlockSpec(w1big.shape, lambda i, c, pt: (0, 0))],
            out_specs=pl.BlockSpec((None, p * cpp, w1big.shape[1]), lambda i, c, pt: (i, c, 0))),
        out_shape=jax.ShapeDtypeStruct((b, n_pages * cpp, w1big.shape[1]), F32),
        compiler_params=_cparams(("arbitrary", "arbitrary")), name="cmp_a_paged")(
            page_table, *([view] * p), w1big)
    return _cmp_b(a, posrows, w1big, w2.astype(BF16))


def _overlap(n_cmp_rows, n_blocks):
    c0 = _iota((n_cmp_rows, n_blocks), 0) * CMP_STRIDE
    s0 = _iota((n_cmp_rows, n_blocks), 1) * SEL_BLOCK
    return ((c0 < s0 + SEL_BLOCK) & (c0 + CMP_BLOCK > s0)).astype(BF16)


def _top_blocks(score, n_pick):
    blk = _iota(score.shape, 1)
    n_blocks = score.shape[1]

    def body(_, carry):
        cur, picked = carry
        best = jnp.max(cur, axis=-1, keepdims=True)
        idx = jnp.min(jnp.where(cur == best, blk, n_blocks), axis=-1, keepdims=True)
        hit = blk == idx
        return jnp.where(hit, -2.0, cur), jnp.where(hit, 1.0, picked)

    _, picked = lax.fori_loop(0, n_pick, body, (score, jnp.zeros(score.shape, F32)))
    return picked


def _nsa_prompt_kernel(slopes_ref, q_ref, gate_ref, kc_ref, vc_ref, sk_ref, sv_ref, wk_ref, wv_ref, o_ref,
                       m_sc, l_sc, acc_sc, *, tq, tk, t, lw):
    q0 = pl.multiple_of(pl.program_id(1) * tq, tq)
    rows = B_REP * tq
    qpos1 = q0 + _iota((tq, 1), 0)
    qpos = jnp.concatenate([qpos1] * B_REP, axis=0)
    n_cmp_rows = kc_ref.shape[0]
    n_cmp = n_cmp_rows - (CMP_BLOCK // CMP_STRIDE - 1)
    n_blocks = t // SEL_BLOCK
    bpt = tk // SEL_BLOCK
    gates = jax.nn.sigmoid(gate_ref[...])
    cidx = _iota((1, n_cmp_rows), 1)
    c_end = cidx * CMP_STRIDE + (CMP_BLOCK - 1)
    blk = _iota((tq, n_blocks), 1)
    n_k = (q0 + tq + tk - 1) // tk
    w0 = pl.multiple_of(jnp.maximum(q0 + tq - lw, 0), tq)

    for g in range(B_KV):
        gs = slice(g * HEAD_DIM, (g + 1) * HEAD_DIM)
        q32 = jnp.concatenate(
            [q_ref[:, (g * B_REP + r) * HEAD_DIM:(g * B_REP + r + 1) * HEAD_DIM] for r in range(B_REP)], axis=0) * SCALE
        qb = q32.astype(BF16)
        slope = jnp.concatenate(
            [jnp.full((tq, 1), slopes_ref[g * B_REP + r], F32) for r in range(B_REP)], axis=0)

        dc = (qpos - c_end).astype(F32)
        sc = _dot_nt_precise(q32, kc_ref[:, gs]) - slope * dc
        pc = _masked_softmax_rows(sc, (dc >= 0) & (cidx < n_cmp))
        o_cmp = _dot(pc.astype(BF16), vc_ref[:, gs].astype(BF16))

        psum = pc[0:tq]
        for r in range(1, B_REP):
            psum = psum + pc[r * tq:(r + 1) * tq]
        imp = _dot_exact_rhs(psum, _overlap(n_cmp_rows, n_blocks))
        valid = blk * SEL_BLOCK <= qpos1
        forced = (blk == qpos1 // SEL_BLOCK) | (blk == 0)
        score = jnp.where(forced, 1e30, jnp.where(valid, imp, -1.0))
        picked = _top_blocks(score, min(SEL_TOP, n_blocks)).astype(BF16)

        m_sc[...] = jnp.full(m_sc.shape, NEG_INF, F32)
        l_sc[...] = jnp.zeros(l_sc.shape, F32)
        acc_sc[...] = jnp.zeros(acc_sc.shape, F32)

        def body(kj, carry, gs=gs, qb=qb, slope=slope, picked=picked):
            k0 = pl.multiple_of(kj * tk, tk)
            kt = sk_ref[pl.ds(k0, tk), gs].astype(BF16)
            vt = sv_ref[pl.ds(k0, tk), gs].astype(BF16)
            expand = (_iota((n_blocks, tk), 0) == kj * bpt + _iota((n_blocks, tk), 1) // SEL_BLOCK).astype(BF16)
            sel1 = _dot(picked, expand)
            sel = jnp.concatenate([sel1] * B_REP, axis=0)
            ds = (qpos - (k0 + _iota((1, tk), 1))).astype(F32)
            s = jnp.where((sel > 0.5) & (ds >= 0), _dot_nt(qb, kt) - slope * ds, NEG_INF)
            _online_step(s, vt, m_sc, l_sc, acc_sc)
            return carry

        lax.fori_loop(0, n_k, body, 0)
        o_sel = acc_sc[...] / jnp.maximum(l_sc[...], 1e-30)

        kw = wk_ref[pl.ds(w0, lw), gs].astype(BF16)
        vw = wv_ref[pl.ds(w0, lw), gs].astype(BF16)
        dw = qpos - (w0 + _iota((1, lw), 1))
        sw = _dot_nt(qb, kw) - slope * dw.astype(F32)
        pw = _masked_softmax_rows(sw, (dw >= 0) & (dw < WINDOW))
        o_win = _dot(pw.astype(BF16), vw)

        for r in range(B_REP):
            h = g * B_REP + r
            rs = slice(r * tq, (r + 1) * tq)
            o_ref[:, h * HEAD_DIM:(h + 1) * HEAD_DIM] = (
                gates[:, 3 * h:3 * h + 1] * o_cmp[rs] + gates[:, 3 * h + 1:3 * h + 2] * o_sel[rs]
                + gates[:, 3 * h + 2:3 * h + 3] * o_win[rs])


def _nsa_prompt(q, gate, kc, vc, sk, sv, wk, wv, slopes):
    b, t, w = q.shape
    tq = 128
    tk = _row_tile(t, (512, 256, 128))
    lw = min(WINDOW + tq, t)
    kvw = B_KV * HEAD_DIM
    ncr = kc.shape[1]
    full = lambda width, rows: pl.BlockSpec((None, rows, width), lambda i, j: (i, 0, 0))
    return pl.pallas_call(
        functools.partial(_nsa_prompt_kernel, tq=tq, tk=tk, t=t, lw=lw),
        grid=(b, t // tq),
        in_specs=[pl.BlockSpec(memory_space=pltpu.SMEM),
                  pl.BlockSpec((None, tq, w), lambda i, j: (i, j, 0)),
                  pl.BlockSpec((None, tq, LANES), lambda i, j: (i, j, 0)),
                  full(kvw, ncr), full(kvw, ncr), full(kvw, t), full(kvw, t), full(kvw, t), full(kvw, t)],
        out_specs=pl.BlockSpec((None, tq, w), lambda i, j: (i, j, 0)),
        out_shape=jax.ShapeDtypeStruct((b, t, w), F32),
        scratch_shapes=[pltpu.VMEM((B_REP * tq, 1), F32), pltpu.VMEM((B_REP * tq, 1), F32),
                        pltpu.VMEM((B_REP * tq, HEAD_DIM), F32)],
        compiler_params=_cparams(("arbitrary", "arbitrary")), name="nsa_prompt")(
            slopes, q, gate, kc, vc, sk, sv, wk, wv)


NEW_PAD = 16


def _page_specs(n, page, width, layer, idx_fn):
    return [pl.BlockSpec((None, None, page, width),
                         lambda i, c, pt, j=j: (layer, pt[i, idx_fn(c, j)], 0, 0)) for j in range(n)]


def _flash_pages(qb, k_refs, v_refs, bias_fn, m_sc, l_sc, acc_sc):
    s = jnp.concatenate([_dot_nt(qb, kr[...].astype(BF16)) for kr in k_refs], axis=1)
    s = bias_fn(s)
    page = k_refs[0].shape[0]
    m_old = m_sc[...]
    m_new = jnp.maximum(m_old, jnp.max(s, axis=-1, keepdims=True))
    alpha = jnp.exp(m_old - m_new)
    p = jnp.exp(s - m_new)
    l_sc[...] = alpha * l_sc[...] + jnp.sum(p, axis=-1, keepdims=True)
    pv = None
    for j, vr in enumerate(v_refs):
        term = _dot(p[:, j * page:(j + 1) * page].astype(BF16), vr[...].astype(BF16))
        pv = term if pv is None else pv + term
    acc_sc[...] = alpha * acc_sc[...] + pv
    m_sc[...] = m_new


def _pad_new(x, b, ts):
    return jnp.pad(x.reshape(b, ts, x.shape[-1]), ((0, 0), (0, NEW_PAD - ts), (0, 0)))


def _diff_dec_kernel(pt_ref, qbd_ref, info_ref, kn_ref, vn_ref, lq1, lk1, lq2, lk2, subg_ref, *rest,
                     p, page, n_past, n_new, lam_init):
    k_refs, v_refs = rest[:p], rest[p:2 * p]
    o_ref, m_sc, l_sc, acc_sc = rest[2 * p:]
    c = pl.program_id(1)

    @pl.when(c == 0)
    def _():
        m_sc[...] = jnp.full(m_sc.shape, NEG_INF, F32)
        l_sc[...] = jnp.zeros(l_sc.shape, F32)
        acc_sc[...] = jnp.zeros(acc_sc.shape, F32)

    qb = (qbd_ref[...] * SCALE).astype(BF16)
    slope = info_ref[:, 0:1]
    qpos = info_ref[:, 1:2]
    kpos = (c * (p * page) + _iota((1, p * page), 1)).astype(F32)
    _flash_pages(qb, k_refs, v_refs, lambda s: s - slope * (qpos - kpos), m_sc, l_sc, acc_sc)

    @pl.when(c == pl.num_programs(1) - 1)
    def _():
        idx = _iota((1, NEW_PAD), 1)
        dist = qpos - (n_past + idx).astype(F32)
        s = _dot_nt(qb, kn_ref[...].astype(BF16))
        s = jnp.where((dist >= 0) & (idx < n_new), s - slope * dist, NEG_INF)
        _online_step(s, vn_ref[...].astype(BF16), m_sc, l_sc, acc_sc)
        full = acc_sc[...] / l_sc[...]
        rows = full.shape[0]
        rowh = (_iota((rows, 1), 0) // n_new) % A_HEADS
        o = jnp.zeros((rows, A_V), F32)
        for h in range(A_HEADS):
            o = o + jnp.where(rowh == h, full[:, h * A_V:(h + 1) * A_V], 0.0)
        half = rows // 2
        lam = _diff_lambda(lq1[...], lk1[...], lq2[...], lk2[...], lam_init)
        o = o[0:half] - lam * o[half:rows]
        o_ref[...] = _rms(o, subg_ref[...]) * (1.0 - lam_init)


def _diff_dec(q, kn, vn, cache_k, cache_v, layer, page_table, slopes, lams, subg, lam_init, pages_per_step):
    b, n_pages = page_table.shape
    ts = q.shape[0] // b
    page, kw = cache_k.shape[2], cache_k.shape[3]
    vw = cache_v.shape[3]
    n_past = n_pages * page
    p = pages_per_step
    rows = 2 * A_HEADS * ts
    q5 = q.reshape(b, ts, A_HEADS, 2, HEAD_DIM).transpose(0, 3, 2, 1, 4)
    eye_h = jnp.eye(A_HEADS, dtype=F32).reshape(1, 1, A_HEADS, 1, A_HEADS, 1, 1)
    eye_c = jnp.eye(2, dtype=F32).reshape(1, 2, 1, 1, 1, 2, 1)
    qbd = (q5[:, :, :, :, None, None, :] * eye_h * eye_c).reshape(b, rows, kw)
    slope_r = jnp.broadcast_to(slopes.reshape(1, A_HEADS, 1), (2, A_HEADS, ts)).reshape(rows)
    qpos_r = jnp.broadcast_to((n_past + jnp.arange(ts, dtype=F32)).reshape(1, 1, ts), (2, A_HEADS, ts)).reshape(rows)
    info = jnp.pad(jnp.stack([slope_r, qpos_r], axis=1), ((0, 0), (0, LANES - 2)))
    const2 = lambda i, c, pt: (0, 0)
    per_b = lambda r, w: pl.BlockSpec((None, r, w), lambda i, c, pt: (i, 0, 0))
    vec = pl.BlockSpec((1, HEAD_DIM), const2)
    in_specs = ([per_b(rows, kw), pl.BlockSpec((rows, LANES), const2), per_b(NEW_PAD, kw), per_b(NEW_PAD, vw),
                 vec, vec, vec, vec, pl.BlockSpec((1, A_V), const2)]
                + _page_specs(p, page, kw, layer, lambda c, j: c * p + j)
                + _page_specs(p, page, vw, layer, lambda c, j: c * p + j))
    out = pl.pallas_call(
        functools.partial(_diff_dec_kernel, p=p, page=page, n_past=n_past, n_new=ts, lam_init=lam_init),
        grid_spec=pltpu.PrefetchScalarGridSpec(
            num_scalar_prefetch=1, grid=(b, n_pages // p), in_specs=in_specs,
            out_specs=pl.BlockSpec((None, rows // 2, A_V), lambda i, c, pt: (i, 0, 0)),
            scratch_shapes=[pltpu.VMEM((rows, 1), F32), pltpu.VMEM((rows, 1), F32), pltpu.VMEM((rows, vw), F32)]),
        out_shape=jax.ShapeDtypeStruct((b, rows // 2, A_V), F32),
        compiler_params=_cparams(("arbitrary", "arbitrary")), name="diff_dec")(
            page_table, qbd, info, _pad_new(kn, b, ts), _pad_new(vn, b, ts),
            *[x.reshape(1, HEAD_DIM) for x in lams], subg.reshape(1, A_V),
            *([cache_k] * p), *([cache_v] * p))
    return out.reshape(b, A_HEADS, ts, A_V).transpose(0, 2, 1, 3).reshape(b * ts, A_HEADS * A_V)


def _fox_dec_kernel(pt_ref, qbd_ref, pn_ref, pnt_ref, kn_ref, vn_ref, *rest, p, page, n_new):
    k_refs, v_refs, lf_refs = rest[:p], rest[p:2 * p], rest[2 * p:3 * p]
    o_ref, m_sc, l_sc, acc_sc, run_sc = rest[3 * p:]
    c = pl.program_id(1)
    qb = (qbd_ref[...] * SCALE).astype(BF16)
    rows = qb.shape[0]
    pn_col = pn_ref[:, 0:1]

    @pl.when(c == 0)
    def _():
        m_sc[...] = jnp.full(m_sc.shape, NEG_INF, F32)
        l_sc[...] = jnp.zeros(l_sc.shape, F32)
        acc_sc[...] = jnp.zeros(acc_sc.shape, F32)
        run_sc[...] = jnp.zeros(run_sc.shape, F32)
        idx = _iota((1, NEW_PAD), 1)
        tok = _iota((rows, 1), 0) // C_HEADS
        bias = pn_col - jnp.concatenate([pnt_ref[...]] * n_new, axis=0)
        s = _dot_nt(qb, kn_ref[...].astype(BF16)) + bias
        s = jnp.where((idx <= tok) & (idx < n_new), s, NEG_INF)
        _online_step(s, vn_ref[...].astype(BF16), m_sc, l_sc, acc_sc)

    later = (_iota((page, page), 0) > _iota((page, page), 1)).astype(BF16)
    run = run_sc[...]
    sufs = [None] * p
    for j in reversed(range(p)):
        lf = lf_refs[j][...]
        sufs[j] = _dot_exact_rhs(lf, later) + run
        run = run + jnp.sum(lf, axis=-1, keepdims=True)
    run_sc[...] = run
    suf = jnp.concatenate(sufs, axis=1)
    bias = jnp.concatenate([suf] * n_new, axis=0) + pn_col
    _flash_pages(qb, k_refs, v_refs, lambda s: s + bias, m_sc, l_sc, acc_sc)

    @pl.when(c == pl.num_programs(1) - 1)
    def _():
        full = acc_sc[...] / l_sc[...]
        rowh = _iota((rows, 1), 0) % C_HEADS
        o2 = jnp.zeros((rows, LANES), F32)
        for hp in range(C_HEADS // 2):
            o2 = o2 + jnp.where(rowh // 2 == hp, full[:, hp * LANES:(hp + 1) * LANES], 0.0)
        o_ref[...] = jnp.where(rowh % 2 == 0, o2[:, 0:HEAD_DIM], o2[:, HEAD_DIM:LANES])


def _fox_dec(q, kn, vn, cum_new, cache_k, cache_v, cache_lft, layer, page_table, pages_per_step):
    b, n_pages = page_table.shape
    ts = q.shape[0] // b
    page, w = cache_k.shape[2], cache_k.shape[3]
    p = pages_per_step
    nsteps = n_pages // p
    rows = ts * C_HEADS
    eye = jnp.eye(C_HEADS, dtype=F32).reshape(1, 1, C_HEADS, C_HEADS, 1)
    qbd = (q.reshape(b, ts, C_HEADS, 1, HEAD_DIM) * eye).reshape(b, rows, w)
    pn = jnp.pad(cum_new.reshape(b, rows, 1), ((0, 0), (0, 0), (0, LANES - 1)))
    pnt = jnp.pad(cum_new.transpose(0, 2, 1), ((0, 0), (0, 0), (0, NEW_PAD - ts)))
    per_b = lambda r, ww: pl.BlockSpec((None, r, ww), lambda i, c, pt: (i, 0, 0))
    rev = lambda c, j: (nsteps - 1 - c) * p + j
    in_specs = ([per_b(rows, w), per_b(rows, LANES), per_b(C_HEADS, NEW_PAD), per_b(NEW_PAD, w), per_b(NEW_PAD, w)]
                + _page_specs(p, page, w, layer, rev) + _page_specs(p, page, w, layer, rev)
                + _page_specs(p, C_HEADS, page, layer, rev))
    out = pl.pallas_call(
        functools.partial(_fox_dec_kernel, p=p, page=page, n_new=ts),
        grid_spec=pltpu.PrefetchScalarGridSpec(
            num_scalar_prefetch=1, grid=(b, nsteps), in_specs=in_specs,
            out_specs=pl.BlockSpec((None, rows, HEAD_DIM), lambda i, c, pt: (i, 0, 0)),
            scratch_shapes=[pltpu.VMEM((rows, 1), F32), pltpu.VMEM((rows, 1), F32), pltpu.VMEM((rows, w), F32),
                            pltpu.VMEM((C_HEADS, 1), F32)]),
        out_shape=jax.ShapeDtypeStruct((b, rows, HEAD_DIM), F32),
        compiler_params=_cparams(("arbitrary", "arbitrary")), name="fox_dec")(
            page_table, qbd, pn, pnt, _pad_new(kn, b, ts), _pad_new(vn, b, ts),
            *([cache_k] * p), *([cache_v] * p), *([cache_lft] * p))
    return out.reshape(b * ts, C_HEADS * HEAD_DIM)


def _nsa_dec_kernel(pt_ref, qbd_ref, info_ref, kc_ref, vc_ref, skn_ref, svn_ref, wkn_ref, wvn_ref, wk_ref, wv_ref,
                    *rest, p, page, n_past, n_new):
    k_refs, v_refs = rest[:p], rest[p:2 * p]
    o_ref, m_sc, l_sc, acc_sc, ocmp_sc, pick_sc = rest[2 * p:]
    c = pl.program_id(1)
    q32 = qbd_ref[...] * SCALE
    qb = q32.astype(BF16)
    rows = q32.shape[0]
    gt = rows // B_REP
    slope = info_ref[:, 0:1]
    qpos = info_ref[:, 1:2]
    in_g0 = (_iota((rows, 1), 0) % gt) < n_new
    pick_cols = lambda full: jnp.where(in_g0, full[:, 0:HEAD_DIM], full[:, HEAD_DIM:2 * HEAD_DIM])
    n_blocks = n_past // SEL_BLOCK

    @pl.when(c == 0)
    def _():
        m_sc[...] = jnp.full(m_sc.shape, NEG_INF, F32)
        l_sc[...] = jnp.zeros(l_sc.shape, F32)
        acc_sc[...] = jnp.zeros(acc_sc.shape, F32)
        n_cmp_rows = kc_ref.shape[0]
        n_cmp = n_cmp_rows - (CMP_BLOCK // CMP_STRIDE - 1)
        cidx = _iota((1, n_cmp_rows), 1)
        dc = qpos - (cidx * CMP_STRIDE + (CMP_BLOCK - 1)).astype(F32)
        sc = _dot_nt_precise(q32, kc_ref[...]) - slope * dc
        pc = _masked_softmax_rows(sc, (dc >= 0) & (cidx < n_cmp))
        ocmp_sc[...] = pick_cols(_dot(pc.astype(BF16), vc_ref[...].astype(BF16)))
        psum = pc[0:gt]
        for r in range(1, B_REP):
            psum = psum + pc[r * gt:(r + 1) * gt]
        imp = _dot_exact_rhs(psum, _overlap(n_cmp_rows, n_blocks))
        blk = _iota((gt, n_blocks), 1)
        qpos1 = qpos[0:gt].astype(jnp.int32)
        forced = (blk == 0) | (blk == qpos1 // SEL_BLOCK)
        score = jnp.where(forced, 1e30, jnp.where(blk * SEL_BLOCK <= qpos1, imp, -1.0))
        pick_sc[...] = _top_blocks(score, min(SEL_TOP, n_blocks + 1) - 1)

    keys = p * page
    kpos = c * keys + _iota((1, keys), 1)
    expand = (_iota((n_blocks, keys), 0) == (c * keys + _iota((n_blocks, keys), 1)) // SEL_BLOCK).astype(BF16)
    sel1 = _dot(pick_sc[...].astype(BF16), expand)
    sel = jnp.concatenate([sel1] * B_REP, axis=0)
    ds = qpos - kpos.astype(F32)
    _flash_pages(qb, k_refs, v_refs, lambda s: jnp.where((sel > 0.5) & (ds >= 0), s - slope * ds, NEG_INF),
                 m_sc, l_sc, acc_sc)

    @pl.when(c == pl.num_programs(1) - 1)
    def _():
        idx = _iota((1, NEW_PAD), 1)
        dn = qpos - (n_past + idx).astype(F32)
        new_ok = (dn >= 0) & (idx < n_new)
        s = jnp.where(new_ok, _dot_nt(qb, skn_ref[...].astype(BF16)) - slope * dn, NEG_INF)
        _online_step(s, svn_ref[...].astype(BF16), m_sc, l_sc, acc_sc)
        o_sel = pick_cols(acc_sc[...] / jnp.maximum(l_sc[...], 1e-30))
        keep = wk_ref.shape[0]
        wpos = n_past - keep + _iota((1, keep), 1)
        dw = qpos - wpos.astype(F32)
        ok1 = (dw >= 0) & (dw < WINDOW) & (wpos >= 0)
        ok2 = new_ok & (dn < WINDOW)
        s1 = jnp.where(ok1, _dot_nt(qb, wk_ref[...].astype(BF16)) - slope * dw, NEG_INF)
        s2 = jnp.where(ok2, _dot_nt(qb, wkn_ref[...].astype(BF16)) - slope * dn, NEG_INF)
        mx = jnp.maximum(jnp.max(s1, axis=-1, keepdims=True), jnp.max(s2, axis=-1, keepdims=True))
        e1 = jnp.where(ok1, jnp.exp(s1 - mx), 0.0)
        e2 = jnp.where(ok2, jnp.exp(s2 - mx), 0.0)
        den = jnp.maximum(jnp.sum(e1, axis=-1, keepdims=True) + jnp.sum(e2, axis=-1, keepdims=True), 1e-30)
        o_win = pick_cols((_dot(e1.astype(BF16), wv_ref[...].astype(BF16))
                           + _dot(e2.astype(BF16), wvn_ref[...].astype(BF16))) / den)
        gates = jax.nn.sigmoid(info_ref[:, 2:5])
        o_ref[...] = gates[:, 0:1] * ocmp_sc[...] + gates[:, 1:2] * o_sel + gates[:, 2:3] * o_win


def _nsa_dec(q, gate, kc, vc, skn, svn, wkn, wvn, win_k, win_v, cache_sk, cache_sv, layer, page_table, slopes,
             pages_per_step):
    b, n_pages = page_table.shape
    ts = q.shape[0] // b
    page, w = cache_sk.shape[2], cache_sk.shape[3]
    n_past = n_pages * page
    keep = win_k.shape[2]
    p = pages_per_step
    rows = B_HEADS * ts
    q5 = q.reshape(b, ts, B_KV, B_REP, HEAD_DIM).transpose(0, 3, 2, 1, 4)
    eye = jnp.eye(B_KV, dtype=F32).reshape(1, 1, B_KV, 1, B_KV, 1)
    qbd = (q5[:, :, :, :, None, :] * eye).reshape(b, rows, w)
    slope_r = jnp.broadcast_to(slopes.reshape(B_KV, B_REP).T.reshape(B_REP, B_KV, 1), (B_REP, B_KV, ts)).reshape(rows)
    qpos_r = jnp.broadcast_to((n_past + jnp.arange(ts, dtype=F32)).reshape(1, 1, ts), (B_REP, B_KV, ts)).reshape(rows)
    glog = gate[:, :3 * B_HEADS].reshape(b, ts, B_KV, B_REP, 3).transpose(0, 3, 2, 1, 4).reshape(b, rows, 3)
    info = jnp.concatenate([jnp.broadcast_to(jnp.stack([slope_r, qpos_r], axis=1)[None], (b, rows, 2)), glog], axis=2)
    info = jnp.pad(info, ((0, 0), (0, 0), (0, LANES - 5)))
    per_b = lambda r, ww: pl.BlockSpec((None, r, ww), lambda i, c, pt: (i, 0, 0))
    win = pl.BlockSpec((None, None, keep, w), lambda i, c, pt: (layer, i, 0, 0))
    ncr = kc.shape[1]
    in_specs = ([per_b(rows, w), per_b(rows, LANES), per_b(ncr, w), per_b(ncr, w)] + [per_b(NEW_PAD, w)] * 4
                + [win, win]
                + _page_specs(p, page, w, layer, lambda c, j: c * p + j)
                + _page_specs(p, page, w, layer, lambda c, j: c * p + j))
    out = pl.pallas_call(
        functools.partial(_nsa_dec_kernel, p=p, page=page, n_past=n_past, n_new=ts),
        grid_spec=pltpu.PrefetchScalarGridSpec(
            num_scalar_prefetch=1, grid=(b, n_pages // p), in_specs=in_specs,
            out_specs=pl.BlockSpec((None, rows, HEAD_DIM), lambda i, c, pt: (i, 0, 0)),
            scratch_shapes=[pltpu.VMEM((rows, 1), F32), pltpu.VMEM((rows, 1), F32), pltpu.VMEM((rows, w), F32),
                            pltpu.VMEM((rows, HEAD_DIM), F32), pltpu.VMEM((rows // B_REP, n_past // SEL_BLOCK), F32)]),
        out_shape=jax.ShapeDtypeStruct((b, rows, HEAD_DIM), F32),
        compiler_params=_cparams(("arbitrary", "arbitrary")), name="nsa_dec")(
            page_table, qbd, info, kc, vc, _pad_new(skn, b, ts), _pad_new(svn, b, ts), _pad_new(wkn, b, ts),
            _pad_new(wvn, b, ts), win_k, win_v, *([cache_sk] * p), *([cache_sv] * p))
    return out.reshape(b, B_REP, B_KV, ts, HEAD_DIM).transpose(0, 3, 2, 1, 4).reshape(b * ts, B_HEADS * HEAD_DIM)


def _alibi_slopes(n):
    return jnp.exp2(-8.0 * jnp.arange(1, n + 1, dtype=F32) / n)


def _pages_per_step(n_pages, want):
    p = min(want, n_pages)
    while n_pages % p:
        p -= 1
    return p


def kernel(x_prompt, x_sample, cache_diff_k, cache_diff_v, cache_nsa_cmp_k, cache_nsa_cmp_v, cache_nsa_sel_k, cache_nsa_sel_v, state_nsa_win_k, state_nsa_win_v, cache_fox_k, cache_fox_v, cache_fox_logf, page_table, norm_ffn1_g, ffn1_wg, ffn1_wu, ffn1_wd, norm_mix_g, norm_ffn2_g, ffn2_wg, ffn2_wu, ffn2_wd, even_w_in, even_w_out, diff_lambda_q1, diff_lambda_k1, diff_lambda_q2, diff_lambda_k2, diff_subln_g, cmp_pos_k, cmp_w1_k, cmp_w2_k, cmp_pos_v, cmp_w1_v, cmp_w2_v, odd_w_in, odd_b_f, odd_w_out, final_norm_g):
    b, t, d = x_prompt.shape
    bs, ts, _ = x_sample.shape
    depth = norm_ffn1_g.shape[0]
    n_pages = page_table.shape[1]
    n_pool, page = cache_diff_k.shape[1], cache_diff_k.shape[2]
    keep = state_nsa_win_k.shape[2]
    n_past = n_pages * page
    assert n_past % SEL_BLOCK == 0 and page % SEL_BLOCK == 0 and page % CMP_STRIDE == 0
    assert 0 < ts <= min(SEL_BLOCK, CMP_STRIDE - 1, NEW_PAD) and (A_HEADS * ts) % 8 == 0
    assert t % LANES == 0 and n_past >= WINDOW
    kvw = B_KV * HEAD_DIM

    slopes_a = _alibi_slopes(A_HEADS)
    slopes_b = _alibi_slopes(B_HEADS)
    flat = lambda c: c.reshape(c.shape[0], n_pool, page, -1)
    c_diff_k, c_diff_v = flat(cache_diff_k), flat(cache_diff_v)
    c_cmp_k, c_cmp_v = flat(cache_nsa_cmp_k), flat(cache_nsa_cmp_v)
    c_sel_k, c_sel_v = flat(cache_nsa_sel_k), flat(cache_nsa_sel_v)
    c_fox_k, c_fox_v = flat(cache_fox_k), flat(cache_fox_v)
    c_fox_lft = jnp.swapaxes(cache_fox_logf, 2, 3)
    win_k = state_nsa_win_k.reshape(state_nsa_win_k.shape[0], bs, keep, kvw)
    win_v = state_nsa_win_v.reshape(state_nsa_win_v.shape[0], bs, keep, kvw)

    a_w = A_HEADS * 2 * HEAD_DIM
    even_cuts = [a_w, a_w, A_HEADS * A_V, B_HEADS * HEAD_DIM] + [kvw] * 6 + [3 * B_HEADS]
    odd_cuts = [C_HEADS * HEAD_DIM] * 3 + [C_HEADS]

    def split_w(w, cuts):
        out, s = [], 0
        for c in cuts:
            out.append(_pad_cols(w[:, s:s + c], -(-c // LANES) * LANES).astype(BF16))
            s += c
        return out

    xp = x_prompt.reshape(b * t, d)
    xs = x_sample.reshape(bs * ts, d)
    even_p = [[] for _ in range(8)]
    even_s = [[] for _ in range(8)]
    odd_p = [[] for _ in range(3)]
    odd_s = [[] for _ in range(3)]

    for li in range(depth):
        w1 = [ffn1_wg[li].astype(BF16), ffn1_wu[li].astype(BF16), ffn1_wd[li].astype(BF16)]
        xp = _ffn(xp, norm_ffn1_g[li], *w1)
        xs = _ffn(xs, norm_ffn1_g[li], *w1)
        if li % 2 == 0:
            e = li // 2
            lam_init = 0.8 - 0.6 * math.exp(-0.3 * li)
            lams = (diff_lambda_q1[e], diff_lambda_k1[e], diff_lambda_q2[e], diff_lambda_k2[e])
            ws = split_w(even_w_in[e], even_cuts)
            wo = even_w_out[e].astype(BF16)
            wo_a, wo_b = wo[:A_HEADS * A_V], wo[A_HEADS * A_V:]
            aq, ak, av, bq, ck, cv, sk, sv, wk, wv, bg = _norm_proj(xp, norm_mix_g[li], ws)
            r3 = lambda a: a.reshape(b, t, a.shape[-1])
            oa = _diff_prompt(r3(aq), r3(ak), r3(av), slopes_a, lams, diff_subln_g[e], lam_init)
            kc = _compress_prompt(r3(ck), cmp_pos_k[e], cmp_w1_k[e], cmp_w2_k[e])
            vc = _compress_prompt(r3(cv), cmp_pos_v[e], cmp_w1_v[e], cmp_w2_v[e])
            ob = _nsa_prompt(r3(bq), r3(bg), kc, vc, r3(sk), r3(sv), r3(wk), r3(wv), slopes_b)
            xp = _out_proj(xp, [oa.reshape(b * t, -1), ob.reshape(b * t, -1)], [wo_a, wo_b])

            def last_rows(a):
                a = jnp.pad(r3(a), ((0, 0), (max(0, keep - t), 0), (0, 0)))
                return a[:, a.shape[1] - keep:].reshape(b, keep, B_KV, HEAD_DIM)

            rows_p = (ak.reshape(b, t, A_HEADS, 2, HEAD_DIM), av.reshape(b, t, A_HEADS, A_V),
                      ck.reshape(b, t, B_KV, HEAD_DIM), cv.reshape(b, t, B_KV, HEAD_DIM),
                      sk.reshape(b, t, B_KV, HEAD_DIM), sv.reshape(b, t, B_KV, HEAD_DIM), last_rows(wk), last_rows(wv))
            for lst, r in zip(even_p, rows_p):
                lst.append(r)
            aq, ak, av, bq, ck, cv, sk, sv, wk, wv, bg = _norm_proj(xs, norm_mix_g[li], ws)
            oa = _diff_dec(aq, ak, av, c_diff_k, c_diff_v, e, page_table, slopes_a, lams, diff_subln_g[e], lam_init,
                           _pages_per_step(n_pages, 16))
            pps = _pages_per_step(n_pages, 16)
            kc = _compress_paged(c_cmp_k, e, page_table, cmp_pos_k[e], cmp_w1_k[e], cmp_w2_k[e], pps)
            vc = _compress_paged(c_cmp_v, e, page_table, cmp_pos_v[e], cmp_w1_v[e], cmp_w2_v[e], pps)
            ob = _nsa_dec(bq, bg, kc, vc, sk, sv, wk, wv, win_k, win_v, c_sel_k, c_sel_v, e, page_table, slopes_b, pps)
            xs = _out_proj(xs, [oa, ob], [wo_a, wo_b])
            s3 = lambda a: a.reshape(bs, ts, a.shape[-1])
            new_win = lambda buf, a: jnp.concatenate([buf[e], s3(a)], axis=1)[:, ts:].reshape(bs, keep, B_KV, HEAD_DIM)
            rows_s = (ak.reshape(bs, ts, A_HEADS, 2, HEAD_DIM), av.reshape(bs, ts, A_HEADS, A_V),
                      ck.reshape(bs, ts, B_KV, HEAD_DIM), cv.reshape(bs, ts, B_KV, HEAD_DIM),
                      sk.reshape(bs, ts, B_KV, HEAD_DIM), sv.reshape(bs, ts, B_KV, HEAD_DIM),
                      new_win(win_k, wk), new_win(win_v, wv))
            for lst, r in zip(even_s, rows_s):
                lst.append(r)
        else:
            o = li // 2
            ws = split_w(odd_w_in[o], odd_cuts)
            wo = odd_w_out[o].astype(BF16)
            q, k, v, f = _norm_proj(xp, norm_mix_g[li], ws)
            r3 = lambda a: a.reshape(b, t, a.shape[-1])
            logf, cum = _logf_cum(r3(f), odd_b_f[o])
            om = _fox_prompt(r3(q), r3(k), r3(v), cum)
            xp = _out_proj(xp, [om.reshape(b * t, -1)], [wo])
            for lst, r in zip(odd_p, (k.reshape(b, t, C_HEADS, HEAD_DIM), v.reshape(b, t, C_HEADS, HEAD_DIM), logf)):
                lst.append(r)
            q, k, v, f = _norm_proj(xs, norm_mix_g[li], ws)
            logf, cum = _logf_cum(f.reshape(bs, ts, -1), odd_b_f[o])
            om = _fox_dec(q, k, v, cum, c_fox_k, c_fox_v, c_fox_lft, o, page_table, _pages_per_step(n_pages, 8))
            xs = _out_proj(xs, [om], [wo])
            for lst, r in zip(odd_s, (k.reshape(bs, ts, C_HEADS, HEAD_DIM), v.reshape(bs, ts, C_HEADS, HEAD_DIM), logf)):
                lst.append(r)
        w2 = [ffn2_wg[li].astype(BF16), ffn2_wu[li].astype(BF16), ffn2_wd[li].astype(BF16)]
        fin = final_norm_g if li == depth - 1 else None
        xp = _ffn(xp, norm_ffn2_g[li], *w2, final_g=fin)
        xs = _ffn(xs, norm_ffn2_g[li], *w2, final_g=fin)

    stk = lambda lst: jnp.stack(lst, 0)
    ep = [stk(l) for l in even_p]
    es = [stk(l) for l in even_s]
    op = [stk(l) for l in odd_p]
    os_ = [stk(l) for l in odd_s]
    out = [xp.reshape(b, t, d), xs.reshape(bs, ts, d)]
    for p_, s_ in zip(ep, es):
        out += [p_, s_]
    for p_, s_ in zip(op, os_):
        out += [p_, s_]
    return tuple(out)
```

```python
import functools
import math

import jax
import jax.numpy as jnp
from jax import lax
from jax.experimental import pallas as pl
from jax.experimental.pallas import tpu as pltpu

HEAD_DIM = 64
A_HEADS = 4
A_V = 2 * HEAD_DIM
B_HEADS = 8
B_KV = 2
B_REP = B_HEADS // B_KV
CMP_BLOCK = 32
CMP_STRIDE = 16
CMP_HIDDEN = 128
SEL_BLOCK = 64
SEL_TOP = 16
WINDOW = 512
C_HEADS = 16
RMS_EPS = 1e-6
NEG_INF = -1e30
SCALE = HEAD_DIM ** -0.5

LANES = 128
V7X_VMEM_LIMIT_BYTES = 56 * 1024 * 1024

F32 = jnp.float32
BF16 = jnp.bfloat16


def _cparams(sem):
    return pltpu.CompilerParams(dimension_semantics=sem, vmem_limit_bytes=V7X_VMEM_LIMIT_BYTES)


def _dot(a, b):
    return jnp.dot(a, b, preferred_element_type=F32)


def _dot_nt(a, b):
    return lax.dot_general(a, b, (((1,), (1,)), ((), ())), preferred_element_type=F32)


def _split3(x):
    hi = x.astype(BF16)
    r1 = x - hi.astype(F32)
    mid = r1.astype(BF16)
    lo = (r1 - mid.astype(F32)).astype(BF16)
    return hi, mid, lo


def _dot_exact_rhs(x, m_bf16):
    hi, mid, lo = _split3(x)
    return _dot(hi, m_bf16) + _dot(mid, m_bf16) + _dot(lo, m_bf16)


def _dot_nt_precise(a, b):
    ah = a.astype(BF16)
    al = (a - ah.astype(F32)).astype(BF16)
    bh = b.astype(BF16)
    bl = (b - bh.astype(F32)).astype(BF16)
    return _dot_nt(ah, bh) + _dot_nt(ah, bl) + _dot_nt(al, bh)


def _rms(x, g):
    ms = jnp.mean(x * x, axis=-1, keepdims=True)
    return x * lax.rsqrt(ms + RMS_EPS) * g


def _iota(shape, dim):
    return lax.broadcasted_iota(jnp.int32, shape, dim)


def _masked_softmax_rows(s, mask):
    s = jnp.where(mask, s, NEG_INF)
    e = jnp.where(mask, jnp.exp(s - jnp.max(s, axis=-1, keepdims=True)), 0.0)
    return e / jnp.maximum(jnp.sum(e, axis=-1, keepdims=True), 1e-30)


def _online_step(s, v_bf16, m_ref, l_ref, acc_ref):
    m_old = m_ref[...]
    m_new = jnp.maximum(m_old, jnp.max(s, axis=-1, keepdims=True))
    alpha = jnp.exp(m_old - m_new)
    p = jnp.exp(s - m_new)
    l_ref[...] = alpha * l_ref[...] + jnp.sum(p, axis=-1, keepdims=True)
    acc_ref[...] = alpha * acc_ref[...] + _dot(p.astype(BF16), v_bf16)
    m_ref[...] = m_new


def _row_tile(n, pref):
    for t in pref:
        if n % t == 0:
            return t
    return n


def _ffn_kernel(x_ref, g_ref, wg_ref, wu_ref, wd_ref, *rest, fc, nfc, has_final):
    if has_final:
        gf_ref, o_ref = rest
    else:
        (o_ref,) = rest
    x = x_ref[...]
    h = _rms(x, g_ref[...]).astype(BF16)
    acc = jnp.zeros(x.shape, F32)
    for c in range(nfc):
        sl = slice(c * fc, (c + 1) * fc)
        a = _dot(h, wg_ref[:, sl])
        u = _dot(h, wu_ref[:, sl])
        act = (a * jax.nn.sigmoid(a) * u).astype(BF16)
        acc = acc + _dot(act, wd_ref[sl, :])
    y = x + 0.5 * acc
    if has_final:
        y = _rms(y, gf_ref[...])
    o_ref[...] = y


def _ffn(x, g, wg, wu, wd, final_g=None):
    n, d = x.shape
    f = wg.shape[1]
    tm = _row_tile(n, (512, 256, 128))
    fc = _row_tile(f, (256, 128))
    const = lambda i: (0, 0)
    in_specs = [pl.BlockSpec((tm, d), lambda i: (i, 0)), pl.BlockSpec((1, d), const),
                pl.BlockSpec((d, f), const), pl.BlockSpec((d, f), const), pl.BlockSpec((f, d), const)]
    args = [x, g.reshape(1, d), wg, wu, wd]
    if final_g is not None:
        in_specs.append(pl.BlockSpec((1, d), const))
        args.append(final_g.reshape(1, d))
    return pl.pallas_call(
        functools.partial(_ffn_kernel, fc=fc, nfc=f // fc, has_final=final_g is not None),
        grid=(n // tm,), in_specs=in_specs, out_specs=pl.BlockSpec((tm, d), lambda i: (i, 0)),
        out_shape=jax.ShapeDtypeStruct((n, d), F32), compiler_params=_cparams(("arbitrary",)),
        name="ffn")(*args)


def _norm_proj_kernel(x_ref, g_ref, *refs, n_out):
    h = _rms(x_ref[...], g_ref[...]).astype(BF16)
    for w_ref, o_ref in zip(refs[:n_out], refs[n_out:]):
        o_ref[...] = _dot(h, w_ref[...])


def _norm_proj(x, g, ws):
    n, d = x.shape
    tm = _row_tile(n, (512, 256, 128))
    const = lambda i: (0, 0)
    in_specs = [pl.BlockSpec((tm, d), lambda i: (i, 0)), pl.BlockSpec((1, d), const)]
    in_specs += [pl.BlockSpec(w.shape, const) for w in ws]
    return pl.pallas_call(
        functools.partial(_norm_proj_kernel, n_out=len(ws)),
        grid=(n // tm,), in_specs=in_specs,
        out_specs=[pl.BlockSpec((tm, w.shape[1]), lambda i: (i, 0)) for w in ws],
        out_shape=[jax.ShapeDtypeStruct((n, w.shape[1]), F32) for w in ws],
        compiler_params=_cparams(("arbitrary",)), name="norm_proj")(x, g.reshape(1, d), *ws)


def _out_proj_kernel(x_ref, *refs, n_in):
    y = x_ref[...]
    for o_ref, w_ref in zip(refs[:n_in], refs[n_in:2 * n_in]):
        y = y + _dot(o_ref[...].astype(BF16), w_ref[...])
    refs[2 * n_in][...] = y


def _out_proj(x, os_, ws):
    n, d = x.shape
    tm = _row_tile(n, (512, 256, 128))
    const = lambda i: (0, 0)
    in_specs = [pl.BlockSpec((tm, d), lambda i: (i, 0))]
    in_specs += [pl.BlockSpec((tm, o.shape[1]), lambda i: (i, 0)) for o in os_]
    in_specs += [pl.BlockSpec(w.shape, const) for w in ws]
    return pl.pallas_call(
        functools.partial(_out_proj_kernel, n_in=len(os_)),
        grid=(n // tm,), in_specs=in_specs, out_specs=pl.BlockSpec((tm, d), lambda i: (i, 0)),
        out_shape=jax.ShapeDtypeStruct((n, d), F32), compiler_params=_cparams(("arbitrary",)),
        name="out_proj")(x, *os_, *ws)


def _pad_cols(w, width):
    return jnp.pad(w, ((0, 0), (0, width - w.shape[1])))


def _logf_cum_kernel(f_ref, b_ref, lf_ref, cum_ref, carry_ref, *, tc):
    @pl.when(pl.program_id(1) == 0)
    def _():
        carry_ref[...] = jnp.zeros(carry_ref.shape, F32)

    z = f_ref[...] + b_ref[...]
    lf = jnp.minimum(z, 0.0) - jnp.log1p(jnp.exp(-jnp.abs(z)))
    tri = (_iota((tc, tc), 0) >= _iota((tc, tc), 1)).astype(BF16)
    cum = _dot_exact_rhs_left(tri, lf) + carry_ref[...]
    lf_ref[...] = lf[:, :C_HEADS]
    cum_ref[...] = cum[:, :C_HEADS]
    carry_ref[...] = cum[tc - 1:tc, :]


def _dot_exact_rhs_left(m_bf16, x):
    hi, mid, lo = _split3(x)
    return _dot(m_bf16, hi) + _dot(m_bf16, mid) + _dot(m_bf16, lo)


def _logf_cum(f, b_f):
    b, t, _ = f.shape
    tc = _row_tile(t, (256, 128))
    bias = jnp.pad(b_f, (0, LANES - C_HEADS)).reshape(1, LANES)
    out = jax.ShapeDtypeStruct((b, t, C_HEADS), F32)
    return pl.pallas_call(
        functools.partial(_logf_cum_kernel, tc=tc),
        grid=(b, t // tc),
        in_specs=[pl.BlockSpec((None, tc, LANES), lambda i, j: (i, j, 0)), pl.BlockSpec((1, LANES), lambda i, j: (0, 0))],
        out_specs=[pl.BlockSpec((None, tc, C_HEADS), lambda i, j: (i, j, 0))] * 2,
        out_shape=[out, out], scratch_shapes=[pltpu.VMEM((1, LANES), F32)],
        compiler_params=_cparams(("arbitrary", "arbitrary")), name="logf_cum")(f, bias)


def _lanes(x, n):
    if n % LANES == 0:
        return x if n == LANES else jnp.tile(x, (1, n // LANES))
    return x[:, :n]


def _flash_update(s, v_aug, m_ref, acc_ref):
    m_prev = m_ref[...]
    m_next = jnp.maximum(m_prev, jnp.max(s, axis=-1, keepdims=True))
    p = jnp.exp(s - _lanes(m_next, s.shape[1]))
    alpha = jnp.exp(m_prev - m_next)
    acc_ref[...] = acc_ref[...] * _lanes(alpha, acc_ref.shape[1]) + _dot(p.astype(BF16), v_aug)
    m_ref[...] = m_next


def _causal_tiles(q0, tq, tk, tile_fn):
    n_full = q0 // tk
    n_k = (q0 + tq + tk - 1) // tk

    def run(masked):
        def body(kj, carry):
            tile_fn(kj, masked)
            return carry
        return body

    lax.fori_loop(0, n_full, run(False), 0)
    lax.fori_loop(n_full, n_k, run(True), 0)


def _fox_prompt_kernel(q_ref, k_ref, v_ref, fq_ref, fk_ref, o_ref, m_sc, acc_sc, *, tq, tk):
    q0 = pl.program_id(2) * tq
    qpos = q0 + _iota((tq, 1), 0)
    m_sc[...] = jnp.full(m_sc.shape, NEG_INF, F32)
    acc_sc[...] = jnp.zeros(acc_sc.shape, F32)
    subs = [slice(c * HEAD_DIM, (c + 1) * HEAD_DIM) for c in range(2)]
    qs = [(q_ref[:, cs] * SCALE).astype(BF16) for cs in subs]
    f0 = [fq_ref[0:1, c:c + 1] for c in range(2)]
    ones = jnp.ones((tk, HEAD_DIM), BF16)

    def tile(kj, masked):
        k0 = pl.multiple_of(kj * tk, tk)
        kt = k_ref[pl.ds(k0, tk), :].astype(BF16)
        vt = v_ref[pl.ds(k0, tk), :].astype(BF16)
        if masked:
            ok = qpos >= k0 + _iota((1, tk), 1)
        for c, cs in enumerate(subs):
            s = _dot_nt(qs[c], kt[:, cs]) + (f0[c] - fk_ref[kj, c:c + 1, :])
            if masked:
                s = jnp.where(ok, s, NEG_INF)
            _flash_update(s, jnp.concatenate([vt[:, cs], ones], axis=1), m_sc.at[c], acc_sc.at[c])

    _causal_tiles(q0, tq, tk, tile)
    for c, cs in enumerate(subs):
        o_ref[:, cs] = acc_sc[c][:, 0:HEAD_DIM] / acc_sc[c][:, HEAD_DIM:LANES]


def _fox_prompt(q, k, v, cum):
    b, t, w = q.shape
    tq = _row_tile(t, (512, 256, 128))
    tk = tq
    npair = C_HEADS // 2
    fq = cum.reshape(b, t, npair, 2).transpose(0, 2, 1, 3)
    fk = cum.reshape(b, t // tk, tk, npair, 2).transpose(0, 3, 1, 4, 2)
    return pl.pallas_call(
        functools.partial(_fox_prompt_kernel, tq=tq, tk=tk),
        grid=(b, npair, t // tq),
        in_specs=[pl.BlockSpec((None, tq, LANES), lambda i, h, j: (i, j, h)),
                  pl.BlockSpec((None, t, LANES), lambda i, h, j: (i, 0, h)),
                  pl.BlockSpec((None, t, LANES), lambda i, h, j: (i, 0, h)),
                  pl.BlockSpec((None, None, tq, 2), lambda i, h, j: (i, h, j, 0)),
                  pl.BlockSpec((None, None, t // tk, 2, tk), lambda i, h, j: (i, h, 0, 0, 0))],
        out_specs=pl.BlockSpec((None, tq, LANES), lambda i, h, j: (i, j, h)),
        out_shape=jax.ShapeDtypeStruct((b, t, w), F32),
        scratch_shapes=[pltpu.VMEM((2, tq, LANES), F32), pltpu.VMEM((2, tq, LANES), F32)],
        compiler_params=_cparams(("arbitrary",) * 3), name="fox_prompt")(q, k, v, fq, fk)


def _diff_lambda(lq1, lk1, lq2, lk2, lam_init):
    return (jnp.exp(jnp.sum(lq1 * lk1, axis=-1, keepdims=True))
            - jnp.exp(jnp.sum(lq2 * lk2, axis=-1, keepdims=True)) + lam_init)


def _diff_prompt_kernel(slopes_ref, q_ref, k_ref, v_ref, lq1, lk1, lq2, lk2, subg_ref, o_ref,
                        m_sc, acc_sc, *, tq, tk, lam_init):
    slope = slopes_ref[pl.program_id(1)]
    q0 = pl.program_id(2) * tq
    qpos = q0 + _iota((tq, 1), 0)
    m_sc[...] = jnp.full(m_sc.shape, NEG_INF, F32)
    acc_sc[...] = jnp.zeros(acc_sc.shape, F32)
    subs = [slice(c * HEAD_DIM, (c + 1) * HEAD_DIM) for c in range(2)]
    qs = [(q_ref[:, cs] * SCALE).astype(BF16) for cs in subs]
    ones = jnp.ones((tk, LANES), BF16)

    def tile(kj, masked):
        k0 = pl.multiple_of(kj * tk, tk)
        kt = k_ref[pl.ds(k0, tk), :].astype(BF16)
        v_aug = jnp.concatenate([v_ref[pl.ds(k0, tk), :].astype(BF16), ones], axis=1)
        kpos = k0 + _iota((1, tk), 1)
        bias = slope * (kpos - q0).astype(F32)
        if masked:
            ok = qpos >= kpos
        for c, cs in enumerate(subs):
            s = _dot_nt(qs[c], kt[:, cs]) + bias
            if masked:
                s = jnp.where(ok, s, NEG_INF)
            _flash_update(s, v_aug, m_sc.at[c], acc_sc.at[c])

    _causal_tiles(q0, tq, tk, tile)
    lam = _diff_lambda(lq1[...], lk1[...], lq2[...], lk2[...], lam_init)
    o = (acc_sc[0][:, 0:A_V] / acc_sc[0][:, A_V:2 * A_V]
         - lam * (acc_sc[1][:, 0:A_V] / acc_sc[1][:, A_V:2 * A_V]))
    o_ref[...] = _rms(o, subg_ref[...]) * (1.0 - lam_init)


def _diff_prompt(q, k, v, slopes, lams, subg, lam_init):
    b, t, w = q.shape
    tq = _row_tile(t, (512, 256, 128))
    tk = tq
    vec = pl.BlockSpec((1, HEAD_DIM), lambda i, h, j: (0, 0))
    return pl.pallas_call(
        functools.partial(_diff_prompt_kernel, tq=tq, tk=tk, lam_init=lam_init),
        grid=(b, A_HEADS, t // tq),
        in_specs=[pl.BlockSpec(memory_space=pltpu.SMEM),
                  pl.BlockSpec((None, tq, LANES), lambda i, h, j: (i, j, h)),
                  pl.BlockSpec((None, t, LANES), lambda i, h, j: (i, 0, h)),
                  pl.BlockSpec((None, t, LANES), lambda i, h, j: (i, 0, h)),
                  vec, vec, vec, vec, pl.BlockSpec((1, A_V), lambda i, h, j: (0, 0))],
        out_specs=pl.BlockSpec((None, tq, LANES), lambda i, h, j: (i, j, h)),
        out_shape=jax.ShapeDtypeStruct((b, t, w), F32),
        scratch_shapes=[pltpu.VMEM((2, tq, LANES), F32), pltpu.VMEM((2, tq, 2 * A_V), F32)],
        compiler_params=_cparams(("arbitrary",) * 3), name="diff_prompt")(
            slopes, q, k, v, *[x.reshape(1, HEAD_DIM) for x in lams], subg.reshape(1, A_V))


def _cmp_weights(pos, w1):
    ratio = CMP_BLOCK // CMP_STRIDE
    w1r = w1.reshape(ratio, CMP_STRIDE, 1, HEAD_DIM, 1, CMP_HIDDEN)
    eye = jnp.eye(B_KV, dtype=w1.dtype).reshape(1, 1, B_KV, 1, B_KV, 1)
    big = (w1r * eye).reshape(ratio, CMP_STRIDE * B_KV * HEAD_DIM, B_KV * CMP_HIDDEN)
    w1big = jnp.concatenate([big[r] for r in range(ratio)], axis=1)
    posr = jnp.broadcast_to(pos.reshape(ratio, CMP_STRIDE, 1, HEAD_DIM), (ratio, CMP_STRIDE, B_KV, HEAD_DIM))
    posrows = jnp.pad(posr.reshape(ratio, CMP_STRIDE * B_KV * HEAD_DIM), ((0, 8 - ratio), (0, 0)))
    return w1big.astype(BF16), posrows


def _cmp_a_kernel(x_ref, w_ref, o_ref):
    o_ref[...] = _dot(x_ref[...].astype(BF16), w_ref[...])


def _cmp_b_kernel(a_ref, posrows_ref, w1_ref, w2_ref, o_ref):
    n = a_ref.shape[0]
    hw = B_KV * CMP_HIDDEN
    pc = _dot(posrows_ref[...].astype(BF16), w1_ref[...])
    posc = pc[0:1, 0:hw] + pc[1:2, hw:2 * hw]
    a = a_ref[...]
    nxt = pltpu.roll(a[:, hw:2 * hw], n - 1, 0)
    pre = a[:, 0:hw] + nxt + posc
    act = (pre * jax.nn.sigmoid(pre)).astype(BF16)
    w2 = w2_ref[...]
    for g in range(B_KV):
        o_ref[:, g * HEAD_DIM:(g + 1) * HEAD_DIM] = _dot(act[:, g * CMP_HIDDEN:(g + 1) * CMP_HIDDEN], w2)


def _cmp_b(a, posrows, w1big, w2):
    b, n, wa = a.shape
    const = lambda i: (0, 0)
    return pl.pallas_call(
        _cmp_b_kernel, grid=(b,),
        in_specs=[pl.BlockSpec((None, n, wa), lambda i: (i, 0, 0)), pl.BlockSpec(posrows.shape, const),
                  pl.BlockSpec(w1big.shape, const), pl.BlockSpec(w2.shape, const)],
        out_specs=pl.BlockSpec((None, n, B_KV * HEAD_DIM), lambda i: (i, 0, 0)),
        out_shape=jax.ShapeDtypeStruct((b, n, B_KV * HEAD_DIM), F32),
        compiler_params=_cparams(("arbitrary",)), name="cmp_b")(a, posrows, w1big, w2)


def _compress_prompt(rows, pos, w1, w2):
    b, t, w = rows.shape
    n = t // CMP_STRIDE
    w1big, posrows = _cmp_weights(pos, w1)
    view = rows.reshape(b * n, CMP_STRIDE * w)
    tm = _row_tile(b * n, (256, 128))
    a = pl.pallas_call(
        _cmp_a_kernel, grid=(b * n // tm,),
        in_specs=[pl.BlockSpec((tm, CMP_STRIDE * w), lambda i: (i, 0)), pl.BlockSpec(w1big.shape, lambda i: (0, 0))],
        out_specs=pl.BlockSpec((tm, w1big.shape[1]), lambda i: (i, 0)),
        out_shape=jax.ShapeDtypeStruct((b * n, w1big.shape[1]), F32),
        compiler_params=_cparams(("arbitrary",)), name="cmp_a")(view, w1big)
    return _cmp_b(a.reshape(b, n, -1), posrows, w1big, w2.astype(BF16))


def _cmp_a_paged_kernel(pt_ref, *refs, n_pages):
    page_refs = refs[:n_pages]
    w_ref, o_ref, x_sc = refs[n_pages:]
    lanes, page = page_refs[0].shape
    for i in range(n_pages):
        x_sc[i * page:(i + 1) * page, :] = page_refs[i][...].T
    n_chunks = n_pages * page // CMP_STRIDE
    acc = None
    for j in range(0, CMP_STRIDE, 2):
        xs = jnp.concatenate([x_sc[pl.ds(j, n_chunks, stride=CMP_STRIDE), :],
                              x_sc[pl.ds(j + 1, n_chunks, stride=CMP_STRIDE), :]], axis=1).astype(BF16)
        term = _dot(xs, w_ref[j * lanes:(j + 2) * lanes, :])
        acc = term if acc is None else acc + term
    o_ref[...] = acc


def _compress_paged(cache_t, layer, page_table, pos, w1, w2, pages_per_step):
    _, n_pool, w, page = cache_t.shape
    b, n_pages = page_table.shape
    cpp = page // CMP_STRIDE
    w1big, posrows = _cmp_weights(pos, w1)
    p = pages_per_step
    a = pl.pallas_call(
        functools.partial(_cmp_a_paged_kernel, n_pages=p),
        grid_spec=pltpu.PrefetchScalarGridSpec(
            num_scalar_prefetch=1, grid=(b, n_pages // p),
            in_specs=_page_specs(p, w, page, layer, lambda c, j: c * p + j)
            + [pl.BlockSpec(w1big.shape, lambda i, c, pt: (0, 0))],
            out_specs=pl.BlockSpec((None, p * cpp, w1big.shape[1]), lambda i, c, pt: (i, c, 0)),
            scratch_shapes=[pltpu.VMEM((p * page, w), F32)]),
        out_shape=jax.ShapeDtypeStruct((b, n_pages * cpp, w1big.shape[1]), F32),
        compiler_params=_cparams(("arbitrary", "arbitrary")), name="cmp_a_paged")(
            page_table, *([cache_t] * p), w1big)
    return _cmp_b(a, posrows, w1big, w2.astype(BF16))


def _overlap(n_cmp_rows, n_blocks):
    c0 = _iota((n_cmp_rows, n_blocks), 0) * CMP_STRIDE
    s0 = _iota((n_cmp_rows, n_blocks), 1) * SEL_BLOCK
    return ((c0 < s0 + SEL_BLOCK) & (c0 + CMP_BLOCK > s0)).astype(BF16)


def _top_blocks(score, n_pick, axis):
    blk = _iota(score.shape, axis)
    n_blocks = score.shape[axis]

    def body(_, carry):
        cur, picked = carry
        best = jnp.max(cur, axis=axis, keepdims=True)
        idx = jnp.min(jnp.where(cur == best, blk, n_blocks), axis=axis, keepdims=True)
        hit = blk == idx
        return jnp.where(hit, -2.0, cur), jnp.where(hit, 1.0, picked)

    _, picked = lax.fori_loop(0, n_pick, body, (score, jnp.zeros(score.shape, F32)))
    return picked


def _nsa_prompt_kernel(slopes_ref, q_ref, gate_ref, kc_ref, vc_ref, sk_ref, sv_ref, wk_ref, wv_ref, o_ref,
                       m_sc, acc_sc, *, tq, tk, t, lw, nbp):
    q0 = pl.multiple_of(pl.program_id(1) * tq, tq)
    qpos1 = q0 + _iota((tq, 1), 0)
    qpos = jnp.concatenate([qpos1] * B_REP, axis=0)
    n_cmp_rows = kc_ref.shape[0]
    n_cmp = n_cmp_rows - (CMP_BLOCK // CMP_STRIDE - 1)
    n_blocks = t // SEL_BLOCK
    bpt = tk // SEL_BLOCK
    gates = jax.nn.sigmoid(gate_ref[...])
    cidx = _iota((1, n_cmp_rows), 1)
    c_end = cidx * CMP_STRIDE + (CMP_BLOCK - 1)
    w0 = pl.multiple_of(jnp.maximum(q0 + tq - lw, 0), tq)
    blk_t = _iota((nbp, tq), 0)
    qpos_t = q0 + _iota((1, tq), 1)
    valid_t = (blk_t * SEL_BLOCK <= qpos_t) & (blk_t < n_blocks)
    forced_t = (blk_t == qpos_t // SEL_BLOCK) | (blk_t == 0)
    overlap_t = (_iota((nbp, n_cmp_rows), 1) * CMP_STRIDE < _iota((nbp, n_cmp_rows), 0) * SEL_BLOCK + SEL_BLOCK) & (
        _iota((nbp, n_cmp_rows), 1) * CMP_STRIDE + CMP_BLOCK > _iota((nbp, n_cmp_rows), 0) * SEL_BLOCK)
    overlap_t = overlap_t.astype(BF16)
    ones = jnp.ones((tk, HEAD_DIM), BF16)

    for g in range(B_KV):
        gs = slice(g * HEAD_DIM, (g + 1) * HEAD_DIM)
        q32 = jnp.concatenate(
            [q_ref[:, (g * B_REP + r) * HEAD_DIM:(g * B_REP + r + 1) * HEAD_DIM] for r in range(B_REP)], axis=0) * SCALE
        qb = q32.astype(BF16)
        slopes = [slopes_ref[g * B_REP + r] for r in range(B_REP)]
        slope = jnp.concatenate([jnp.full((tq, 1), sl, F32) for sl in slopes], axis=0)

        dc = (qpos - c_end).astype(F32)
        sc = _dot_nt_precise(q32, kc_ref[:, gs]) - slope * dc
        pc = _masked_softmax_rows(sc, (dc >= 0) & (cidx < n_cmp))
        o_cmp = _dot(pc.astype(BF16), vc_ref[:, gs].astype(BF16))

        psum = pc[0:tq]
        for r in range(1, B_REP):
            psum = psum + pc[r * tq:(r + 1) * tq]
        hi, mid, lo = _split3(psum)
        imp_t = _dot_nt(overlap_t, hi) + _dot_nt(overlap_t, mid) + _dot_nt(overlap_t, lo)
        score_t = jnp.where(forced_t, 1e30, jnp.where(valid_t, imp_t, -1.0))
        picked = _top_blocks(score_t, min(SEL_TOP, n_blocks), 0).T.astype(BF16)

        m_sc[...] = jnp.full(m_sc.shape, NEG_INF, F32)
        acc_sc[...] = jnp.zeros(acc_sc.shape, F32)

        def tile(kj, masked, gs=gs, qb=qb, slopes=slopes, picked=picked):
            k0 = pl.multiple_of(kj * tk, tk)
            kt = sk_ref[pl.ds(k0, tk), gs].astype(BF16)
            v_aug = jnp.concatenate([sv_ref[pl.ds(k0, tk), gs].astype(BF16), ones], axis=1)
            expand = (_iota((nbp, tk), 0) == kj * bpt + _iota((nbp, tk), 1) // SEL_BLOCK).astype(BF16)
            keep = _dot(picked, expand) > 0.5
            kpos = k0 + _iota((1, tk), 1)
            if masked:
                keep = keep & (qpos1 >= kpos)
            krel = (kpos - q0).astype(F32)
            raw = _dot_nt(qb, kt)
            s = jnp.concatenate(
                [jnp.where(keep, raw[r * tq:(r + 1) * tq] + slopes[r] * krel, NEG_INF) for r in range(B_REP)], axis=0)
            _flash_update(s, v_aug, m_sc, acc_sc)

        _causal_tiles(q0, tq, tk, tile)
        o_sel = acc_sc[:, 0:HEAD_DIM] / jnp.maximum(acc_sc[:, HEAD_DIM:LANES], 1e-30)

        kw = wk_ref[pl.ds(w0, lw), gs].astype(BF16)
        vw = wv_ref[pl.ds(w0, lw), gs].astype(BF16)
        dw = qpos - (w0 + _iota((1, lw), 1))
        sw = _dot_nt(qb, kw) - slope * dw.astype(F32)
        pw = _masked_softmax_rows(sw, (dw >= 0) & (dw < WINDOW))
        o_win = _dot(pw.astype(BF16), vw)

        for r in range(B_REP):
            h = g * B_REP + r
            rs = slice(r * tq, (r + 1) * tq)
            o_ref[:, h * HEAD_DIM:(h + 1) * HEAD_DIM] = (
                gates[:, 3 * h:3 * h + 1] * o_cmp[rs] + gates[:, 3 * h + 1:3 * h + 2] * o_sel[rs]
                + gates[:, 3 * h + 2:3 * h + 3] * o_win[rs])


def _nsa_prompt(q, gate, kc, vc, sk, sv, wk, wv, slopes):
    b, t, w = q.shape
    tq = 128
    tk = _row_tile(t, (512, 256, 128))
    lw = min(WINDOW + tq, t)
    kvw = B_KV * HEAD_DIM
    ncr = kc.shape[1]
    nbp = -(-(t // SEL_BLOCK) // LANES) * LANES
    full = lambda width, rows: pl.BlockSpec((None, rows, width), lambda i, j: (i, 0, 0))
    return pl.pallas_call(
        functools.partial(_nsa_prompt_kernel, tq=tq, tk=tk, t=t, lw=lw, nbp=nbp),
        grid=(b, t // tq),
        in_specs=[pl.BlockSpec(memory_space=pltpu.SMEM),
                  pl.BlockSpec((None, tq, w), lambda i, j: (i, j, 0)),
                  pl.BlockSpec((None, tq, LANES), lambda i, j: (i, j, 0)),
                  full(kvw, ncr), full(kvw, ncr), full(kvw, t), full(kvw, t), full(kvw, t), full(kvw, t)],
        out_specs=pl.BlockSpec((None, tq, w), lambda i, j: (i, j, 0)),
        out_shape=jax.ShapeDtypeStruct((b, t, w), F32),
        scratch_shapes=[pltpu.VMEM((B_REP * tq, LANES), F32), pltpu.VMEM((B_REP * tq, LANES), F32)],
        compiler_params=_cparams(("arbitrary", "arbitrary")), name="nsa_prompt")(
            slopes, q, gate, kc, vc, sk, sv, wk, wv)


NEW_PAD = 16


def _page_specs(n, rows, cols, layer, idx_fn):
    return [pl.BlockSpec((None, None, rows, cols),
                         lambda i, c, pt, j=j: (layer, pt[i, idx_fn(c, j)], 0, 0)) for j in range(n)]


def _flash_pages(qb, k_refs, v_refs, bias_fn, m_sc, l_sc, acc_sc, key_major_v=None):
    s = jnp.concatenate([_dot(qb, kr[...].astype(BF16)) for kr in k_refs], axis=1)
    s = bias_fn(s)
    page = k_refs[0].shape[1]
    m_old = m_sc[...]
    m_new = jnp.maximum(m_old, jnp.max(s, axis=-1, keepdims=True))
    alpha = jnp.exp(m_old - m_new)
    p = jnp.exp(s - m_new)
    l_sc[...] = alpha * l_sc[...] + jnp.sum(p, axis=-1, keepdims=True)
    pv = None
    for j, vr in enumerate(v_refs):
        pj = p[:, j * page:(j + 1) * page].astype(BF16)
        term = _dot_nt(pj, vr[...].astype(BF16)) if key_major_v is None else _dot(pj, key_major_v(vr))
        pv = term if pv is None else pv + term
    acc_sc[...] = alpha * acc_sc[...] + pv
    m_sc[...] = m_new


def _pad_new(x, b, ts):
    return jnp.pad(x.reshape(b, ts, x.shape[-1]), ((0, 0), (0, NEW_PAD - ts), (0, 0)))


def _diff_dec_kernel(pt_ref, qbd_ref, info_ref, kn_ref, vn_ref, lq1, lk1, lq2, lk2, subg_ref, *rest,
                     p, page, n_past, n_new, lam_init):
    k_refs, v_refs = rest[:p], rest[p:2 * p]
    o_ref, m_sc, l_sc, acc_sc = rest[2 * p:]
    c = pl.program_id(1)

    @pl.when(c == 0)
    def _():
        m_sc[...] = jnp.full(m_sc.shape, NEG_INF, F32)
        l_sc[...] = jnp.zeros(l_sc.shape, F32)
        acc_sc[...] = jnp.zeros(acc_sc.shape, F32)

    qb = (qbd_ref[...] * SCALE).astype(BF16)
    slope = info_ref[:, 0:1]
    qpos = info_ref[:, 1:2]
    kpos = (c * (p * page) + _iota((1, p * page), 1)).astype(F32)
    key_major = lambda vr: jnp.concatenate(
        [vr[pl.ds(h, page, stride=A_HEADS), :] for h in range(A_HEADS)], axis=1).astype(BF16)
    _flash_pages(qb, k_refs, v_refs, lambda s: s - slope * (qpos - kpos), m_sc, l_sc, acc_sc, key_major)

    @pl.when(c == pl.num_programs(1) - 1)
    def _():
        idx = _iota((1, NEW_PAD), 1)
        dist = qpos - (n_past + idx).astype(F32)
        s = _dot_nt(qb, kn_ref[...].astype(BF16))
        s = jnp.where((dist >= 0) & (idx < n_new), s - slope * dist, NEG_INF)
        _online_step(s, vn_ref[...].astype(BF16), m_sc, l_sc, acc_sc)
        full = acc_sc[...] / l_sc[...]
        rows = full.shape[0]
        rowh = (_iota((rows, 1), 0) // n_new) % A_HEADS
        o = jnp.zeros((rows, A_V), F32)
        for h in range(A_HEADS):
            o = o + jnp.where(rowh == h, full[:, h * A_V:(h + 1) * A_V], 0.0)
        half = rows // 2
        lam = _diff_lambda(lq1[...], lk1[...], lq2[...], lk2[...], lam_init)
        o = o[0:half] - lam * o[half:rows]
        o_ref[...] = _rms(o, subg_ref[...]) * (1.0 - lam_init)


def _diff_dec(q, kn, vn, cache_k, cache_v, layer, page_table, slopes, lams, subg, lam_init, pages_per_step):
    b, n_pages = page_table.shape
    ts = q.shape[0] // b
    kw, page = cache_k.shape[2], cache_k.shape[3]
    vw = A_HEADS * A_V
    n_past = n_pages * page
    p = pages_per_step
    rows = 2 * A_HEADS * ts
    q5 = q.reshape(b, ts, A_HEADS, 2, HEAD_DIM).transpose(0, 3, 2, 1, 4)
    eye_h = jnp.eye(A_HEADS, dtype=F32).reshape(1, 1, A_HEADS, 1, A_HEADS, 1, 1)
    eye_c = jnp.eye(2, dtype=F32).reshape(1, 2, 1, 1, 1, 2, 1)
    qbd = (q5[:, :, :, :, None, None, :] * eye_h * eye_c).reshape(b, rows, kw)
    slope_r = jnp.broadcast_to(slopes.reshape(1, A_HEADS, 1), (2, A_HEADS, ts)).reshape(rows)
    qpos_r = jnp.broadcast_to((n_past + jnp.arange(ts, dtype=F32)).reshape(1, 1, ts), (2, A_HEADS, ts)).reshape(rows)
    info = jnp.pad(jnp.stack([slope_r, qpos_r], axis=1), ((0, 0), (0, LANES - 2)))
    const2 = lambda i, c, pt: (0, 0)
    per_b = lambda r, w: pl.BlockSpec((None, r, w), lambda i, c, pt: (i, 0, 0))
    vec = pl.BlockSpec((1, HEAD_DIM), const2)
    in_specs = ([per_b(rows, kw), pl.BlockSpec((rows, LANES), const2), per_b(NEW_PAD, kw), per_b(NEW_PAD, vw),
                 vec, vec, vec, vec, pl.BlockSpec((1, A_V), const2)]
                + _page_specs(p, kw, page, layer, lambda c, j: c * p + j)
                + _page_specs(p, page * A_HEADS, A_V, layer, lambda c, j: c * p + j))
    out = pl.pallas_call(
        functools.partial(_diff_dec_kernel, p=p, page=page, n_past=n_past, n_new=ts, lam_init=lam_init),
        grid_spec=pltpu.PrefetchScalarGridSpec(
            num_scalar_prefetch=1, grid=(b, n_pages // p), in_specs=in_specs,
            out_specs=pl.BlockSpec((None, rows // 2, A_V), lambda i, c, pt: (i, 0, 0)),
            scratch_shapes=[pltpu.VMEM((rows, 1), F32), pltpu.VMEM((rows, 1), F32), pltpu.VMEM((rows, vw), F32)]),
        out_shape=jax.ShapeDtypeStruct((b, rows // 2, A_V), F32),
        compiler_params=_cparams(("arbitrary", "arbitrary")), name="diff_dec")(
            page_table, qbd, info, _pad_new(kn, b, ts), _pad_new(vn, b, ts),
            *[x.reshape(1, HEAD_DIM) for x in lams], subg.reshape(1, A_V),
            *([cache_k] * p), *([cache_v] * p))
    return out.reshape(b, A_HEADS, ts, A_V).transpose(0, 2, 1, 3).reshape(b * ts, A_HEADS * A_V)


def _fox_dec_kernel(pt_ref, qbd_ref, pn_ref, pnt_ref, kn_ref, vn_ref, *rest, p, page, n_new):
    k_refs, v_refs, lf_refs = rest[:p], rest[p:2 * p], rest[2 * p:3 * p]
    o_ref, m_sc, l_sc, acc_sc, run_sc = rest[3 * p:]
    c = pl.program_id(1)
    qb = (qbd_ref[...] * SCALE).astype(BF16)
    rows = qb.shape[0]
    pn_col = pn_ref[:, 0:1]

    @pl.when(c == 0)
    def _():
        m_sc[...] = jnp.full(m_sc.shape, NEG_INF, F32)
        l_sc[...] = jnp.zeros(l_sc.shape, F32)
        acc_sc[...] = jnp.zeros(acc_sc.shape, F32)
        run_sc[...] = jnp.zeros(run_sc.shape, F32)
        idx = _iota((1, NEW_PAD), 1)
        tok = _iota((rows, 1), 0) // C_HEADS
        bias = pn_col - jnp.concatenate([pnt_ref[...]] * n_new, axis=0)
        s = _dot_nt(qb, kn_ref[...].astype(BF16)) + bias
        s = jnp.where((idx <= tok) & (idx < n_new), s, NEG_INF)
        _online_step(s, vn_ref[...].astype(BF16), m_sc, l_sc, acc_sc)

    later = (_iota((page, page), 0) > _iota((page, page), 1)).astype(BF16)
    run = run_sc[...]
    sufs = [None] * p
    for j in reversed(range(p)):
        lf = lf_refs[j][...]
        sufs[j] = _dot_exact_rhs(lf, later) + run
        run = run + jnp.sum(lf, axis=-1, keepdims=True)
    run_sc[...] = run
    suf = jnp.concatenate(sufs, axis=1)
    bias = jnp.concatenate([suf] * n_new, axis=0) + pn_col
    _flash_pages(qb, k_refs, v_refs, lambda s: s + bias, m_sc, l_sc, acc_sc)

    @pl.when(c == pl.num_programs(1) - 1)
    def _():
        full = acc_sc[...] / l_sc[...]
        rowh = _iota((rows, 1), 0) % C_HEADS
        o2 = jnp.zeros((rows, LANES), F32)
        for hp in range(C_HEADS // 2):
            o2 = o2 + jnp.where(rowh // 2 == hp, full[:, hp * LANES:(hp + 1) * LANES], 0.0)
        o_ref[...] = jnp.where(rowh % 2 == 0, o2[:, 0:HEAD_DIM], o2[:, HEAD_DIM:LANES])


def _fox_dec(q, kn, vn, cum_new, cache_k, cache_v, cache_lft, layer, page_table, pages_per_step):
    b, n_pages = page_table.shape
    ts = q.shape[0] // b
    w, page = cache_k.shape[2], cache_k.shape[3]
    p = pages_per_step
    nsteps = n_pages // p
    rows = ts * C_HEADS
    eye = jnp.eye(C_HEADS, dtype=F32).reshape(1, 1, C_HEADS, C_HEADS, 1)
    qbd = (q.reshape(b, ts, C_HEADS, 1, HEAD_DIM) * eye).reshape(b, rows, w)
    pn = jnp.pad(cum_new.reshape(b, rows, 1), ((0, 0), (0, 0), (0, LANES - 1)))
    pnt = jnp.pad(cum_new.transpose(0, 2, 1), ((0, 0), (0, 0), (0, NEW_PAD - ts)))
    per_b = lambda r, ww: pl.BlockSpec((None, r, ww), lambda i, c, pt: (i, 0, 0))
    rev = lambda c, j: (nsteps - 1 - c) * p + j
    in_specs = ([per_b(rows, w), per_b(rows, LANES), per_b(C_HEADS, NEW_PAD), per_b(NEW_PAD, w), per_b(NEW_PAD, w)]
                + _page_specs(p, w, page, layer, rev) + _page_specs(p, w, page, layer, rev)
                + _page_specs(p, C_HEADS, page, layer, rev))
    out = pl.pallas_call(
        functools.partial(_fox_dec_kernel, p=p, page=page, n_new=ts),
        grid_spec=pltpu.PrefetchScalarGridSpec(
            num_scalar_prefetch=1, grid=(b, nsteps), in_specs=in_specs,
            out_specs=pl.BlockSpec((None, rows, HEAD_DIM), lambda i, c, pt: (i, 0, 0)),
            scratch_shapes=[pltpu.VMEM((rows, 1), F32), pltpu.VMEM((rows, 1), F32), pltpu.VMEM((rows, w), F32),
                            pltpu.VMEM((C_HEADS, 1), F32)]),
        out_shape=jax.ShapeDtypeStruct((b, rows, HEAD_DIM), F32),
        compiler_params=_cparams(("arbitrary", "arbitrary")), name="fox_dec")(
            page_table, qbd, pn, pnt, _pad_new(kn, b, ts), _pad_new(vn, b, ts),
            *([cache_k] * p), *([cache_v] * p), *([cache_lft] * p))
    return out.reshape(b * ts, C_HEADS * HEAD_DIM)


def _nsa_dec_kernel(pt_ref, qbd_ref, info_ref, kc_ref, vc_ref, skn_ref, svn_ref, wkn_ref, wvn_ref, wk_ref, wv_ref,
                    *rest, p, page, n_past, n_new):
    k_refs, v_refs = rest[:p], rest[p:2 * p]
    o_ref, m_sc, l_sc, acc_sc, ocmp_sc, pick_sc = rest[2 * p:]
    c = pl.program_id(1)
    q32 = qbd_ref[...] * SCALE
    qb = q32.astype(BF16)
    rows = q32.shape[0]
    gt = rows // B_REP
    slope = info_ref[:, 0:1]
    qpos = info_ref[:, 1:2]
    in_g0 = (_iota((rows, 1), 0) % gt) < n_new
    pick_cols = lambda full: jnp.where(in_g0, full[:, 0:HEAD_DIM], full[:, HEAD_DIM:2 * HEAD_DIM])
    n_blocks = n_past // SEL_BLOCK

    @pl.when(c == 0)
    def _():
        m_sc[...] = jnp.full(m_sc.shape, NEG_INF, F32)
        l_sc[...] = jnp.zeros(l_sc.shape, F32)
        acc_sc[...] = jnp.zeros(acc_sc.shape, F32)
        n_cmp_rows = kc_ref.shape[0]
        n_cmp = n_cmp_rows - (CMP_BLOCK // CMP_STRIDE - 1)
        cidx = _iota((1, n_cmp_rows), 1)
        dc = qpos - (cidx * CMP_STRIDE + (CMP_BLOCK - 1)).astype(F32)
        sc = _dot_nt_precise(q32, kc_ref[...]) - slope * dc
        pc = _masked_softmax_rows(sc, (dc >= 0) & (cidx < n_cmp))
        ocmp_sc[...] = pick_cols(_dot(pc.astype(BF16), vc_ref[...].astype(BF16)))
        psum = pc[0:gt]
        for r in range(1, B_REP):
            psum = psum + pc[r * gt:(r + 1) * gt]
        imp = _dot_exact_rhs(psum, _overlap(n_cmp_rows, n_blocks))
        blk = _iota((gt, n_blocks), 1)
        qpos1 = qpos[0:gt].astype(jnp.int32)
        forced = (blk == 0) | (blk == qpos1 // SEL_BLOCK)
        score = jnp.where(forced, 1e30, jnp.where(blk * SEL_BLOCK <= qpos1, imp, -1.0))
        pick_sc[...] = _top_blocks(score, min(SEL_TOP, n_blocks + 1) - 1, 1)

    keys = p * page
    kpos = c * keys + _iota((1, keys), 1)
    expand = (_iota((n_blocks, keys), 0) == (c * keys + _iota((n_blocks, keys), 1)) // SEL_BLOCK).astype(BF16)
    sel1 = _dot(pick_sc[...].astype(BF16), expand)
    sel = jnp.concatenate([sel1] * B_REP, axis=0)
    ds = qpos - kpos.astype(F32)
    _flash_pages(qb, k_refs, v_refs, lambda s: jnp.where((sel > 0.5) & (ds >= 0), s - slope * ds, NEG_INF),
                 m_sc, l_sc, acc_sc)

    @pl.when(c == pl.num_programs(1) - 1)
    def _():
        idx = _iota((1, NEW_PAD), 1)
        dn = qpos - (n_past + idx).astype(F32)
        new_ok = (dn >= 0) & (idx < n_new)
        s = jnp.where(new_ok, _dot_nt(qb, skn_ref[...].astype(BF16)) - slope * dn, NEG_INF)
        _online_step(s, svn_ref[...].astype(BF16), m_sc, l_sc, acc_sc)
        o_sel = pick_cols(acc_sc[...] / jnp.maximum(l_sc[...], 1e-30))
        keep = wk_ref.shape[1]
        wpos = n_past - keep + _iota((1, keep), 1)
        dw = qpos - wpos.astype(F32)
        ok1 = (dw >= 0) & (dw < WINDOW) & (wpos >= 0)
        ok2 = new_ok & (dn < WINDOW)
        s1 = jnp.where(ok1, _dot(qb, wk_ref[...].astype(BF16)) - slope * dw, NEG_INF)
        s2 = jnp.where(ok2, _dot_nt(qb, wkn_ref[...].astype(BF16)) - slope * dn, NEG_INF)
        mx = jnp.maximum(jnp.max(s1, axis=-1, keepdims=True), jnp.max(s2, axis=-1, keepdims=True))
        e1 = jnp.where(ok1, jnp.exp(s1 - mx), 0.0)
        e2 = jnp.where(ok2, jnp.exp(s2 - mx), 0.0)
        den = jnp.maximum(jnp.sum(e1, axis=-1, keepdims=True) + jnp.sum(e2, axis=-1, keepdims=True), 1e-30)
        o_win = pick_cols((_dot_nt(e1.astype(BF16), wv_ref[...].astype(BF16))
                           + _dot(e2.astype(BF16), wvn_ref[...].astype(BF16))) / den)
        gates = jax.nn.sigmoid(info_ref[:, 2:5])
        o_ref[...] = gates[:, 0:1] * ocmp_sc[...] + gates[:, 1:2] * o_sel + gates[:, 2:3] * o_win


def _nsa_dec(q, gate, kc, vc, skn, svn, wkn, wvn, win_k, win_v, cache_sk, cache_sv, layer, page_table, slopes,
             pages_per_step):
    b, n_pages = page_table.shape
    ts = q.shape[0] // b
    w, page = cache_sk.shape[2], cache_sk.shape[3]
    n_past = n_pages * page
    keep = win_k.shape[3]
    p = pages_per_step
    rows = B_HEADS * ts
    q5 = q.reshape(b, ts, B_KV, B_REP, HEAD_DIM).transpose(0, 3, 2, 1, 4)
    eye = jnp.eye(B_KV, dtype=F32).reshape(1, 1, B_KV, 1, B_KV, 1)
    qbd = (q5[:, :, :, :, None, :] * eye).reshape(b, rows, w)
    slope_r = jnp.broadcast_to(slopes.reshape(B_KV, B_REP).T.reshape(B_REP, B_KV, 1), (B_REP, B_KV, ts)).reshape(rows)
    qpos_r = jnp.broadcast_to((n_past + jnp.arange(ts, dtype=F32)).reshape(1, 1, ts), (B_REP, B_KV, ts)).reshape(rows)
    glog = gate[:, :3 * B_HEADS].reshape(b, ts, B_KV, B_REP, 3).transpose(0, 3, 2, 1, 4).reshape(b, rows, 3)
    info = jnp.concatenate([jnp.broadcast_to(jnp.stack([slope_r, qpos_r], axis=1)[None], (b, rows, 2)), glog], axis=2)
    info = jnp.pad(info, ((0, 0), (0, 0), (0, LANES - 5)))
    per_b = lambda r, ww: pl.BlockSpec((None, r, ww), lambda i, c, pt: (i, 0, 0))
    win = pl.BlockSpec((None, None, w, keep), lambda i, c, pt: (layer, i, 0, 0))
    ncr = kc.shape[1]
    in_specs = ([per_b(rows, w), per_b(rows, LANES), per_b(ncr, w), per_b(ncr, w)] + [per_b(NEW_PAD, w)] * 4
                + [win, win]
                + _page_specs(p, w, page, layer, lambda c, j: c * p + j)
                + _page_specs(p, w, page, layer, lambda c, j: c * p + j))
    out = pl.pallas_call(
        functools.partial(_nsa_dec_kernel, p=p, page=page, n_past=n_past, n_new=ts),
        grid_spec=pltpu.PrefetchScalarGridSpec(
            num_scalar_prefetch=1, grid=(b, n_pages // p), in_specs=in_specs,
            out_specs=pl.BlockSpec((None, rows, HEAD_DIM), lambda i, c, pt: (i, 0, 0)),
            scratch_shapes=[pltpu.VMEM((rows, 1), F32), pltpu.VMEM((rows, 1), F32), pltpu.VMEM((rows, w), F32),
                            pltpu.VMEM((rows, HEAD_DIM), F32), pltpu.VMEM((rows // B_REP, n_past // SEL_BLOCK), F32)]),
        out_shape=jax.ShapeDtypeStruct((b, rows, HEAD_DIM), F32),
        compiler_params=_cparams(("arbitrary", "arbitrary")), name="nsa_dec")(
            page_table, qbd, info, kc, vc, _pad_new(skn, b, ts), _pad_new(svn, b, ts), _pad_new(wkn, b, ts),
            _pad_new(wvn, b, ts), win_k, win_v, *([cache_sk] * p), *([cache_sv] * p))
    return out.reshape(b, B_REP, B_KV, ts, HEAD_DIM).transpose(0, 3, 2, 1, 4).reshape(b * ts, B_HEADS * HEAD_DIM)


def _alibi_slopes(n):
    return jnp.exp2(-8.0 * jnp.arange(1, n + 1, dtype=F32) / n)


def _pages_per_step(n_pages, want):
    p = min(want, n_pages)
    while n_pages % p:
        p -= 1
    return p


def kernel(x_prompt, x_sample, cache_diff_k, cache_diff_v, cache_nsa_cmp_k, cache_nsa_cmp_v, cache_nsa_sel_k, cache_nsa_sel_v, state_nsa_win_k, state_nsa_win_v, cache_fox_k, cache_fox_v, cache_fox_logf, page_table, norm_ffn1_g, ffn1_wg, ffn1_wu, ffn1_wd, norm_mix_g, norm_ffn2_g, ffn2_wg, ffn2_wu, ffn2_wd, even_w_in, even_w_out, diff_lambda_q1, diff_lambda_k1, diff_lambda_q2, diff_lambda_k2, diff_subln_g, cmp_pos_k, cmp_w1_k, cmp_w2_k, cmp_pos_v, cmp_w1_v, cmp_w2_v, odd_w_in, odd_b_f, odd_w_out, final_norm_g):
    b, t, d = x_prompt.shape
    bs, ts, _ = x_sample.shape
    depth = norm_ffn1_g.shape[0]
    n_pages = page_table.shape[1]
    n_pool, page = cache_diff_k.shape[1], cache_diff_k.shape[2]
    keep = state_nsa_win_k.shape[2]
    n_past = n_pages * page
    assert n_past % SEL_BLOCK == 0 and page % SEL_BLOCK == 0 and page % CMP_STRIDE == 0
    assert 0 < ts <= min(SEL_BLOCK, CMP_STRIDE - 1, NEW_PAD) and (A_HEADS * ts) % 8 == 0
    assert t % LANES == 0 and n_past >= WINDOW
    kvw = B_KV * HEAD_DIM

    slopes_a = _alibi_slopes(A_HEADS)
    slopes_b = _alibi_slopes(B_HEADS)
    keys_last = lambda c: jnp.moveaxis(c, 2, -1).reshape(c.shape[0], c.shape[1], -1, c.shape[2])
    c_diff_k = keys_last(cache_diff_k)
    c_diff_v = cache_diff_v.reshape(cache_diff_v.shape[0], n_pool, page * A_HEADS, A_V)
    c_cmp_k, c_cmp_v = keys_last(cache_nsa_cmp_k), keys_last(cache_nsa_cmp_v)
    c_sel_k, c_sel_v = keys_last(cache_nsa_sel_k), keys_last(cache_nsa_sel_v)
    c_fox_k, c_fox_v = keys_last(cache_fox_k), keys_last(cache_fox_v)
    c_fox_lft = keys_last(cache_fox_logf)
    win_k, win_v = keys_last(state_nsa_win_k), keys_last(state_nsa_win_v)

    a_w = A_HEADS * 2 * HEAD_DIM
    even_cuts = [a_w, a_w, A_HEADS * A_V, B_HEADS * HEAD_DIM] + [kvw] * 6 + [3 * B_HEADS]
    odd_cuts = [C_HEADS * HEAD_DIM] * 3 + [C_HEADS]

    def split_w(w, cuts):
        out, s = [], 0
        for c in cuts:
            out.append(_pad_cols(w[:, s:s + c], -(-c // LANES) * LANES).astype(BF16))
            s += c
        return out

    xp = x_prompt.reshape(b * t, d)
    xs = x_sample.reshape(bs * ts, d)
    even_p = [[] for _ in range(8)]
    even_s = [[] for _ in range(8)]
    odd_p = [[] for _ in range(3)]
    odd_s = [[] for _ in range(3)]

    for li in range(depth):
        w1 = [ffn1_wg[li].astype(BF16), ffn1_wu[li].astype(BF16), ffn1_wd[li].astype(BF16)]
        xp = _ffn(xp, norm_ffn1_g[li], *w1)
        xs = _ffn(xs, norm_ffn1_g[li], *w1)
        if li % 2 == 0:
            e = li // 2
            lam_init = 0.8 - 0.6 * math.exp(-0.3 * li)
            lams = (diff_lambda_q1[e], diff_lambda_k1[e], diff_lambda_q2[e], diff_lambda_k2[e])
            ws = split_w(even_w_in[e], even_cuts)
            wo = even_w_out[e].astype(BF16)
            wo_a, wo_b = wo[:A_HEADS * A_V], wo[A_HEADS * A_V:]
            aq, ak, av, bq, ck, cv, sk, sv, wk, wv, bg = _norm_proj(xp, norm_mix_g[li], ws)
            r3 = lambda a: a.reshape(b, t, a.shape[-1])
            oa = _diff_prompt(r3(aq), r3(ak), r3(av), slopes_a, lams, diff_subln_g[e], lam_init)
            kc = _compress_prompt(r3(ck), cmp_pos_k[e], cmp_w1_k[e], cmp_w2_k[e])
            vc = _compress_prompt(r3(cv), cmp_pos_v[e], cmp_w1_v[e], cmp_w2_v[e])
            ob = _nsa_prompt(r3(bq), r3(bg), kc, vc, r3(sk), r3(sv), r3(wk), r3(wv), slopes_b)
            xp = _out_proj(xp, [oa.reshape(b * t, -1), ob.reshape(b * t, -1)], [wo_a, wo_b])

            def last_rows(a):
                a = jnp.pad(r3(a), ((0, 0), (max(0, keep - t), 0), (0, 0)))
                return a[:, a.shape[1] - keep:].reshape(b, keep, B_KV, HEAD_DIM)

            rows_p = (ak.reshape(b, t, A_HEADS, 2, HEAD_DIM), av.reshape(b, t, A_HEADS, A_V),
                      ck.reshape(b, t, B_KV, HEAD_DIM), cv.reshape(b, t, B_KV, HEAD_DIM),
                      sk.reshape(b, t, B_KV, HEAD_DIM), sv.reshape(b, t, B_KV, HEAD_DIM), last_rows(wk), last_rows(wv))
            for lst, r in zip(even_p, rows_p):
                lst.append(r)
            aq, ak, av, bq, ck, cv, sk, sv, wk, wv, bg = _norm_proj(xs, norm_mix_g[li], ws)
            oa = _diff_dec(aq, ak, av, c_diff_k, c_diff_v, e, page_table, slopes_a, lams, diff_subln_g[e], lam_init,
                           _pages_per_step(n_pages, 16))
            pps = _pages_per_step(n_pages, 16)
            kc = _compress_paged(c_cmp_k, e, page_table, cmp_pos_k[e], cmp_w1_k[e], cmp_w2_k[e], pps)
            vc = _compress_paged(c_cmp_v, e, page_table, cmp_pos_v[e], cmp_w1_v[e], cmp_w2_v[e], pps)
            ob = _nsa_dec(bq, bg, kc, vc, sk, sv, wk, wv, win_k, win_v, c_sel_k, c_sel_v, e, page_table, slopes_b, pps)
            xs = _out_proj(xs, [oa, ob], [wo_a, wo_b])
            s3 = lambda a: a.reshape(bs, ts, a.shape[-1])
            new_win = lambda buf, a: jnp.concatenate(
                [buf[e], s3(a).reshape(bs, ts, B_KV, HEAD_DIM)], axis=1)[:, ts:]
            rows_s = (ak.reshape(bs, ts, A_HEADS, 2, HEAD_DIM), av.reshape(bs, ts, A_HEADS, A_V),
                      ck.reshape(bs, ts, B_KV, HEAD_DIM), cv.reshape(bs, ts, B_KV, HEAD_DIM),
                      sk.reshape(bs, ts, B_KV, HEAD_DIM), sv.reshape(bs, ts, B_KV, HEAD_DIM),
                      new_win(state_nsa_win_k, wk), new_win(state_nsa_win_v, wv))
            for lst, r in zip(even_s, rows_s):
                lst.append(r)
        else:
            o = li // 2
            ws = split_w(odd_w_in[o], odd_cuts)
            wo = odd_w_out[o].astype(BF16)
            q, k, v, f = _norm_proj(xp, norm_mix_g[li], ws)
            r3 = lambda a: a.reshape(b, t, a.shape[-1])
            logf, cum = _logf_cum(r3(f), odd_b_f[o])
            om = _fox_prompt(r3(q), r3(k), r3(v), cum)
            xp = _out_proj(xp, [om.reshape(b * t, -1)], [wo])
            for lst, r in zip(odd_p, (k.reshape(b, t, C_HEADS, HEAD_DIM), v.reshape(b, t, C_HEADS, HEAD_DIM), logf)):
                lst.append(r)
            q, k, v, f = _norm_proj(xs, norm_mix_g[li], ws)
            logf, cum = _logf_cum(f.reshape(bs, ts, -1), odd_b_f[o])
            om = _fox_dec(q, k, v, cum, c_fox_k, c_fox_v, c_fox_lft, o, page_table, _pages_per_step(n_pages, 8))
            xs = _out_proj(xs, [om], [wo])
            for lst, r in zip(odd_s, (k.reshape(bs, ts, C_HEADS, HEAD_DIM), v.reshape(bs, ts, C_HEADS, HEAD_DIM), logf)):
                lst.append(r)
        w2 = [ffn2_wg[li].astype(BF16), ffn2_wu[li].astype(BF16), ffn2_wd[li].astype(BF16)]
        fin = final_norm_g if li == depth - 1 else None
        xp = _ffn(xp, norm_ffn2_g[li], *w2, final_g=fin)
        xs = _ffn(xs, norm_ffn2_g[li], *w2, final_g=fin)

    stk = lambda lst: jnp.stack(lst, 0)
    ep = [stk(l) for l in even_p]
    es = [stk(l) for l in even_s]
    op = [stk(l) for l in odd_p]
    os_ = [stk(l) for l in odd_s]
    out = [xp.reshape(b, t, d), xs.reshape(bs, ts, d)]
    for p_, s_ in zip(ep, es):
        out += [p_, s_]
    for p_, s_ in zip(op, os_):
        out += [p_, s_]
    return tuple(out)
```

```python
import functools
import math

import jax
import jax.numpy as jnp
from jax import lax
from jax.experimental import pallas as pl
from jax.experimental.pallas import tpu as pltpu

HEAD_DIM = 64
A_HEADS = 4
A_V = 2 * HEAD_DIM
B_HEADS = 8
B_KV = 2
B_REP = B_HEADS // B_KV
CMP_BLOCK = 32
CMP_STRIDE = 16
CMP_HIDDEN = 128
SEL_BLOCK = 64
SEL_TOP = 16
WINDOW = 512
C_HEADS = 16
RMS_EPS = 1e-6
NEG_INF = -1e30
SCALE = HEAD_DIM ** -0.5

LANES = 128
V7X_VMEM_LIMIT_BYTES = 56 * 1024 * 1024

F32 = jnp.float32
BF16 = jnp.bfloat16


def _cparams(sem):
    return pltpu.CompilerParams(dimension_semantics=sem, vmem_limit_bytes=V7X_VMEM_LIMIT_BYTES)


def _dot(a, b):
    return jnp.dot(a, b, preferred_element_type=F32)


def _dot_nt(a, b):
    return lax.dot_general(a, b, (((1,), (1,)), ((), ())), preferred_element_type=F32)


def _split3(x):
    hi = x.astype(BF16)
    r1 = x - hi.astype(F32)
    mid = r1.astype(BF16)
    lo = (r1 - mid.astype(F32)).astype(BF16)
    return hi, mid, lo


def _dot_exact_rhs(x, m_bf16):
    hi, mid, lo = _split3(x)
    return _dot(hi, m_bf16) + _dot(mid, m_bf16) + _dot(lo, m_bf16)


def _dot_nt_precise(a, b):
    ah = a.astype(BF16)
    al = (a - ah.astype(F32)).astype(BF16)
    bh = b.astype(BF16)
    bl = (b - bh.astype(F32)).astype(BF16)
    return _dot_nt(ah, bh) + _dot_nt(ah, bl) + _dot_nt(al, bh)


def _rms(x, g):
    ms = jnp.mean(x * x, axis=-1, keepdims=True)
    return x * lax.rsqrt(ms + RMS_EPS) * g


def _iota(shape, dim):
    return lax.broadcasted_iota(jnp.int32, shape, dim)


def _masked_softmax_rows(s, mask):
    s = jnp.where(mask, s, NEG_INF)
    e = jnp.where(mask, jnp.exp(s - jnp.max(s, axis=-1, keepdims=True)), 0.0)
    return e / jnp.maximum(jnp.sum(e, axis=-1, keepdims=True), 1e-30)


def _online_step(s, v_bf16, m_ref, l_ref, acc_ref):
    m_old = m_ref[...]
    m_new = jnp.maximum(m_old, jnp.max(s, axis=-1, keepdims=True))
    alpha = jnp.exp(m_old - m_new)
    p = jnp.exp(s - m_new)
    l_ref[...] = alpha * l_ref[...] + jnp.sum(p, axis=-1, keepdims=True)
    acc_ref[...] = alpha * acc_ref[...] + _dot(p.astype(BF16), v_bf16)
    m_ref[...] = m_new


def _row_tile(n, pref):
    for t in pref:
        if n % t == 0:
            return t
    return n


def _ffn_kernel(x_ref, g_ref, wg_ref, wu_ref, wd_ref, *rest, fc, nfc, has_final):
    if has_final:
        gf_ref, o_ref = rest
    else:
        (o_ref,) = rest
    x = x_ref[...]
    h = _rms(x, g_ref[...]).astype(BF16)
    acc = jnp.zeros(x.shape, F32)
    for c in range(nfc):
        sl = slice(c * fc, (c + 1) * fc)
        a = _dot(h, wg_ref[:, sl])
        u = _dot(h, wu_ref[:, sl])
        act = (a * jax.nn.sigmoid(a) * u).astype(BF16)
        acc = acc + _dot(act, wd_ref[sl, :])
    y = x + 0.5 * acc
    if has_final:
        y = _rms(y, gf_ref[...])
    o_ref[...] = y


def _ffn(x, g, wg, wu, wd, final_g=None):
    n, d = x.shape
    f = wg.shape[1]
    tm = _row_tile(n, (512, 256, 128))
    fc = _row_tile(f, (256, 128))
    const = lambda i: (0, 0)
    in_specs = [pl.BlockSpec((tm, d), lambda i: (i, 0)), pl.BlockSpec((1, d), const),
                pl.BlockSpec((d, f), const), pl.BlockSpec((d, f), const), pl.BlockSpec((f, d), const)]
    args = [x, g.reshape(1, d), wg, wu, wd]
    if final_g is not None:
        in_specs.append(pl.BlockSpec((1, d), const))
        args.append(final_g.reshape(1, d))
    return pl.pallas_call(
        functools.partial(_ffn_kernel, fc=fc, nfc=f // fc, has_final=final_g is not None),
        grid=(n // tm,), in_specs=in_specs, out_specs=pl.BlockSpec((tm, d), lambda i: (i, 0)),
        out_shape=jax.ShapeDtypeStruct((n, d), F32), compiler_params=_cparams(("arbitrary",)),
        name="ffn")(*args)


def _norm_proj_kernel(x_ref, g_ref, *refs, n_out):
    h = _rms(x_ref[...], g_ref[...]).astype(BF16)
    for w_ref, o_ref in zip(refs[:n_out], refs[n_out:]):
        o_ref[...] = _dot(h, w_ref[...])


def _norm_proj(x, g, ws):
    n, d = x.shape
    tm = _row_tile(n, (512, 256, 128))
    const = lambda i: (0, 0)
    in_specs = [pl.BlockSpec((tm, d), lambda i: (i, 0)), pl.BlockSpec((1, d), const)]
    in_specs += [pl.BlockSpec(w.shape, const) for w in ws]
    return pl.pallas_call(
        functools.partial(_norm_proj_kernel, n_out=len(ws)),
        grid=(n // tm,), in_specs=in_specs,
        out_specs=[pl.BlockSpec((tm, w.shape[1]), lambda i: (i, 0)) for w in ws],
        out_shape=[jax.ShapeDtypeStruct((n, w.shape[1]), F32) for w in ws],
        compiler_params=_cparams(("arbitrary",)), name="norm_proj")(x, g.reshape(1, d), *ws)


def _out_proj_kernel(x_ref, *refs, n_in):
    y = x_ref[...]
    for o_ref, w_ref in zip(refs[:n_in], refs[n_in:2 * n_in]):
        y = y + _dot(o_ref[...].astype(BF16), w_ref[...])
    refs[2 * n_in][...] = y


def _out_proj(x, os_, ws):
    n, d = x.shape
    tm = _row_tile(n, (512, 256, 128))
    const = lambda i: (0, 0)
    in_specs = [pl.BlockSpec((tm, d), lambda i: (i, 0))]
    in_specs += [pl.BlockSpec((tm, o.shape[1]), lambda i: (i, 0)) for o in os_]
    in_specs += [pl.BlockSpec(w.shape, const) for w in ws]
    return pl.pallas_call(
        functools.partial(_out_proj_kernel, n_in=len(os_)),
        grid=(n // tm,), in_specs=in_specs, out_specs=pl.BlockSpec((tm, d), lambda i: (i, 0)),
        out_shape=jax.ShapeDtypeStruct((n, d), F32), compiler_params=_cparams(("arbitrary",)),
        name="out_proj")(x, *os_, *ws)


def _pad_cols(w, width):
    return jnp.pad(w, ((0, 0), (0, width - w.shape[1])))


def _logf_cum_kernel(f_ref, b_ref, lf_ref, cum_ref, carry_ref, *, tc):
    @pl.when(pl.program_id(1) == 0)
    def _():
        carry_ref[...] = jnp.zeros(carry_ref.shape, F32)

    z = f_ref[...] + b_ref[...]
    lf = jnp.minimum(z, 0.0) - jnp.log1p(jnp.exp(-jnp.abs(z)))
    tri = (_iota((tc, tc), 0) >= _iota((tc, tc), 1)).astype(BF16)
    cum = _dot_exact_rhs_left(tri, lf) + carry_ref[...]
    lf_ref[...] = lf[:, :C_HEADS]
    cum_ref[...] = cum[:, :C_HEADS]
    carry_ref[...] = cum[tc - 1:tc, :]


def _dot_exact_rhs_left(m_bf16, x):
    hi, mid, lo = _split3(x)
    return _dot(m_bf16, hi) + _dot(m_bf16, mid) + _dot(m_bf16, lo)


def _logf_cum(f, b_f):
    b, t, _ = f.shape
    tc = _row_tile(t, (256, 128))
    bias = jnp.pad(b_f, (0, LANES - C_HEADS)).reshape(1, LANES)
    out = jax.ShapeDtypeStruct((b, t, C_HEADS), F32)
    return pl.pallas_call(
        functools.partial(_logf_cum_kernel, tc=tc),
        grid=(b, t // tc),
        in_specs=[pl.BlockSpec((None, tc, LANES), lambda i, j: (i, j, 0)), pl.BlockSpec((1, LANES), lambda i, j: (0, 0))],
        out_specs=[pl.BlockSpec((None, tc, C_HEADS), lambda i, j: (i, j, 0))] * 2,
        out_shape=[out, out], scratch_shapes=[pltpu.VMEM((1, LANES), F32)],
        compiler_params=_cparams(("arbitrary", "arbitrary")), name="logf_cum")(f, bias)


LOG2E = 1.4426950408889634


def _lanes(x, n):
    if n % LANES == 0:
        return x if n == LANES else jnp.tile(x, (1, n // LANES))
    return x[:, :n]


def _flash_update(s, v_aug, m_ref, acc_ref):
    m_prev = m_ref[...]
    m_next = jnp.maximum(m_prev, jnp.max(s, axis=-1, keepdims=True))
    p = jnp.exp2(s - _lanes(m_next, s.shape[1]))
    alpha = jnp.exp2(m_prev - m_next)
    acc_ref[...] = acc_ref[...] * _lanes(alpha, acc_ref.shape[1]) + _dot(p.astype(BF16), v_aug)
    m_ref[...] = m_next


def _causal_tiles(q0, tq, tk, tile_fn):
    n_full = q0 // tk
    n_k = (q0 + tq + tk - 1) // tk

    def run(masked):
        def body(kj, carry):
            tile_fn(kj, masked)
            return carry
        return body

    lax.fori_loop(0, n_full, run(False), 0)
    lax.fori_loop(n_full, n_k, run(True), 0)


def _stage_bf16(src_ref, dst_ref, tk):
    for j in range(src_ref.shape[0] // tk):
        rows = slice(j * tk, (j + 1) * tk)
        dst_ref[rows, :] = src_ref[rows, :].astype(BF16)


def _stage_values(src_ref, dst_ref, tk, cols, width):
    ones = jnp.ones((tk, width), BF16)
    for j in range(src_ref.shape[0] // tk):
        rows = slice(j * tk, (j + 1) * tk)
        dst_ref[rows, :] = jnp.concatenate([src_ref[rows, cols].astype(BF16), ones], axis=1)


def _fox_prompt_kernel(q_ref, k_ref, v_ref, fq_ref, fk_ref, o_ref, m_sc, acc_sc, kb_sc, va_sc, *, tq, tk, nsub):
    subs = [slice(c * HEAD_DIM, (c + 1) * HEAD_DIM) for c in range(nsub)]

    @pl.when(pl.program_id(2) == 0)
    def _():
        _stage_bf16(k_ref, kb_sc, tk)
        for c, cs in enumerate(subs):
            _stage_values(v_ref, va_sc.at[c], tk, cs, HEAD_DIM)

    q0 = pl.program_id(2) * tq
    qpos = q0 + _iota((tq, 1), 0)
    m_sc[...] = jnp.full(m_sc.shape, NEG_INF, F32)
    acc_sc[...] = jnp.zeros(acc_sc.shape, F32)
    qs = [(q_ref[:, cs] * (SCALE * LOG2E)).astype(BF16) for cs in subs]
    f0 = [fq_ref[0:1, c:c + 1] for c in range(nsub)]

    def tile(kj, masked):
        k0 = pl.multiple_of(kj * tk, tk)
        if masked:
            ok = qpos >= k0 + _iota((1, tk), 1)
        for c, cs in enumerate(subs):
            s = _dot_nt(qs[c], kb_sc[pl.ds(k0, tk), cs]) + (f0[c] - fk_ref[kj, c:c + 1, :]) * LOG2E
            if masked:
                s = jnp.where(ok, s, NEG_INF)
            _flash_update(s, va_sc[c, pl.ds(k0, tk), :], m_sc.at[c], acc_sc.at[c])

    _causal_tiles(q0, tq, tk, tile)
    for c, cs in enumerate(subs):
        o_ref[:, cs] = acc_sc[c][:, 0:HEAD_DIM] / acc_sc[c][:, HEAD_DIM:LANES]


def _fox_prompt(q, k, v, cum):
    b, t, w = q.shape
    tq = _row_tile(t, (512, 256, 128))
    tk = tq
    nsub = 4
    ngrp = C_HEADS // nsub
    wb = nsub * HEAD_DIM
    fq = cum.reshape(b, t, ngrp, nsub).transpose(0, 2, 1, 3)
    fk = cum.reshape(b, t // tk, tk, ngrp, nsub).transpose(0, 3, 1, 4, 2)
    return pl.pallas_call(
        functools.partial(_fox_prompt_kernel, tq=tq, tk=tk, nsub=nsub),
        grid=(b, ngrp, t // tq),
        in_specs=[pl.BlockSpec((None, tq, wb), lambda i, h, j: (i, j, h)),
                  pl.BlockSpec((None, t, wb), lambda i, h, j: (i, 0, h)),
                  pl.BlockSpec((None, t, wb), lambda i, h, j: (i, 0, h)),
                  pl.BlockSpec((None, None, tq, nsub), lambda i, h, j: (i, h, j, 0)),
                  pl.BlockSpec((None, None, t // tk, nsub, tk), lambda i, h, j: (i, h, 0, 0, 0))],
        out_specs=pl.BlockSpec((None, tq, wb), lambda i, h, j: (i, j, h)),
        out_shape=jax.ShapeDtypeStruct((b, t, w), F32),
        scratch_shapes=[pltpu.VMEM((nsub, tq, LANES), F32), pltpu.VMEM((nsub, tq, LANES), F32),
                        pltpu.VMEM((t, wb), BF16), pltpu.VMEM((nsub, t, LANES), BF16)],
        compiler_params=_cparams(("arbitrary",) * 3), name="fox_prompt")(q, k, v, fq, fk)


def _diff_lambda(lq1, lk1, lq2, lk2, lam_init):
    return (jnp.exp(jnp.sum(lq1 * lk1, axis=-1, keepdims=True))
            - jnp.exp(jnp.sum(lq2 * lk2, axis=-1, keepdims=True)) + lam_init)


def _diff_prompt_kernel(slopes_ref, q_ref, k_ref, v_ref, lq1, lk1, lq2, lk2, subg_ref, o_ref,
                        m_sc, acc_sc, kb_sc, va_sc, *, tq, tk, lam_init, nh):
    @pl.when(pl.program_id(2) == 0)
    def _():
        _stage_bf16(k_ref, kb_sc, tk)
        for h in range(nh):
            _stage_values(v_ref, va_sc.at[h], tk, slice(h * A_V, (h + 1) * A_V), A_V)

    q0 = pl.program_id(2) * tq
    qpos = q0 + _iota((tq, 1), 0)
    m_sc[...] = jnp.full(m_sc.shape, NEG_INF, F32)
    acc_sc[...] = jnp.zeros(acc_sc.shape, F32)
    subs = [slice(c * HEAD_DIM, (c + 1) * HEAD_DIM) for c in range(2 * nh)]
    qs = [(q_ref[:, cs] * (SCALE * LOG2E)).astype(BF16) for cs in subs]
    slope2 = [slopes_ref[pl.program_id(1) * nh + h] * LOG2E for h in range(nh)]

    def tile(kj, masked):
        k0 = pl.multiple_of(kj * tk, tk)
        kpos = k0 + _iota((1, tk), 1)
        krel = (kpos - q0).astype(F32)
        if masked:
            ok = qpos >= kpos
        for c, cs in enumerate(subs):
            s = _dot_nt(qs[c], kb_sc[pl.ds(k0, tk), cs]) + slope2[c // 2] * krel
            if masked:
                s = jnp.where(ok, s, NEG_INF)
            _flash_update(s, va_sc[c // 2, pl.ds(k0, tk), :], m_sc.at[c], acc_sc.at[c])

    _causal_tiles(q0, tq, tk, tile)
    lam = _diff_lambda(lq1[...], lk1[...], lq2[...], lk2[...], lam_init)
    for h in range(nh):
        a0, a1 = acc_sc[2 * h], acc_sc[2 * h + 1]
        o = a0[:, 0:A_V] / a0[:, A_V:2 * A_V] - lam * (a1[:, 0:A_V] / a1[:, A_V:2 * A_V])
        o_ref[:, h * A_V:(h + 1) * A_V] = _rms(o, subg_ref[...]) * (1.0 - lam_init)


def _diff_prompt(q, k, v, slopes, lams, subg, lam_init):
    b, t, w = q.shape
    tq = _row_tile(t, (512, 256, 128))
    tk = tq
    nh = 2
    wb = nh * A_V
    vec = pl.BlockSpec((1, HEAD_DIM), lambda i, h, j: (0, 0))
    return pl.pallas_call(
        functools.partial(_diff_prompt_kernel, tq=tq, tk=tk, lam_init=lam_init, nh=nh),
        grid=(b, A_HEADS // nh, t // tq),
        in_specs=[pl.BlockSpec(memory_space=pltpu.SMEM),
                  pl.BlockSpec((None, tq, wb), lambda i, h, j: (i, j, h)),
                  pl.BlockSpec((None, t, wb), lambda i, h, j: (i, 0, h)),
                  pl.BlockSpec((None, t, wb), lambda i, h, j: (i, 0, h)),
                  vec, vec, vec, vec, pl.BlockSpec((1, A_V), lambda i, h, j: (0, 0))],
        out_specs=pl.BlockSpec((None, tq, wb), lambda i, h, j: (i, j, h)),
        out_shape=jax.ShapeDtypeStruct((b, t, w), F32),
        scratch_shapes=[pltpu.VMEM((2 * nh, tq, LANES), F32), pltpu.VMEM((2 * nh, tq, 2 * A_V), F32),
                        pltpu.VMEM((t, wb), BF16), pltpu.VMEM((nh, t, 2 * A_V), BF16)],
        compiler_params=_cparams(("arbitrary",) * 3), name="diff_prompt")(
            slopes, q, k, v, *[x.reshape(1, HEAD_DIM) for x in lams], subg.reshape(1, A_V))


def _cmp_weights(pos, w1):
    ratio = CMP_BLOCK // CMP_STRIDE
    w1r = w1.reshape(ratio, CMP_STRIDE, 1, HEAD_DIM, 1, CMP_HIDDEN)
    eye = jnp.eye(B_KV, dtype=w1.dtype).reshape(1, 1, B_KV, 1, B_KV, 1)
    big = (w1r * eye).reshape(ratio, CMP_STRIDE * B_KV * HEAD_DIM, B_KV * CMP_HIDDEN)
    w1big = jnp.concatenate([big[r] for r in range(ratio)], axis=1)
    posr = jnp.broadcast_to(pos.reshape(ratio, CMP_STRIDE, 1, HEAD_DIM), (ratio, CMP_STRIDE, B_KV, HEAD_DIM))
    posrows = jnp.pad(posr.reshape(ratio, CMP_STRIDE * B_KV * HEAD_DIM), ((0, 8 - ratio), (0, 0)))
    return w1big.astype(BF16), posrows


def _cmp_a_kernel(x_ref, w_ref, o_ref):
    o_ref[...] = _dot(x_ref[...].astype(BF16), w_ref[...])


def _cmp_b_kernel(a_ref, posrows_ref, w1_ref, w2_ref, o_ref):
    n = a_ref.shape[0]
    hw = B_KV * CMP_HIDDEN
    pc = _dot(posrows_ref[...].astype(BF16), w1_ref[...])
    posc = pc[0:1, 0:hw] + pc[1:2, hw:2 * hw]
    a = a_ref[...]
    nxt = pltpu.roll(a[:, hw:2 * hw], n - 1, 0)
    pre = a[:, 0:hw] + nxt + posc
    act = (pre * jax.nn.sigmoid(pre)).astype(BF16)
    w2 = w2_ref[...]
    for g in range(B_KV):
        o_ref[:, g * HEAD_DIM:(g + 1) * HEAD_DIM] = _dot(act[:, g * CMP_HIDDEN:(g + 1) * CMP_HIDDEN], w2)


def _cmp_b(a, posrows, w1big, w2):
    b, n, wa = a.shape
    const = lambda i: (0, 0)
    return pl.pallas_call(
        _cmp_b_kernel, grid=(b,),
        in_specs=[pl.BlockSpec((None, n, wa), lambda i: (i, 0, 0)), pl.BlockSpec(posrows.shape, const),
                  pl.BlockSpec(w1big.shape, const), pl.BlockSpec(w2.shape, const)],
        out_specs=pl.BlockSpec((None, n, B_KV * HEAD_DIM), lambda i: (i, 0, 0)),
        out_shape=jax.ShapeDtypeStruct((b, n, B_KV * HEAD_DIM), F32),
        compiler_params=_cparams(("arbitrary",)), name="cmp_b")(a, posrows, w1big, w2)


def _compress_prompt(rows, pos, w1, w2):
    b, t, w = rows.shape
    n = t // CMP_STRIDE
    w1big, posrows = _cmp_weights(pos, w1)
    view = rows.reshape(b * n, CMP_STRIDE * w)
    tm = _row_tile(b * n, (256, 128))
    a = pl.pallas_call(
        _cmp_a_kernel, grid=(b * n // tm,),
        in_specs=[pl.BlockSpec((tm, CMP_STRIDE * w), lambda i: (i, 0)), pl.BlockSpec(w1big.shape, lambda i: (0, 0))],
        out_specs=pl.BlockSpec((tm, w1big.shape[1]), lambda i: (i, 0)),
        out_shape=jax.ShapeDtypeStruct((b * n, w1big.shape[1]), F32),
        compiler_params=_cparams(("arbitrary",)), name="cmp_a")(view, w1big)
    return _cmp_b(a.reshape(b, n, -1), posrows, w1big, w2.astype(BF16))


def _cmp_a_paged_kernel(pt_ref, *refs, n_pages):
    page_refs = refs[:n_pages]
    w_ref, o_ref, x_sc = refs[n_pages:]
    lanes, page = page_refs[0].shape
    for i in range(n_pages):
        x_sc[i * page:(i + 1) * page, :] = page_refs[i][...].T
    n_chunks = n_pages * page // CMP_STRIDE
    acc = None
    for j in range(0, CMP_STRIDE, 2):
        xs = jnp.concatenate([x_sc[pl.ds(j, n_chunks, stride=CMP_STRIDE), :],
                              x_sc[pl.ds(j + 1, n_chunks, stride=CMP_STRIDE), :]], axis=1).astype(BF16)
        term = _dot(xs, w_ref[j * lanes:(j + 2) * lanes, :])
        acc = term if acc is None else acc + term
    o_ref[...] = acc


def _compress_paged(cache_t, layer, page_table, pos, w1, w2, pages_per_step):
    _, n_pool, w, page = cache_t.shape
    b, n_pages = page_table.shape
    cpp = page // CMP_STRIDE
    w1big, posrows = _cmp_weights(pos, w1)
    p = pages_per_step
    a = pl.pallas_call(
        functools.partial(_cmp_a_paged_kernel, n_pages=p),
        grid_spec=pltpu.PrefetchScalarGridSpec(
            num_scalar_prefetch=1, grid=(b, n_pages // p),
            in_specs=_page_specs(p, w, page, layer, lambda c, j: c * p + j)
            + [pl.BlockSpec(w1big.shape, lambda i, c, pt: (0, 0))],
            out_specs=pl.BlockSpec((None, p * cpp, w1big.shape[1]), lambda i, c, pt: (i, c, 0)),
            scratch_shapes=[pltpu.VMEM((p * page, w), F32)]),
        out_shape=jax.ShapeDtypeStruct((b, n_pages * cpp, w1big.shape[1]), F32),
        compiler_params=_cparams(("arbitrary", "arbitrary")), name="cmp_a_paged")(
            page_table, *([cache_t] * p), w1big)
    return _cmp_b(a, posrows, w1big, w2.astype(BF16))


def _top_blocks(score, n_pick, axis):
    blk = _iota(score.shape, axis)
    n_blocks = score.shape[axis]

    def body(_, carry):
        cur, picked = carry
        best = jnp.max(cur, axis=axis, keepdims=True)
        idx = jnp.min(jnp.where(cur == best, blk, n_blocks), axis=axis, keepdims=True)
        hit = blk == idx
        return jnp.where(hit, -2.0, cur), jnp.where(hit, 1.0, picked)

    _, picked = lax.fori_loop(0, n_pick, body, (score, jnp.zeros(score.shape, F32)))
    return picked


def _nsa_prompt_kernel(slopes_ref, q_ref, gate_ref, kc_ref, vc_ref, sk_ref, sv_ref, wk_ref, wv_ref, o_ref,
                       m_sc, acc_sc, skb_sc, sva_sc, wkb_sc, wva_sc, *, tq, tk, t, lw, nbp):
    groups = [slice(g * HEAD_DIM, (g + 1) * HEAD_DIM) for g in range(B_KV)]

    @pl.when(pl.program_id(1) == 0)
    def _():
        _stage_bf16(sk_ref, skb_sc, tk)
        _stage_bf16(wk_ref, wkb_sc, tk)
        for g, gs in enumerate(groups):
            _stage_values(sv_ref, sva_sc.at[g], tk, gs, HEAD_DIM)
            _stage_values(wv_ref, wva_sc.at[g], tk, gs, HEAD_DIM)

    q0 = pl.multiple_of(pl.program_id(1) * tq, tq)
    qpos1 = q0 + _iota((tq, 1), 0)
    n_cmp_rows = kc_ref.shape[0]
    n_cmp = n_cmp_rows - (CMP_BLOCK // CMP_STRIDE - 1)
    n_blocks = t // SEL_BLOCK
    bpt = tk // SEL_BLOCK
    gates = jax.nn.sigmoid(gate_ref[...])
    cidx = _iota((1, n_cmp_rows), 1)
    c_end = cidx * CMP_STRIDE + (CMP_BLOCK - 1)
    ok_cmp = (c_end <= qpos1) & (cidx < n_cmp)
    crel = (c_end - q0).astype(F32)
    w0 = pl.multiple_of(jnp.maximum(q0 + tq - lw, 0), tq)
    wpos = w0 + _iota((1, lw), 1)
    dw = qpos1 - wpos
    ok_win = (dw >= 0) & (dw < WINDOW)
    wrel = (wpos - q0).astype(F32)
    blk_t = _iota((nbp, tq), 0)
    qpos_t = q0 + _iota((1, tq), 1)
    valid_t = (blk_t * SEL_BLOCK <= qpos_t) & (blk_t < n_blocks)
    forced_t = (blk_t == qpos_t // SEL_BLOCK) | (blk_t == 0)
    overlap_t = (_iota((nbp, n_cmp_rows), 1) * CMP_STRIDE < _iota((nbp, n_cmp_rows), 0) * SEL_BLOCK + SEL_BLOCK) & (
        _iota((nbp, n_cmp_rows), 1) * CMP_STRIDE + CMP_BLOCK > _iota((nbp, n_cmp_rows), 0) * SEL_BLOCK)
    overlap_t = overlap_t.astype(BF16)

    def biased(raw, ok, slopes2, rel):
        return jnp.concatenate(
            [jnp.where(ok, raw[r * tq:(r + 1) * tq] + slopes2[r] * rel, NEG_INF) for r in range(B_REP)], axis=0)

    for g, gs in enumerate(groups):
        q32 = jnp.concatenate(
            [q_ref[:, (g * B_REP + r) * HEAD_DIM:(g * B_REP + r + 1) * HEAD_DIM] for r in range(B_REP)],
            axis=0) * (SCALE * LOG2E)
        qb = q32.astype(BF16)
        slopes2 = [slopes_ref[g * B_REP + r] * LOG2E for r in range(B_REP)]

        sc = biased(_dot_nt_precise(q32, kc_ref[:, gs]), ok_cmp, slopes2, crel)
        mx = jnp.max(sc, axis=-1, keepdims=True)
        e = jnp.exp2(sc - mx)
        den = jnp.sum(e, axis=-1, keepdims=True)
        pc = e * jnp.where(mx > 0.5 * NEG_INF, 1.0 / den, 0.0)
        o_cmp = _dot(pc.astype(BF16), vc_ref[:, gs].astype(BF16))

        psum = pc[0:tq]
        for r in range(1, B_REP):
            psum = psum + pc[r * tq:(r + 1) * tq]
        hi, mid, lo = _split3(psum)
        imp_t = _dot_nt(overlap_t, hi) + _dot_nt(overlap_t, mid) + _dot_nt(overlap_t, lo)
        score_t = jnp.where(forced_t, 1e30, jnp.where(valid_t, imp_t, -1.0))
        picked = _top_blocks(score_t, min(SEL_TOP, n_blocks), 0).T.astype(BF16)

        m_sc[...] = jnp.full(m_sc.shape, NEG_INF, F32)
        acc_sc[...] = jnp.zeros(acc_sc.shape, F32)

        def tile(kj, masked, g=g, gs=gs, qb=qb, slopes2=slopes2, picked=picked):
            k0 = pl.multiple_of(kj * tk, tk)
            expand = (_iota((nbp, tk), 0) == kj * bpt + _iota((nbp, tk), 1) // SEL_BLOCK).astype(BF16)
            keep = _dot(picked, expand) > 0.5
            kpos = k0 + _iota((1, tk), 1)
            if masked:
                keep = keep & (qpos1 >= kpos)
            s = biased(_dot_nt(qb, skb_sc[pl.ds(k0, tk), gs]), keep, slopes2, (kpos - q0).astype(F32))
            _flash_update(s, sva_sc[g, pl.ds(k0, tk), :], m_sc, acc_sc)

        _causal_tiles(q0, tq, tk, tile)
        o_sel = acc_sc[:, 0:HEAD_DIM] / jnp.maximum(acc_sc[:, HEAD_DIM:LANES], 1e-30)

        sw = biased(_dot_nt(qb, wkb_sc[pl.ds(w0, lw), gs]), ok_win, slopes2, wrel)
        ew = jnp.exp2(sw - jnp.max(sw, axis=-1, keepdims=True))
        ow = _dot(ew.astype(BF16), wva_sc[g, pl.ds(w0, lw), :])
        o_win = ow[:, 0:HEAD_DIM] / ow[:, HEAD_DIM:LANES]

        for r in range(B_REP):
            h = g * B_REP + r
            rs = slice(r * tq, (r + 1) * tq)
            o_ref[:, h * HEAD_DIM:(h + 1) * HEAD_DIM] = (
                gates[:, 3 * h:3 * h + 1] * o_cmp[rs] + gates[:, 3 * h + 1:3 * h + 2] * o_sel[rs]
                + gates[:, 3 * h + 2:3 * h + 3] * o_win[rs])


def _nsa_prompt(q, gate, kc, vc, sk, sv, wk, wv, slopes):
    b, t, w = q.shape
    tq = 128
    tk = _row_tile(t, (512, 256, 128))
    lw = min(WINDOW + tq, t)
    kvw = B_KV * HEAD_DIM
    ncr = kc.shape[1]
    nbp = -(-(t // SEL_BLOCK) // LANES) * LANES
    full = lambda width, rows: pl.BlockSpec((None, rows, width), lambda i, j: (i, 0, 0))
    return pl.pallas_call(
        functools.partial(_nsa_prompt_kernel, tq=tq, tk=tk, t=t, lw=lw, nbp=nbp),
        grid=(b, t // tq),
        in_specs=[pl.BlockSpec(memory_space=pltpu.SMEM),
                  pl.BlockSpec((None, tq, w), lambda i, j: (i, j, 0)),
                  pl.BlockSpec((None, tq, LANES), lambda i, j: (i, j, 0)),
                  full(kvw, ncr), full(kvw, ncr), full(kvw, t), full(kvw, t), full(kvw, t), full(kvw, t)],
        out_specs=pl.BlockSpec((None, tq, w), lambda i, j: (i, j, 0)),
        out_shape=jax.ShapeDtypeStruct((b, t, w), F32),
        scratch_shapes=[pltpu.VMEM((B_REP * tq, LANES), F32), pltpu.VMEM((B_REP * tq, LANES), F32),
                        pltpu.VMEM((t, kvw), BF16), pltpu.VMEM((B_KV, t, LANES), BF16),
                        pltpu.VMEM((t, kvw), BF16), pltpu.VMEM((B_KV, t, LANES), BF16)],
        compiler_params=_cparams(("arbitrary", "arbitrary")), name="nsa_prompt")(
            slopes, q, gate, kc, vc, sk, sv, wk, wv)


NEW_PAD = 16


def _page_specs(n, rows, cols, layer, idx_fn):
    return [pl.BlockSpec((None, None, rows, cols),
                         lambda i, c, pt, j=j: (layer, pt[i, idx_fn(c, j)], 0, 0)) for j in range(n)]


def _flash_pages(qb, k_refs, v_refs, bias_fn, m_sc, l_sc, acc_sc, key_major_v=None):
    s = bias_fn(_dot(qb, jnp.concatenate([kr[...].astype(BF16) for kr in k_refs], axis=1)))
    m_old = m_sc[...]
    m_new = jnp.maximum(m_old, jnp.max(s, axis=-1, keepdims=True))
    alpha = jnp.exp(m_old - m_new)
    p = jnp.exp(s - m_new)
    l_sc[...] = alpha * l_sc[...] + jnp.sum(p, axis=-1, keepdims=True)
    if key_major_v is None:
        pv = _dot_nt(p.astype(BF16), jnp.concatenate([vr[...].astype(BF16) for vr in v_refs], axis=1))
    else:
        pv = _dot(p.astype(BF16), jnp.concatenate([key_major_v(vr) for vr in v_refs], axis=0))
    acc_sc[...] = alpha * acc_sc[...] + pv
    m_sc[...] = m_new


def _pad_new(x, b, ts):
    return jnp.pad(x.reshape(b, ts, x.shape[-1]), ((0, 0), (0, NEW_PAD - ts), (0, 0)))


def _diff_dec_kernel(pt_ref, qbd_ref, info_ref, kn_ref, vn_ref, lq1, lk1, lq2, lk2, subg_ref, *rest,
                     p, page, n_past, n_new, lam_init):
    k_refs, v_refs = rest[:p], rest[p:2 * p]
    o_ref, m_sc, l_sc, acc_sc = rest[2 * p:]
    c = pl.program_id(1)

    @pl.when(c == 0)
    def _():
        m_sc[...] = jnp.full(m_sc.shape, NEG_INF, F32)
        l_sc[...] = jnp.zeros(l_sc.shape, F32)
        acc_sc[...] = jnp.zeros(acc_sc.shape, F32)

    qb = (qbd_ref[...] * SCALE).astype(BF16)
    slope = info_ref[:, 0:1]
    qpos = info_ref[:, 1:2]
    kpos = (c * (p * page) + _iota((1, p * page), 1)).astype(F32)
    key_major = lambda vr: jnp.concatenate(
        [vr[pl.ds(h, page, stride=A_HEADS), :] for h in range(A_HEADS)], axis=1).astype(BF16)
    _flash_pages(qb, k_refs, v_refs, lambda s: s - slope * (qpos - kpos), m_sc, l_sc, acc_sc, key_major)

    @pl.when(c == pl.num_programs(1) - 1)
    def _():
        idx = _iota((1, NEW_PAD), 1)
        dist = qpos - (n_past + idx).astype(F32)
        s = _dot_nt(qb, kn_ref[...].astype(BF16))
        s = jnp.where((dist >= 0) & (idx < n_new), s - slope * dist, NEG_INF)
        _online_step(s, vn_ref[...].astype(BF16), m_sc, l_sc, acc_sc)
        full = acc_sc[...] / l_sc[...]
        rows = full.shape[0]
        rowh = (_iota((rows, 1), 0) // n_new) % A_HEADS
        o = jnp.zeros((rows, A_V), F32)
        for h in range(A_HEADS):
            o = o + jnp.where(rowh == h, full[:, h * A_V:(h + 1) * A_V], 0.0)
        half = rows // 2
        lam = _diff_lambda(lq1[...], lk1[...], lq2[...], lk2[...], lam_init)
        o = o[0:half] - lam * o[half:rows]
        o_ref[...] = _rms(o, subg_ref[...]) * (1.0 - lam_init)


def _diff_dec(q, kn, vn, cache_k, cache_v, layer, page_table, slopes, lams, subg, lam_init, pages_per_step):
    b, n_pages = page_table.shape
    ts = q.shape[0] // b
    kw, page = cache_k.shape[2], cache_k.shape[3]
    vw = A_HEADS * A_V
    n_past = n_pages * page
    p = pages_per_step
    rows = 2 * A_HEADS * ts
    q5 = q.reshape(b, ts, A_HEADS, 2, HEAD_DIM).transpose(0, 3, 2, 1, 4)
    eye_h = jnp.eye(A_HEADS, dtype=F32).reshape(1, 1, A_HEADS, 1, A_HEADS, 1, 1)
    eye_c = jnp.eye(2, dtype=F32).reshape(1, 2, 1, 1, 1, 2, 1)
    qbd = (q5[:, :, :, :, None, None, :] * eye_h * eye_c).reshape(b, rows, kw)
    slope_r = jnp.broadcast_to(slopes.reshape(1, A_HEADS, 1), (2, A_HEADS, ts)).reshape(rows)
    qpos_r = jnp.broadcast_to((n_past + jnp.arange(ts, dtype=F32)).reshape(1, 1, ts), (2, A_HEADS, ts)).reshape(rows)
    info = jnp.pad(jnp.stack([slope_r, qpos_r], axis=1), ((0, 0), (0, LANES - 2)))
    const2 = lambda i, c, pt: (0, 0)
    per_b = lambda r, w: pl.BlockSpec((None, r, w), lambda i, c, pt: (i, 0, 0))
    vec = pl.BlockSpec((1, HEAD_DIM), const2)
    in_specs = ([per_b(rows, kw), pl.BlockSpec((rows, LANES), const2), per_b(NEW_PAD, kw), per_b(NEW_PAD, vw),
                 vec, vec, vec, vec, pl.BlockSpec((1, A_V), const2)]
                + _page_specs(p, kw, page, layer, lambda c, j: c * p + j)
                + _page_specs(p, page * A_HEADS, A_V, layer, lambda c, j: c * p + j))
    out = pl.pallas_call(
        functools.partial(_diff_dec_kernel, p=p, page=page, n_past=n_past, n_new=ts, lam_init=lam_init),
        grid_spec=pltpu.PrefetchScalarGridSpec(
            num_scalar_prefetch=1, grid=(b, n_pages // p), in_specs=in_specs,
            out_specs=pl.BlockSpec((None, rows // 2, A_V), lambda i, c, pt: (i, 0, 0)),
            scratch_shapes=[pltpu.VMEM((rows, 1), F32), pltpu.VMEM((rows, 1), F32), pltpu.VMEM((rows, vw), F32)]),
        out_shape=jax.ShapeDtypeStruct((b, rows // 2, A_V), F32),
        compiler_params=_cparams(("arbitrary", "arbitrary")), name="diff_dec")(
            page_table, qbd, info, _pad_new(kn, b, ts), _pad_new(vn, b, ts),
            *[x.reshape(1, HEAD_DIM) for x in lams], subg.reshape(1, A_V),
            *([cache_k] * p), *([cache_v] * p))
    return out.reshape(b, A_HEADS, ts, A_V).transpose(0, 2, 1, 3).reshape(b * ts, A_HEADS * A_V)


def _fox_dec_kernel(pt_ref, qbd_ref, pn_ref, pnt_ref, kn_ref, vn_ref, *rest, p, page, n_new):
    k_refs, v_refs, lf_refs = rest[:p], rest[p:2 * p], rest[2 * p:3 * p]
    o_ref, m_sc, l_sc, acc_sc, run_sc = rest[3 * p:]
    c = pl.program_id(1)
    qb = (qbd_ref[...] * SCALE).astype(BF16)
    rows = qb.shape[0]
    pn_col = pn_ref[:, 0:1]

    @pl.when(c == 0)
    def _():
        m_sc[...] = jnp.full(m_sc.shape, NEG_INF, F32)
        l_sc[...] = jnp.zeros(l_sc.shape, F32)
        acc_sc[...] = jnp.zeros(acc_sc.shape, F32)
        run_sc[...] = jnp.zeros(run_sc.shape, F32)
        idx = _iota((1, NEW_PAD), 1)
        tok = _iota((rows, 1), 0) // C_HEADS
        bias = pn_col - jnp.concatenate([pnt_ref[...]] * n_new, axis=0)
        s = _dot_nt(qb, kn_ref[...].astype(BF16)) + bias
        s = jnp.where((idx <= tok) & (idx < n_new), s, NEG_INF)
        _online_step(s, vn_ref[...].astype(BF16), m_sc, l_sc, acc_sc)

    later = (_iota((page, page), 0) > _iota((page, page), 1)).astype(BF16)
    run = run_sc[...]
    sufs = [None] * p
    for j in reversed(range(p)):
        lf = lf_refs[j][...]
        sufs[j] = _dot_exact_rhs(lf, later) + run
        run = run + jnp.sum(lf, axis=-1, keepdims=True)
    run_sc[...] = run
    suf = jnp.concatenate(sufs, axis=1)
    bias = jnp.concatenate([suf] * n_new, axis=0) + pn_col
    _flash_pages(qb, k_refs, v_refs, lambda s: s + bias, m_sc, l_sc, acc_sc)

    @pl.when(c == pl.num_programs(1) - 1)
    def _():
        full = acc_sc[...] / l_sc[...]
        rowh = _iota((rows, 1), 0) % C_HEADS
        o2 = jnp.zeros((rows, LANES), F32)
        for hp in range(C_HEADS // 2):
            o2 = o2 + jnp.where(rowh // 2 == hp, full[:, hp * LANES:(hp + 1) * LANES], 0.0)
        o_ref[...] = jnp.where(rowh % 2 == 0, o2[:, 0:HEAD_DIM], o2[:, HEAD_DIM:LANES])


def _fox_dec(q, kn, vn, cum_new, cache_k, cache_v, cache_lft, layer, page_table, pages_per_step):
    b, n_pages = page_table.shape
    ts = q.shape[0] // b
    w, page = cache_k.shape[2], cache_k.shape[3]
    p = pages_per_step
    nsteps = n_pages // p
    rows = ts * C_HEADS
    eye = jnp.eye(C_HEADS, dtype=F32).reshape(1, 1, C_HEADS, C_HEADS, 1)
    qbd = (q.reshape(b, ts, C_HEADS, 1, HEAD_DIM) * eye).reshape(b, rows, w)
    pn = jnp.pad(cum_new.reshape(b, rows, 1), ((0, 0), (0, 0), (0, LANES - 1)))
    pnt = jnp.pad(cum_new.transpose(0, 2, 1), ((0, 0), (0, 0), (0, NEW_PAD - ts)))
    per_b = lambda r, ww: pl.BlockSpec((None, r, ww), lambda i, c, pt: (i, 0, 0))
    rev = lambda c, j: (nsteps - 1 - c) * p + j
    in_specs = ([per_b(rows, w), per_b(rows, LANES), per_b(C_HEADS, NEW_PAD), per_b(NEW_PAD, w), per_b(NEW_PAD, w)]
                + _page_specs(p, w, page, layer, rev) + _page_specs(p, w, page, layer, rev)
                + _page_specs(p, C_HEADS, page, layer, rev))
    out = pl.pallas_call(
        functools.partial(_fox_dec_kernel, p=p, page=page, n_new=ts),
        grid_spec=pltpu.PrefetchScalarGridSpec(
            num_scalar_prefetch=1, grid=(b, nsteps), in_specs=in_specs,
            out_specs=pl.BlockSpec((None, rows, HEAD_DIM), lambda i, c, pt: (i, 0, 0)),
            scratch_shapes=[pltpu.VMEM((rows, 1), F32), pltpu.VMEM((rows, 1), F32), pltpu.VMEM((rows, w), F32),
                            pltpu.VMEM((C_HEADS, 1), F32)]),
        out_shape=jax.ShapeDtypeStruct((b, rows, HEAD_DIM), F32),
        compiler_params=_cparams(("arbitrary", "arbitrary")), name="fox_dec")(
            page_table, qbd, pn, pnt, _pad_new(kn, b, ts), _pad_new(vn, b, ts),
            *([cache_k] * p), *([cache_v] * p), *([cache_lft] * p))
    return out.reshape(b * ts, C_HEADS * HEAD_DIM)


def _nsa_dec_kernel(pt_ref, qbd_ref, info_ref, kc_ref, vc_ref, skn_ref, svn_ref, wkn_ref, wvn_ref, wk_ref, wv_ref,
                    *rest, p, page, n_past, n_new):
    k_refs, v_refs = rest[:p], rest[p:2 * p]
    o_ref, m_sc, l_sc, acc_sc, ocmp_sc, pick_sc = rest[2 * p:]
    c = pl.program_id(1)
    q32 = qbd_ref[...] * SCALE
    qb = q32.astype(BF16)
    rows = q32.shape[0]
    gt = rows // B_REP
    slope = info_ref[:, 0:1]
    qpos = info_ref[:, 1:2]
    in_g0 = (_iota((rows, 1), 0) % gt) < n_new
    pick_cols = lambda full: jnp.where(in_g0, full[:, 0:HEAD_DIM], full[:, HEAD_DIM:2 * HEAD_DIM])
    n_blocks = n_past // SEL_BLOCK
    bps = p * page // SEL_BLOCK

    @pl.when(c == 0)
    def _():
        m_sc[...] = jnp.full(m_sc.shape, NEG_INF, F32)
        l_sc[...] = jnp.zeros(l_sc.shape, F32)
        acc_sc[...] = jnp.zeros(acc_sc.shape, F32)
        n_cmp_rows = kc_ref.shape[0]
        n_cmp = n_cmp_rows - (CMP_BLOCK // CMP_STRIDE - 1)
        cidx = _iota((1, n_cmp_rows), 1)
        dc = qpos - (cidx * CMP_STRIDE + (CMP_BLOCK - 1)).astype(F32)
        sc = _dot_nt_precise(q32, kc_ref[...]) - slope * dc
        pc = _masked_softmax_rows(sc, (dc >= 0) & (cidx < n_cmp))
        ocmp_sc[...] = pick_cols(_dot(pc.astype(BF16), vc_ref[...].astype(BF16)))
        psum = pc[0:gt]
        for r in range(1, B_REP):
            psum = psum + pc[r * gt:(r + 1) * gt]
        nbp = -(-n_blocks // LANES) * LANES
        hi, mid, lo = _split3(jnp.concatenate([psum, jnp.zeros((LANES - gt, n_cmp_rows), F32)], axis=0))
        blk_c = _iota((nbp, n_cmp_rows), 0) * SEL_BLOCK
        cmp_c = _iota((nbp, n_cmp_rows), 1) * CMP_STRIDE
        overlap_t = ((cmp_c < blk_c + SEL_BLOCK) & (cmp_c + CMP_BLOCK > blk_c)).astype(BF16)
        imp_t = _dot_nt(overlap_t, hi) + _dot_nt(overlap_t, mid) + _dot_nt(overlap_t, lo)
        blk_t = _iota((nbp, LANES), 0)
        qpos_t = n_past + _iota((1, LANES), 1) % n_new
        forced_t = (blk_t == 0) | (blk_t == qpos_t // SEL_BLOCK)
        valid_t = (blk_t * SEL_BLOCK <= qpos_t) & (blk_t < n_blocks)
        score_t = jnp.where(forced_t, 1e30, jnp.where(valid_t, imp_t, -1.0))
        picked = _top_blocks(score_t, min(SEL_TOP, n_blocks + 1) - 1, 0).T[0:gt]
        for cc in range(pick_sc.shape[0]):
            pick_sc[cc] = picked[:, cc * bps:(cc + 1) * bps]

    keys = p * page
    kpos = c * keys + _iota((1, keys), 1)
    expand = (_iota((bps, keys), 0) == _iota((bps, keys), 1) // SEL_BLOCK).astype(BF16)
    sel1 = _dot(pick_sc[c].astype(BF16), expand)
    sel = jnp.concatenate([sel1] * B_REP, axis=0)
    ds = qpos - kpos.astype(F32)
    _flash_pages(qb, k_refs, v_refs, lambda s: jnp.where((sel > 0.5) & (ds >= 0), s - slope * ds, NEG_INF),
                 m_sc, l_sc, acc_sc)

    @pl.when(c == pl.num_programs(1) - 1)
    def _():
        idx = _iota((1, NEW_PAD), 1)
        dn = qpos - (n_past + idx).astype(F32)
        new_ok = (dn >= 0) & (idx < n_new)
        s = jnp.where(new_ok, _dot_nt(qb, skn_ref[...].astype(BF16)) - slope * dn, NEG_INF)
        _online_step(s, svn_ref[...].astype(BF16), m_sc, l_sc, acc_sc)
        o_sel = pick_cols(acc_sc[...] / jnp.maximum(l_sc[...], 1e-30))
        keep = wk_ref.shape[1]
        wpos = n_past - keep + _iota((1, keep), 1)
        dw = qpos - wpos.astype(F32)
        ok1 = (dw >= 0) & (dw < WINDOW) & (wpos >= 0)
        ok2 = new_ok & (dn < WINDOW)
        s1 = jnp.where(ok1, _dot(qb, wk_ref[...].astype(BF16)) - slope * dw, NEG_INF)
        s2 = jnp.where(ok2, _dot_nt(qb, wkn_ref[...].astype(BF16)) - slope * dn, NEG_INF)
        mx = jnp.maximum(jnp.max(s1, axis=-1, keepdims=True), jnp.max(s2, axis=-1, keepdims=True))
        e1 = jnp.where(ok1, jnp.exp(s1 - mx), 0.0)
        e2 = jnp.where(ok2, jnp.exp(s2 - mx), 0.0)
        den = jnp.maximum(jnp.sum(e1, axis=-1, keepdims=True) + jnp.sum(e2, axis=-1, keepdims=True), 1e-30)
        o_win = pick_cols((_dot_nt(e1.astype(BF16), wv_ref[...].astype(BF16))
                           + _dot(e2.astype(BF16), wvn_ref[...].astype(BF16))) / den)
        gates = jax.nn.sigmoid(info_ref[:, 2:5])
        o_ref[...] = gates[:, 0:1] * ocmp_sc[...] + gates[:, 1:2] * o_sel + gates[:, 2:3] * o_win


def _nsa_dec(q, gate, kc, vc, skn, svn, wkn, wvn, win_k, win_v, cache_sk, cache_sv, layer, page_table, slopes,
             pages_per_step):
    b, n_pages = page_table.shape
    ts = q.shape[0] // b
    w, page = cache_sk.shape[2], cache_sk.shape[3]
    n_past = n_pages * page
    keep = win_k.shape[3]
    p = pages_per_step
    rows = B_HEADS * ts
    q5 = q.reshape(b, ts, B_KV, B_REP, HEAD_DIM).transpose(0, 3, 2, 1, 4)
    eye = jnp.eye(B_KV, dtype=F32).reshape(1, 1, B_KV, 1, B_KV, 1)
    qbd = (q5[:, :, :, :, None, :] * eye).reshape(b, rows, w)
    slope_r = jnp.broadcast_to(slopes.reshape(B_KV, B_REP).T.reshape(B_REP, B_KV, 1), (B_REP, B_KV, ts)).reshape(rows)
    qpos_r = jnp.broadcast_to((n_past + jnp.arange(ts, dtype=F32)).reshape(1, 1, ts), (B_REP, B_KV, ts)).reshape(rows)
    glog = gate[:, :3 * B_HEADS].reshape(b, ts, B_KV, B_REP, 3).transpose(0, 3, 2, 1, 4).reshape(b, rows, 3)
    info = jnp.concatenate([jnp.broadcast_to(jnp.stack([slope_r, qpos_r], axis=1)[None], (b, rows, 2)), glog], axis=2)
    info = jnp.pad(info, ((0, 0), (0, 0), (0, LANES - 5)))
    per_b = lambda r, ww: pl.BlockSpec((None, r, ww), lambda i, c, pt: (i, 0, 0))
    win = pl.BlockSpec((None, None, w, keep), lambda i, c, pt: (layer, i, 0, 0))
    ncr = kc.shape[1]
    in_specs = ([per_b(rows, w), per_b(rows, LANES), per_b(ncr, w), per_b(ncr, w)] + [per_b(NEW_PAD, w)] * 4
                + [win, win]
                + _page_specs(p, w, page, layer, lambda c, j: c * p + j)
                + _page_specs(p, w, page, layer, lambda c, j: c * p + j))
    out = pl.pallas_call(
        functools.partial(_nsa_dec_kernel, p=p, page=page, n_past=n_past, n_new=ts),
        grid_spec=pltpu.PrefetchScalarGridSpec(
            num_scalar_prefetch=1, grid=(b, n_pages // p), in_specs=in_specs,
            out_specs=pl.BlockSpec((None, rows, HEAD_DIM), lambda i, c, pt: (i, 0, 0)),
            scratch_shapes=[pltpu.VMEM((rows, 1), F32), pltpu.VMEM((rows, 1), F32), pltpu.VMEM((rows, w), F32),
                            pltpu.VMEM((rows, HEAD_DIM), F32), pltpu.VMEM((n_pages // p, rows // B_REP, p * page // SEL_BLOCK), F32)]),
        out_shape=jax.ShapeDtypeStruct((b, rows, HEAD_DIM), F32),
        compiler_params=_cparams(("arbitrary", "arbitrary")), name="nsa_dec")(
            page_table, qbd, info, kc, vc, _pad_new(skn, b, ts), _pad_new(svn, b, ts), _pad_new(wkn, b, ts),
            _pad_new(wvn, b, ts), win_k, win_v, *([cache_sk] * p), *([cache_sv] * p))
    return out.reshape(b, B_REP, B_KV, ts, HEAD_DIM).transpose(0, 3, 2, 1, 4).reshape(b * ts, B_HEADS * HEAD_DIM)


def _alibi_slopes(n):
    return jnp.exp2(-8.0 * jnp.arange(1, n + 1, dtype=F32) / n)


def _pages_per_step(n_pages, want):
    p = min(want, n_pages)
    while n_pages % p:
        p -= 1
    return p


def kernel(x_prompt, x_sample, cache_diff_k, cache_diff_v, cache_nsa_cmp_k, cache_nsa_cmp_v, cache_nsa_sel_k, cache_nsa_sel_v, state_nsa_win_k, state_nsa_win_v, cache_fox_k, cache_fox_v, cache_fox_logf, page_table, norm_ffn1_g, ffn1_wg, ffn1_wu, ffn1_wd, norm_mix_g, norm_ffn2_g, ffn2_wg, ffn2_wu, ffn2_wd, even_w_in, even_w_out, diff_lambda_q1, diff_lambda_k1, diff_lambda_q2, diff_lambda_k2, diff_subln_g, cmp_pos_k, cmp_w1_k, cmp_w2_k, cmp_pos_v, cmp_w1_v, cmp_w2_v, odd_w_in, odd_b_f, odd_w_out, final_norm_g):
    b, t, d = x_prompt.shape
    bs, ts, _ = x_sample.shape
    depth = norm_ffn1_g.shape[0]
    n_pages = page_table.shape[1]
    n_pool, page = cache_diff_k.shape[1], cache_diff_k.shape[2]
    keep = state_nsa_win_k.shape[2]
    n_past = n_pages * page
    assert n_past % SEL_BLOCK == 0 and page % SEL_BLOCK == 0 and page % CMP_STRIDE == 0
    assert 0 < ts <= min(SEL_BLOCK, CMP_STRIDE - 1, NEW_PAD) and (A_HEADS * ts) % 8 == 0
    assert t % LANES == 0 and n_past >= WINDOW
    kvw = B_KV * HEAD_DIM

    slopes_a = _alibi_slopes(A_HEADS)
    slopes_b = _alibi_slopes(B_HEADS)
    keys_last = lambda c: jnp.moveaxis(c, 2, -1).reshape(c.shape[0], c.shape[1], -1, c.shape[2])
    c_diff_k = keys_last(cache_diff_k)
    c_diff_v = cache_diff_v.reshape(cache_diff_v.shape[0], n_pool, page * A_HEADS, A_V)
    c_cmp_k, c_cmp_v = keys_last(cache_nsa_cmp_k), keys_last(cache_nsa_cmp_v)
    c_sel_k, c_sel_v = keys_last(cache_nsa_sel_k), keys_last(cache_nsa_sel_v)
    c_fox_k, c_fox_v = keys_last(cache_fox_k), keys_last(cache_fox_v)
    c_fox_lft = keys_last(cache_fox_logf)
    win_k, win_v = keys_last(state_nsa_win_k), keys_last(state_nsa_win_v)

    a_w = A_HEADS * 2 * HEAD_DIM
    even_cuts = [a_w, a_w, A_HEADS * A_V, B_HEADS * HEAD_DIM] + [kvw] * 6 + [3 * B_HEADS]
    odd_cuts = [C_HEADS * HEAD_DIM] * 3 + [C_HEADS]

    def split_w(w, cuts):
        out, s = [], 0
        for c in cuts:
            out.append(_pad_cols(w[:, s:s + c], -(-c // LANES) * LANES).astype(BF16))
            s += c
        return out

    xp = x_prompt.reshape(b * t, d)
    xs = x_sample.reshape(bs * ts, d)
    even_p = [[] for _ in range(8)]
    even_s = [[] for _ in range(8)]
    odd_p = [[] for _ in range(3)]
    odd_s = [[] for _ in range(3)]

    for li in range(depth):
        w1 = [ffn1_wg[li].astype(BF16), ffn1_wu[li].astype(BF16), ffn1_wd[li].astype(BF16)]
        xp = _ffn(xp, norm_ffn1_g[li], *w1)
        xs = _ffn(xs, norm_ffn1_g[li], *w1)
        if li % 2 == 0:
            e = li // 2
            lam_init = 0.8 - 0.6 * math.exp(-0.3 * li)
            lams = (diff_lambda_q1[e], diff_lambda_k1[e], diff_lambda_q2[e], diff_lambda_k2[e])
            ws = split_w(even_w_in[e], even_cuts)
            wo = even_w_out[e].astype(BF16)
            wo_a, wo_b = wo[:A_HEADS * A_V], wo[A_HEADS * A_V:]
            aq, ak, av, bq, ck, cv, sk, sv, wk, wv, bg = _norm_proj(xp, norm_mix_g[li], ws)
            r3 = lambda a: a.reshape(b, t, a.shape[-1])
            oa = _diff_prompt(r3(aq), r3(ak), r3(av), slopes_a, lams, diff_subln_g[e], lam_init)
            kc = _compress_prompt(r3(ck), cmp_pos_k[e], cmp_w1_k[e], cmp_w2_k[e])
            vc = _compress_prompt(r3(cv), cmp_pos_v[e], cmp_w1_v[e], cmp_w2_v[e])
            ob = _nsa_prompt(r3(bq), r3(bg), kc, vc, r3(sk), r3(sv), r3(wk), r3(wv), slopes_b)
            xp = _out_proj(xp, [oa.reshape(b * t, -1), ob.reshape(b * t, -1)], [wo_a, wo_b])

            def last_rows(a):
                a = jnp.pad(r3(a), ((0, 0), (max(0, keep - t), 0), (0, 0)))
                return a[:, a.shape[1] - keep:].reshape(b, keep, B_KV, HEAD_DIM)

            rows_p = (ak.reshape(b, t, A_HEADS, 2, HEAD_DIM), av.reshape(b, t, A_HEADS, A_V),
                      ck.reshape(b, t, B_KV, HEAD_DIM), cv.reshape(b, t, B_KV, HEAD_DIM),
                      sk.reshape(b, t, B_KV, HEAD_DIM), sv.reshape(b, t, B_KV, HEAD_DIM), last_rows(wk), last_rows(wv))
            for lst, r in zip(even_p, rows_p):
                lst.append(r)
            aq, ak, av, bq, ck, cv, sk, sv, wk, wv, bg = _norm_proj(xs, norm_mix_g[li], ws)
            oa = _diff_dec(aq, ak, av, c_diff_k, c_diff_v, e, page_table, slopes_a, lams, diff_subln_g[e], lam_init,
                           _pages_per_step(n_pages, 16))
            pps = _pages_per_step(n_pages, 16)
            kc = _compress_paged(c_cmp_k, e, page_table, cmp_pos_k[e], cmp_w1_k[e], cmp_w2_k[e], pps)
            vc = _compress_paged(c_cmp_v, e, page_table, cmp_pos_v[e], cmp_w1_v[e], cmp_w2_v[e], pps)
            ob = _nsa_dec(bq, bg, kc, vc, sk, sv, wk, wv, win_k, win_v, c_sel_k, c_sel_v, e, page_table, slopes_b, pps)
            xs = _out_proj(xs, [oa, ob], [wo_a, wo_b])
            s3 = lambda a: a.reshape(bs, ts, a.shape[-1])
            new_win = lambda buf, a: jnp.concatenate(
                [buf[e], s3(a).reshape(bs, ts, B_KV, HEAD_DIM)], axis=1)[:, ts:]
            rows_s = (ak.reshape(bs, ts, A_HEADS, 2, HEAD_DIM), av.reshape(bs, ts, A_HEADS, A_V),
                      ck.reshape(bs, ts, B_KV, HEAD_DIM), cv.reshape(bs, ts, B_KV, HEAD_DIM),
                      sk.reshape(bs, ts, B_KV, HEAD_DIM), sv.reshape(bs, ts, B_KV, HEAD_DIM),
                      new_win(state_nsa_win_k, wk), new_win(state_nsa_win_v, wv))
            for lst, r in zip(even_s, rows_s):
                lst.append(r)
        else:
            o = li // 2
            ws = split_w(odd_w_in[o], odd_cuts)
            wo = odd_w_out[o].astype(BF16)
            q, k, v, f = _norm_proj(xp, norm_mix_g[li], ws)
            r3 = lambda a: a.reshape(b, t, a.shape[-1])
            logf, cum = _logf_cum(r3(f), odd_b_f[o])
            om = _fox_prompt(r3(q), r3(k), r3(v), cum)
            xp = _out_proj(xp, [om.reshape(b * t, -1)], [wo])
            for lst, r in zip(odd_p, (k.reshape(b, t, C_HEADS, HEAD_DIM), v.reshape(b, t, C_HEADS, HEAD_DIM), logf)):
                lst.append(r)
            q, k, v, f = _norm_proj(xs, norm_mix_g[li], ws)
            logf, cum = _logf_cum(f.reshape(bs, ts, -1), odd_b_f[o])
            om = _fox_dec(q, k, v, cum, c_fox_k, c_fox_v, c_fox_lft, o, page_table, _pages_per_step(n_pages, 8))
            xs = _out_proj(xs, [om], [wo])
            for lst, r in zip(odd_s, (k.reshape(bs, ts, C_HEADS, HEAD_DIM), v.reshape(bs, ts, C_HEADS, HEAD_DIM), logf)):
                lst.append(r)
        w2 = [ffn2_wg[li].astype(BF16), ffn2_wu[li].astype(BF16), ffn2_wd[li].astype(BF16)]
        fin = final_norm_g if li == depth - 1 else None
        xp = _ffn(xp, norm_ffn2_g[li], *w2, final_g=fin)
        xs = _ffn(xs, norm_ffn2_g[li], *w2, final_g=fin)

    stk = lambda lst: jnp.stack(lst, 0)
    ep = [stk(l) for l in even_p]
    es = [stk(l) for l in even_s]
    op = [stk(l) for l in odd_p]
    os_ = [stk(l) for l in odd_s]
    out = [xp.reshape(b, t, d), xs.reshape(bs, ts, d)]
    for p_, s_ in zip(ep, es):
        out += [p_, s_]
    for p_, s_ in zip(op, os_):
        out += [p_, s_]
    return tuple(out)
```

```python
import functools
import math

import jax
import jax.numpy as jnp
from jax import lax
from jax.experimental import pallas as pl
from jax.experimental.pallas import tpu as pltpu

HEAD_DIM = 64
A_HEADS = 4
A_V = 2 * HEAD_DIM
B_HEADS = 8
B_KV = 2
B_REP = B_HEADS // B_KV
CMP_BLOCK = 32
CMP_STRIDE = 16
CMP_HIDDEN = 128
SEL_BLOCK = 64
SEL_TOP = 16
WINDOW = 512
C_HEADS = 16
RMS_EPS = 1e-6
NEG_INF = -1e30
SCALE = HEAD_DIM ** -0.5

LANES = 128
V7X_VMEM_LIMIT_BYTES = 56 * 1024 * 1024

F32 = jnp.float32
BF16 = jnp.bfloat16


def _cparams(sem):
    return pltpu.CompilerParams(dimension_semantics=sem, vmem_limit_bytes=V7X_VMEM_LIMIT_BYTES)


def _dot(a, b):
    return jnp.dot(a, b, preferred_element_type=F32)


def _dot_nt(a, b):
    return lax.dot_general(a, b, (((1,), (1,)), ((), ())), preferred_element_type=F32)


def _split3(x):
    hi = x.astype(BF16)
    r1 = x - hi.astype(F32)
    mid = r1.astype(BF16)
    lo = (r1 - mid.astype(F32)).astype(BF16)
    return hi, mid, lo


def _dot_exact_rhs(x, m_bf16):
    hi, mid, lo = _split3(x)
    return _dot(hi, m_bf16) + _dot(mid, m_bf16) + _dot(lo, m_bf16)


def _dot_nt_precise(a, b):
    ah = a.astype(BF16)
    al = (a - ah.astype(F32)).astype(BF16)
    bh = b.astype(BF16)
    bl = (b - bh.astype(F32)).astype(BF16)
    return _dot_nt(ah, bh) + _dot_nt(ah, bl) + _dot_nt(al, bh)


def _rms(x, g):
    ms = jnp.mean(x * x, axis=-1, keepdims=True)
    return x * lax.rsqrt(ms + RMS_EPS) * g


def _iota(shape, dim):
    return lax.broadcasted_iota(jnp.int32, shape, dim)


def _masked_softmax_rows(s, mask):
    s = jnp.where(mask, s, NEG_INF)
    e = jnp.where(mask, jnp.exp(s - jnp.max(s, axis=-1, keepdims=True)), 0.0)
    return e / jnp.maximum(jnp.sum(e, axis=-1, keepdims=True), 1e-30)


def _online_step(s, v_bf16, m_ref, l_ref, acc_ref):
    m_old = m_ref[...]
    m_new = jnp.maximum(m_old, jnp.max(s, axis=-1, keepdims=True))
    alpha = jnp.exp(m_old - m_new)
    p = jnp.exp(s - m_new)
    l_ref[...] = alpha * l_ref[...] + jnp.sum(p, axis=-1, keepdims=True)
    acc_ref[...] = alpha * acc_ref[...] + _dot(p.astype(BF16), v_bf16)
    m_ref[...] = m_new


def _row_tile(n, pref):
    for t in pref:
        if n % t == 0:
            return t
    return n


def _ffn_kernel(x_ref, g_ref, wg_ref, wu_ref, wd_ref, *rest, fc, nfc, has_final):
    if has_final:
        gf_ref, o_ref = rest
    else:
        (o_ref,) = rest
    x = x_ref[...]
    h = _rms(x, g_ref[...]).astype(BF16)
    acc = jnp.zeros(x.shape, F32)
    for c in range(nfc):
        sl = slice(c * fc, (c + 1) * fc)
        a = _dot(h, wg_ref[:, sl])
        u = _dot(h, wu_ref[:, sl])
        act = (a * jax.nn.sigmoid(a) * u).astype(BF16)
        acc = acc + _dot(act, wd_ref[sl, :])
    y = x + 0.5 * acc
    if has_final:
        y = _rms(y, gf_ref[...])
    o_ref[...] = y


def _ffn(x, g, wg, wu, wd, final_g=None):
    n, d = x.shape
    f = wg.shape[1]
    tm = _row_tile(n, (512, 256, 128))
    fc = _row_tile(f, (256, 128))
    const = lambda i: (0, 0)
    in_specs = [pl.BlockSpec((tm, d), lambda i: (i, 0)), pl.BlockSpec((1, d), const),
                pl.BlockSpec((d, f), const), pl.BlockSpec((d, f), const), pl.BlockSpec((f, d), const)]
    args = [x, g.reshape(1, d), wg, wu, wd]
    if final_g is not None:
        in_specs.append(pl.BlockSpec((1, d), const))
        args.append(final_g.reshape(1, d))
    return pl.pallas_call(
        functools.partial(_ffn_kernel, fc=fc, nfc=f // fc, has_final=final_g is not None),
        grid=(n // tm,), in_specs=in_specs, out_specs=pl.BlockSpec((tm, d), lambda i: (i, 0)),
        out_shape=jax.ShapeDtypeStruct((n, d), F32), compiler_params=_cparams(("arbitrary",)),
        name="ffn")(*args)


def _norm_proj_kernel(x_ref, g_ref, *refs, n_out):
    h = _rms(x_ref[...], g_ref[...]).astype(BF16)
    for w_ref, o_ref in zip(refs[:n_out], refs[n_out:]):
        o_ref[...] = _dot(h, w_ref[...])


def _norm_proj(x, g, ws):
    n, d = x.shape
    tm = _row_tile(n, (512, 256, 128))
    const = lambda i: (0, 0)
    in_specs = [pl.BlockSpec((tm, d), lambda i: (i, 0)), pl.BlockSpec((1, d), const)]
    in_specs += [pl.BlockSpec(w.shape, const) for w in ws]
    return pl.pallas_call(
        functools.partial(_norm_proj_kernel, n_out=len(ws)),
        grid=(n // tm,), in_specs=in_specs,
        out_specs=[pl.BlockSpec((tm, w.shape[1]), lambda i: (i, 0)) for w in ws],
        out_shape=[jax.ShapeDtypeStruct((n, w.shape[1]), F32) for w in ws],
        compiler_params=_cparams(("arbitrary",)), name="norm_proj")(x, g.reshape(1, d), *ws)


def _out_proj_kernel(x_ref, *refs, n_in):
    y = x_ref[...]
    for o_ref, w_ref in zip(refs[:n_in], refs[n_in:2 * n_in]):
        y = y + _dot(o_ref[...].astype(BF16), w_ref[...])
    refs[2 * n_in][...] = y


def _out_proj(x, os_, ws):
    n, d = x.shape
    tm = _row_tile(n, (512, 256, 128))
    const = lambda i: (0, 0)
    in_specs = [pl.BlockSpec((tm, d), lambda i: (i, 0))]
    in_specs += [pl.BlockSpec((tm, o.shape[1]), lambda i: (i, 0)) for o in os_]
    in_specs += [pl.BlockSpec(w.shape, const) for w in ws]
    return pl.pallas_call(
        functools.partial(_out_proj_kernel, n_in=len(os_)),
        grid=(n // tm,), in_specs=in_specs, out_specs=pl.BlockSpec((tm, d), lambda i: (i, 0)),
        out_shape=jax.ShapeDtypeStruct((n, d), F32), compiler_params=_cparams(("arbitrary",)),
        name="out_proj")(x, *os_, *ws)


def _pad_cols(w, width):
    return jnp.pad(w, ((0, 0), (0, width - w.shape[1])))


def _logf_cum_kernel(f_ref, b_ref, lf_ref, cum_ref, carry_ref, *, tc):
    @pl.when(pl.program_id(1) == 0)
    def _():
        carry_ref[...] = jnp.zeros(carry_ref.shape, F32)

    z = f_ref[...] + b_ref[...]
    lf = jnp.minimum(z, 0.0) - jnp.log1p(jnp.exp(-jnp.abs(z)))
    tri = (_iota((tc, tc), 0) >= _iota((tc, tc), 1)).astype(BF16)
    cum = _dot_exact_rhs_left(tri, lf) + carry_ref[...]
    lf_ref[...] = lf[:, :C_HEADS]
    cum_ref[...] = cum[:, :C_HEADS]
    carry_ref[...] = cum[tc - 1:tc, :]


def _dot_exact_rhs_left(m_bf16, x):
    hi, mid, lo = _split3(x)
    return _dot(m_bf16, hi) + _dot(m_bf16, mid) + _dot(m_bf16, lo)


def _logf_cum(f, b_f):
    b, t, _ = f.shape
    tc = _row_tile(t, (256, 128))
    bias = jnp.pad(b_f, (0, LANES - C_HEADS)).reshape(1, LANES)
    out = jax.ShapeDtypeStruct((b, t, C_HEADS), F32)
    return pl.pallas_call(
        functools.partial(_logf_cum_kernel, tc=tc),
        grid=(b, t // tc),
        in_specs=[pl.BlockSpec((None, tc, LANES), lambda i, j: (i, j, 0)), pl.BlockSpec((1, LANES), lambda i, j: (0, 0))],
        out_specs=[pl.BlockSpec((None, tc, C_HEADS), lambda i, j: (i, j, 0))] * 2,
        out_shape=[out, out], scratch_shapes=[pltpu.VMEM((1, LANES), F32)],
        compiler_params=_cparams(("arbitrary", "arbitrary")), name="logf_cum")(f, bias)


LOG2E = 1.4426950408889634


def _lanes(x, n):
    if n % LANES == 0:
        return x if n == LANES else jnp.tile(x, (1, n // LANES))
    return x[:, :n]


def _flash_update(s, v_aug, m_ref, acc_ref):
    m_prev = m_ref[...]
    m_next = jnp.maximum(m_prev, jnp.max(s, axis=-1, keepdims=True))
    p = jnp.exp2(s - _lanes(m_next, s.shape[1]))
    alpha = jnp.exp2(m_prev - m_next)
    acc_ref[...] = acc_ref[...] * _lanes(alpha, acc_ref.shape[1]) + _dot(p.astype(BF16), v_aug)
    m_ref[...] = m_next


def _causal_tiles(q0, tq, tk, tile_fn):
    n_full = q0 // tk
    n_k = (q0 + tq + tk - 1) // tk

    def run(masked):
        def body(kj, carry):
            tile_fn(kj, masked)
            return carry
        return body

    lax.fori_loop(0, n_full, run(False), 0)
    lax.fori_loop(n_full, n_k, run(True), 0)


def _stage_bf16(src_ref, dst_ref, tk):
    for j in range(src_ref.shape[0] // tk):
        rows = slice(j * tk, (j + 1) * tk)
        dst_ref[rows, :] = src_ref[rows, :].astype(BF16)


def _stage_values(src_ref, dst_ref, tk, cols, width):
    ones = jnp.ones((tk, width), BF16)
    for j in range(src_ref.shape[0] // tk):
        rows = slice(j * tk, (j + 1) * tk)
        dst_ref[rows, :] = jnp.concatenate([src_ref[rows, cols].astype(BF16), ones], axis=1)


def _fox_prompt_kernel(q_ref, k_ref, v_ref, fq_ref, fk_ref, o_ref, m_sc, acc_sc, kb_sc, va_sc, *, tq, tk, nsub):
    subs = [slice(c * HEAD_DIM, (c + 1) * HEAD_DIM) for c in range(nsub)]

    @pl.when(pl.program_id(2) == 0)
    def _():
        _stage_bf16(k_ref, kb_sc, tk)
        for c, cs in enumerate(subs):
            _stage_values(v_ref, va_sc.at[c], tk, cs, HEAD_DIM)

    q0 = pl.program_id(2) * tq
    qpos = q0 + _iota((tq, 1), 0)
    m_sc[...] = jnp.full(m_sc.shape, NEG_INF, F32)
    acc_sc[...] = jnp.zeros(acc_sc.shape, F32)
    qs = [(q_ref[:, cs] * (SCALE * LOG2E)).astype(BF16) for cs in subs]
    f0 = [fq_ref[0:1, c:c + 1] for c in range(nsub)]

    def tile(kj, masked):
        k0 = pl.multiple_of(kj * tk, tk)
        if masked:
            ok = qpos >= k0 + _iota((1, tk), 1)
        for c, cs in enumerate(subs):
            s = _dot_nt(qs[c], kb_sc[pl.ds(k0, tk), cs]) + (f0[c] - fk_ref[kj, c:c + 1, :]) * LOG2E
            if masked:
                s = jnp.where(ok, s, NEG_INF)
            _flash_update(s, va_sc[c, pl.ds(k0, tk), :], m_sc.at[c], acc_sc.at[c])

    _causal_tiles(q0, tq, tk, tile)
    for c, cs in enumerate(subs):
        o_ref[:, cs] = acc_sc[c][:, 0:HEAD_DIM] / acc_sc[c][:, HEAD_DIM:LANES]


def _fox_prompt(q, k, v, cum):
    b, t, w = q.shape
    tq = _row_tile(t, (512, 256, 128))
    tk = tq
    nsub = 4
    ngrp = C_HEADS // nsub
    wb = nsub * HEAD_DIM
    fq = cum.reshape(b, t, ngrp, nsub).transpose(0, 2, 1, 3)
    fk = cum.reshape(b, t // tk, tk, ngrp, nsub).transpose(0, 3, 1, 4, 2)
    return pl.pallas_call(
        functools.partial(_fox_prompt_kernel, tq=tq, tk=tk, nsub=nsub),
        grid=(b, ngrp, t // tq),
        in_specs=[pl.BlockSpec((None, tq, wb), lambda i, h, j: (i, j, h)),
                  pl.BlockSpec((None, t, wb), lambda i, h, j: (i, 0, h)),
                  pl.BlockSpec((None, t, wb), lambda i, h, j: (i, 0, h)),
                  pl.BlockSpec((None, None, tq, nsub), lambda i, h, j: (i, h, j, 0)),
                  pl.BlockSpec((None, None, t // tk, nsub, tk), lambda i, h, j: (i, h, 0, 0, 0))],
        out_specs=pl.BlockSpec((None, tq, wb), lambda i, h, j: (i, j, h)),
        out_shape=jax.ShapeDtypeStruct((b, t, w), F32),
        scratch_shapes=[pltpu.VMEM((nsub, tq, LANES), F32), pltpu.VMEM((nsub, tq, LANES), F32),
                        pltpu.VMEM((t, wb), BF16), pltpu.VMEM((nsub, t, LANES), BF16)],
        compiler_params=_cparams(("arbitrary",) * 3), name="fox_prompt")(q, k, v, fq, fk)


def _diff_lambda(lq1, lk1, lq2, lk2, lam_init):
    return (jnp.exp(jnp.sum(lq1 * lk1, axis=-1, keepdims=True))
            - jnp.exp(jnp.sum(lq2 * lk2, axis=-1, keepdims=True)) + lam_init)


def _diff_prompt_kernel(slopes_ref, q_ref, k_ref, v_ref, lq1, lk1, lq2, lk2, subg_ref, o_ref,
                        m_sc, acc_sc, kb_sc, va_sc, *, tq, tk, lam_init, nh):
    @pl.when(pl.program_id(2) == 0)
    def _():
        _stage_bf16(k_ref, kb_sc, tk)
        for h in range(nh):
            _stage_values(v_ref, va_sc.at[h], tk, slice(h * A_V, (h + 1) * A_V), A_V)

    q0 = pl.program_id(2) * tq
    qpos = q0 + _iota((tq, 1), 0)
    m_sc[...] = jnp.full(m_sc.shape, NEG_INF, F32)
    acc_sc[...] = jnp.zeros(acc_sc.shape, F32)
    subs = [slice(c * HEAD_DIM, (c + 1) * HEAD_DIM) for c in range(2 * nh)]
    qs = [(q_ref[:, cs] * (SCALE * LOG2E)).astype(BF16) for cs in subs]
    slope2 = [slopes_ref[pl.program_id(1) * nh + h] * LOG2E for h in range(nh)]

    def tile(kj, masked):
        k0 = pl.multiple_of(kj * tk, tk)
        kpos = k0 + _iota((1, tk), 1)
        krel = (kpos - q0).astype(F32)
        if masked:
            ok = qpos >= kpos
        for c, cs in enumerate(subs):
            s = _dot_nt(qs[c], kb_sc[pl.ds(k0, tk), cs]) + slope2[c // 2] * krel
            if masked:
                s = jnp.where(ok, s, NEG_INF)
            _flash_update(s, va_sc[c // 2, pl.ds(k0, tk), :], m_sc.at[c], acc_sc.at[c])

    _causal_tiles(q0, tq, tk, tile)
    lam = _diff_lambda(lq1[...], lk1[...], lq2[...], lk2[...], lam_init)
    for h in range(nh):
        a0, a1 = acc_sc[2 * h], acc_sc[2 * h + 1]
        o = a0[:, 0:A_V] / a0[:, A_V:2 * A_V] - lam * (a1[:, 0:A_V] / a1[:, A_V:2 * A_V])
        o_ref[:, h * A_V:(h + 1) * A_V] = _rms(o, subg_ref[...]) * (1.0 - lam_init)


def _diff_prompt(q, k, v, slopes, lams, subg, lam_init):
    b, t, w = q.shape
    tq = _row_tile(t, (512, 256, 128))
    tk = tq
    nh = 2
    wb = nh * A_V
    vec = pl.BlockSpec((1, HEAD_DIM), lambda i, h, j: (0, 0))
    return pl.pallas_call(
        functools.partial(_diff_prompt_kernel, tq=tq, tk=tk, lam_init=lam_init, nh=nh),
        grid=(b, A_HEADS // nh, t // tq),
        in_specs=[pl.BlockSpec(memory_space=pltpu.SMEM),
                  pl.BlockSpec((None, tq, wb), lambda i, h, j: (i, j, h)),
                  pl.BlockSpec((None, t, wb), lambda i, h, j: (i, 0, h)),
                  pl.BlockSpec((None, t, wb), lambda i, h, j: (i, 0, h)),
                  vec, vec, vec, vec, pl.BlockSpec((1, A_V), lambda i, h, j: (0, 0))],
        out_specs=pl.BlockSpec((None, tq, wb), lambda i, h, j: (i, j, h)),
        out_shape=jax.ShapeDtypeStruct((b, t, w), F32),
        scratch_shapes=[pltpu.VMEM((2 * nh, tq, LANES), F32), pltpu.VMEM((2 * nh, tq, 2 * A_V), F32),
                        pltpu.VMEM((t, wb), BF16), pltpu.VMEM((nh, t, 2 * A_V), BF16)],
        compiler_params=_cparams(("arbitrary",) * 3), name="diff_prompt")(
            slopes, q, k, v, *[x.reshape(1, HEAD_DIM) for x in lams], subg.reshape(1, A_V))


def _cmp_weights(pos, w1):
    ratio = CMP_BLOCK // CMP_STRIDE
    w1r = w1.reshape(ratio, CMP_STRIDE, 1, HEAD_DIM, 1, CMP_HIDDEN)
    eye = jnp.eye(B_KV, dtype=w1.dtype).reshape(1, 1, B_KV, 1, B_KV, 1)
    big = (w1r * eye).reshape(ratio, CMP_STRIDE * B_KV * HEAD_DIM, B_KV * CMP_HIDDEN)
    w1big = jnp.concatenate([big[r] for r in range(ratio)], axis=1)
    posr = jnp.broadcast_to(pos.reshape(ratio, CMP_STRIDE, 1, HEAD_DIM), (ratio, CMP_STRIDE, B_KV, HEAD_DIM))
    posrows = jnp.pad(posr.reshape(ratio, CMP_STRIDE * B_KV * HEAD_DIM), ((0, 8 - ratio), (0, 0)))
    return w1big.astype(BF16), posrows


def _cmp_a_kernel(x_ref, w_ref, o_ref):
    o_ref[...] = _dot(x_ref[...].astype(BF16), w_ref[...])


def _cmp_b_kernel(a_ref, posrows_ref, w1_ref, w2_ref, o_ref):
    n = a_ref.shape[0]
    hw = B_KV * CMP_HIDDEN
    pc = _dot(posrows_ref[...].astype(BF16), w1_ref[...])
    posc = pc[0:1, 0:hw] + pc[1:2, hw:2 * hw]
    a = a_ref[...]
    nxt = pltpu.roll(a[:, hw:2 * hw], n - 1, 0)
    pre = a[:, 0:hw] + nxt + posc
    act = (pre * jax.nn.sigmoid(pre)).astype(BF16)
    w2 = w2_ref[...]
    for g in range(B_KV):
        o_ref[:, g * HEAD_DIM:(g + 1) * HEAD_DIM] = _dot(act[:, g * CMP_HIDDEN:(g + 1) * CMP_HIDDEN], w2)


def _cmp_b(a, posrows, w1big, w2):
    b, n, wa = a.shape
    const = lambda i: (0, 0)
    return pl.pallas_call(
        _cmp_b_kernel, grid=(b,),
        in_specs=[pl.BlockSpec((None, n, wa), lambda i: (i, 0, 0)), pl.BlockSpec(posrows.shape, const),
                  pl.BlockSpec(w1big.shape, const), pl.BlockSpec(w2.shape, const)],
        out_specs=pl.BlockSpec((None, n, B_KV * HEAD_DIM), lambda i: (i, 0, 0)),
        out_shape=jax.ShapeDtypeStruct((b, n, B_KV * HEAD_DIM), F32),
        compiler_params=_cparams(("arbitrary",)), name="cmp_b")(a, posrows, w1big, w2)


def _compress_prompt(rows, pos, w1, w2):
    b, t, w = rows.shape
    n = t // CMP_STRIDE
    w1big, posrows = _cmp_weights(pos, w1)
    view = rows.reshape(b * n, CMP_STRIDE * w)
    tm = _row_tile(b * n, (256, 128))
    a = pl.pallas_call(
        _cmp_a_kernel, grid=(b * n // tm,),
        in_specs=[pl.BlockSpec((tm, CMP_STRIDE * w), lambda i: (i, 0)), pl.BlockSpec(w1big.shape, lambda i: (0, 0))],
        out_specs=pl.BlockSpec((tm, w1big.shape[1]), lambda i: (i, 0)),
        out_shape=jax.ShapeDtypeStruct((b * n, w1big.shape[1]), F32),
        compiler_params=_cparams(("arbitrary",)), name="cmp_a")(view, w1big)
    return _cmp_b(a.reshape(b, n, -1), posrows, w1big, w2.astype(BF16))


def _cmp_a_paged_kernel(pt_ref, *refs, n_pages):
    x_sc = refs[-1]
    for which in range(2):
        page_refs = refs[which * n_pages:(which + 1) * n_pages]
        w_ref, o_ref = refs[2 * n_pages + which], refs[2 * n_pages + 2 + which]
        lanes, page = page_refs[0].shape
        for i in range(n_pages):
            x_sc[i * page:(i + 1) * page, :] = page_refs[i][...].T
        n_chunks = n_pages * page // CMP_STRIDE
        acc = None
        for j in range(0, CMP_STRIDE, 2):
            xs = jnp.concatenate([x_sc[pl.ds(j, n_chunks, stride=CMP_STRIDE), :],
                                  x_sc[pl.ds(j + 1, n_chunks, stride=CMP_STRIDE), :]], axis=1).astype(BF16)
            term = _dot(xs, w_ref[j * lanes:(j + 2) * lanes, :])
            acc = term if acc is None else acc + term
        o_ref[...] = acc


def _compress_paged(cache_k, cache_v, layer, page_table, pos_k, w1_k, w2_k, pos_v, w1_v, w2_v, pages_per_step):
    _, n_pool, w, page = cache_k.shape
    b, n_pages = page_table.shape
    cpp = page // CMP_STRIDE
    w1big_k, posrows_k = _cmp_weights(pos_k, w1_k)
    w1big_v, posrows_v = _cmp_weights(pos_v, w1_v)
    p = pages_per_step
    pages = _page_specs(p, w, page, layer, lambda c, j: c * p + j)
    wspec = pl.BlockSpec(w1big_k.shape, lambda i, c, pt: (0, 0))
    ospec = pl.BlockSpec((None, p * cpp, w1big_k.shape[1]), lambda i, c, pt: (i, c, 0))
    oshape = jax.ShapeDtypeStruct((b, n_pages * cpp, w1big_k.shape[1]), F32)
    a_k, a_v = pl.pallas_call(
        functools.partial(_cmp_a_paged_kernel, n_pages=p),
        grid_spec=pltpu.PrefetchScalarGridSpec(
            num_scalar_prefetch=1, grid=(b, n_pages // p),
            in_specs=pages + pages + [wspec, wspec], out_specs=[ospec, ospec],
            scratch_shapes=[pltpu.VMEM((p * page, w), F32)]),
        out_shape=[oshape, oshape],
        compiler_params=_cparams(("arbitrary", "arbitrary")), name="cmp_a_paged")(
            page_table, *([cache_k] * p), *([cache_v] * p), w1big_k, w1big_v)
    return (_cmp_b(a_k, posrows_k, w1big_k, w2_k.astype(BF16)), _cmp_b(a_v, posrows_v, w1big_v, w2_v.astype(BF16)))


def _top_blocks(score, n_pick, axis):
    blk = _iota(score.shape, axis)
    n_blocks = score.shape[axis]

    def body(_, carry):
        cur, picked = carry
        best = jnp.max(cur, axis=axis, keepdims=True)
        idx = jnp.min(jnp.where(cur == best, blk, n_blocks), axis=axis, keepdims=True)
        hit = blk == idx
        return jnp.where(hit, -2.0, cur), jnp.where(hit, 1.0, picked)

    _, picked = lax.fori_loop(0, n_pick, body, (score, jnp.zeros(score.shape, F32)))
    return picked


def _nsa_prompt_kernel(slopes_ref, q_ref, gate_ref, kc_ref, vc_ref, sk_ref, sv_ref, wk_ref, wv_ref, o_ref,
                       m_sc, acc_sc, skb_sc, sva_sc, wkb_sc, wva_sc, *, tq, tk, t, lw, nbp):
    groups = [slice(g * HEAD_DIM, (g + 1) * HEAD_DIM) for g in range(B_KV)]

    @pl.when(pl.program_id(1) == 0)
    def _():
        _stage_bf16(sk_ref, skb_sc, tk)
        _stage_bf16(wk_ref, wkb_sc, tk)
        for g, gs in enumerate(groups):
            _stage_values(sv_ref, sva_sc.at[g], tk, gs, HEAD_DIM)
            _stage_values(wv_ref, wva_sc.at[g], tk, gs, HEAD_DIM)

    q0 = pl.multiple_of(pl.program_id(1) * tq, tq)
    qpos1 = q0 + _iota((tq, 1), 0)
    n_cmp_rows = kc_ref.shape[0]
    n_cmp = n_cmp_rows - (CMP_BLOCK // CMP_STRIDE - 1)
    n_blocks = t // SEL_BLOCK
    bpt = tk // SEL_BLOCK
    gates = jax.nn.sigmoid(gate_ref[...])
    cidx = _iota((1, n_cmp_rows), 1)
    c_end = cidx * CMP_STRIDE + (CMP_BLOCK - 1)
    ok_cmp = (c_end <= qpos1) & (cidx < n_cmp)
    crel = (c_end - q0).astype(F32)
    w0 = pl.multiple_of(jnp.maximum(q0 + tq - lw, 0), tq)
    wpos = w0 + _iota((1, lw), 1)
    dw = qpos1 - wpos
    ok_win = (dw >= 0) & (dw < WINDOW)
    wrel = (wpos - q0).astype(F32)
    blk_t = _iota((nbp, tq), 0)
    qpos_t = q0 + _iota((1, tq), 1)
    valid_t = (blk_t * SEL_BLOCK <= qpos_t) & (blk_t < n_blocks)
    forced_t = (blk_t == qpos_t // SEL_BLOCK) | (blk_t == 0)
    overlap_t = (_iota((nbp, n_cmp_rows), 1) * CMP_STRIDE < _iota((nbp, n_cmp_rows), 0) * SEL_BLOCK + SEL_BLOCK) & (
        _iota((nbp, n_cmp_rows), 1) * CMP_STRIDE + CMP_BLOCK > _iota((nbp, n_cmp_rows), 0) * SEL_BLOCK)
    overlap_t = overlap_t.astype(BF16)

    def biased(raw, ok, slopes2, rel):
        return jnp.concatenate(
            [jnp.where(ok, raw[r * tq:(r + 1) * tq] + slopes2[r] * rel, NEG_INF) for r in range(B_REP)], axis=0)

    per_group = []
    for g, gs in enumerate(groups):
        q32 = jnp.concatenate(
            [q_ref[:, (g * B_REP + r) * HEAD_DIM:(g * B_REP + r + 1) * HEAD_DIM] for r in range(B_REP)],
            axis=0) * (SCALE * LOG2E)
        qb = q32.astype(BF16)
        slopes2 = [slopes_ref[g * B_REP + r] * LOG2E for r in range(B_REP)]

        sc = biased(_dot_nt_precise(q32, kc_ref[:, gs]), ok_cmp, slopes2, crel)
        mx = jnp.max(sc, axis=-1, keepdims=True)
        e = jnp.exp2(sc - mx)
        den = jnp.sum(e, axis=-1, keepdims=True)
        pc = e * jnp.where(mx > 0.5 * NEG_INF, 1.0 / den, 0.0)
        o_cmp = _dot(pc.astype(BF16), vc_ref[:, gs].astype(BF16))

        psum = pc[0:tq]
        for r in range(1, B_REP):
            psum = psum + pc[r * tq:(r + 1) * tq]
        hi, mid, lo = _split3(psum)
        imp_t = _dot_nt(overlap_t, hi) + _dot_nt(overlap_t, mid) + _dot_nt(overlap_t, lo)
        score_t = jnp.where(forced_t, 1e30, jnp.where(valid_t, imp_t, -1.0))
        picked = _top_blocks(score_t, min(SEL_TOP, n_blocks), 0).T.astype(BF16)
        per_group.append((qb, slopes2, o_cmp, picked))

    m_sc[...] = jnp.full(m_sc.shape, NEG_INF, F32)
    acc_sc[...] = jnp.zeros(acc_sc.shape, F32)

    def tile(kj, masked):
        k0 = pl.multiple_of(kj * tk, tk)
        expand = (_iota((nbp, tk), 0) == kj * bpt + _iota((nbp, tk), 1) // SEL_BLOCK).astype(BF16)
        kpos = k0 + _iota((1, tk), 1)
        krel = (kpos - q0).astype(F32)
        for g, gs in enumerate(groups):
            qb, slopes2, _, picked = per_group[g]
            keep = _dot(picked, expand) > 0.5
            if masked:
                keep = keep & (qpos1 >= kpos)
            s = biased(_dot_nt(qb, skb_sc[pl.ds(k0, tk), gs]), keep, slopes2, krel)
            _flash_update(s, sva_sc[g, pl.ds(k0, tk), :], m_sc.at[g], acc_sc.at[g])

    _causal_tiles(q0, tq, tk, tile)

    for g, gs in enumerate(groups):
        qb, slopes2, o_cmp, _ = per_group[g]
        o_sel = acc_sc[g][:, 0:HEAD_DIM] / jnp.maximum(acc_sc[g][:, HEAD_DIM:LANES], 1e-30)

        sw = biased(_dot_nt(qb, wkb_sc[pl.ds(w0, lw), gs]), ok_win, slopes2, wrel)
        ew = jnp.exp2(sw - jnp.max(sw, axis=-1, keepdims=True))
        ow = _dot(ew.astype(BF16), wva_sc[g, pl.ds(w0, lw), :])
        o_win = ow[:, 0:HEAD_DIM] / ow[:, HEAD_DIM:LANES]

        for r in range(B_REP):
            h = g * B_REP + r
            rs = slice(r * tq, (r + 1) * tq)
            o_ref[:, h * HEAD_DIM:(h + 1) * HEAD_DIM] = (
                gates[:, 3 * h:3 * h + 1] * o_cmp[rs] + gates[:, 3 * h + 1:3 * h + 2] * o_sel[rs]
                + gates[:, 3 * h + 2:3 * h + 3] * o_win[rs])


def _nsa_prompt(q, gate, kc, vc, sk, sv, wk, wv, slopes):
    b, t, w = q.shape
    tq = _row_tile(t, (256, 128))
    tk = _row_tile(t, (512, 256, 128))
    lw = min(WINDOW + tq, t)
    kvw = B_KV * HEAD_DIM
    ncr = kc.shape[1]
    nbp = -(-(t // SEL_BLOCK) // LANES) * LANES
    full = lambda width, rows: pl.BlockSpec((None, rows, width), lambda i, j: (i, 0, 0))
    return pl.pallas_call(
        functools.partial(_nsa_prompt_kernel, tq=tq, tk=tk, t=t, lw=lw, nbp=nbp),
        grid=(b, t // tq),
        in_specs=[pl.BlockSpec(memory_space=pltpu.SMEM),
                  pl.BlockSpec((None, tq, w), lambda i, j: (i, j, 0)),
                  pl.BlockSpec((None, tq, LANES), lambda i, j: (i, j, 0)),
                  full(kvw, ncr), full(kvw, ncr), full(kvw, t), full(kvw, t), full(kvw, t), full(kvw, t)],
        out_specs=pl.BlockSpec((None, tq, w), lambda i, j: (i, j, 0)),
        out_shape=jax.ShapeDtypeStruct((b, t, w), F32),
        scratch_shapes=[pltpu.VMEM((B_KV, B_REP * tq, LANES), F32), pltpu.VMEM((B_KV, B_REP * tq, LANES), F32),
                        pltpu.VMEM((t, kvw), BF16), pltpu.VMEM((B_KV, t, LANES), BF16),
                        pltpu.VMEM((t, kvw), BF16), pltpu.VMEM((B_KV, t, LANES), BF16)],
        compiler_params=_cparams(("arbitrary", "arbitrary")), name="nsa_prompt")(
            slopes, q, gate, kc, vc, sk, sv, wk, wv)


NEW_PAD = 16


def _page_specs(n, rows, cols, layer, idx_fn):
    return [pl.BlockSpec((None, None, rows, cols),
                         lambda i, c, pt, j=j: (layer, pt[i, idx_fn(c, j)], 0, 0)) for j in range(n)]


def _flash_pages(qb, k_refs, v_refs, bias_fn, m_sc, l_sc, acc_sc, key_major_v=None):
    s = bias_fn(_dot(qb, jnp.concatenate([kr[...].astype(BF16) for kr in k_refs], axis=1)))
    m_old = m_sc[...]
    m_new = jnp.maximum(m_old, jnp.max(s, axis=-1, keepdims=True))
    alpha = jnp.exp(m_old - m_new)
    p = jnp.exp(s - m_new)
    l_sc[...] = alpha * l_sc[...] + jnp.sum(p, axis=-1, keepdims=True)
    if key_major_v is None:
        pv = _dot_nt(p.astype(BF16), jnp.concatenate([vr[...].astype(BF16) for vr in v_refs], axis=1))
    else:
        pv = _dot(p.astype(BF16), jnp.concatenate([key_major_v(vr) for vr in v_refs], axis=0))
    acc_sc[...] = alpha * acc_sc[...] + pv
    m_sc[...] = m_new


def _pad_new(x, b, ts):
    return jnp.pad(x.reshape(b, ts, x.shape[-1]), ((0, 0), (0, NEW_PAD - ts), (0, 0)))


def _diff_dec_kernel(pt_ref, qbd_ref, info_ref, kn_ref, vn_ref, lq1, lk1, lq2, lk2, subg_ref, *rest,
                     p, page, n_past, n_new, lam_init):
    k_refs, v_refs = rest[:p], rest[p:2 * p]
    o_ref, m_sc, l_sc, acc_sc = rest[2 * p:]
    c = pl.program_id(1)

    @pl.when(c == 0)
    def _():
        m_sc[...] = jnp.full(m_sc.shape, NEG_INF, F32)
        l_sc[...] = jnp.zeros(l_sc.shape, F32)
        acc_sc[...] = jnp.zeros(acc_sc.shape, F32)

    qb = (qbd_ref[...] * SCALE).astype(BF16)
    slope = info_ref[:, 0:1]
    qpos = info_ref[:, 1:2]
    kpos = (c * (p * page) + _iota((1, p * page), 1)).astype(F32)
    key_major = lambda vr: jnp.concatenate(
        [vr[pl.ds(h, page, stride=A_HEADS), :] for h in range(A_HEADS)], axis=1).astype(BF16)
    _flash_pages(qb, k_refs, v_refs, lambda s: s - slope * (qpos - kpos), m_sc, l_sc, acc_sc, key_major)

    @pl.when(c == pl.num_programs(1) - 1)
    def _():
        idx = _iota((1, NEW_PAD), 1)
        dist = qpos - (n_past + idx).astype(F32)
        s = _dot_nt(qb, kn_ref[...].astype(BF16))
        s = jnp.where((dist >= 0) & (idx < n_new), s - slope * dist, NEG_INF)
        _online_step(s, vn_ref[...].astype(BF16), m_sc, l_sc, acc_sc)
        full = acc_sc[...] / l_sc[...]
        rows = full.shape[0]
        rowh = (_iota((rows, 1), 0) // n_new) % A_HEADS
        o = jnp.zeros((rows, A_V), F32)
        for h in range(A_HEADS):
            o = o + jnp.where(rowh == h, full[:, h * A_V:(h + 1) * A_V], 0.0)
        half = rows // 2
        lam = _diff_lambda(lq1[...], lk1[...], lq2[...], lk2[...], lam_init)
        o = o[0:half] - lam * o[half:rows]
        o_ref[...] = _rms(o, subg_ref[...]) * (1.0 - lam_init)


def _diff_dec(q, kn, vn, cache_k, cache_v, layer, page_table, slopes, lams, subg, lam_init, pages_per_step):
    b, n_pages = page_table.shape
    ts = q.shape[0] // b
    kw, page = cache_k.shape[2], cache_k.shape[3]
    vw = A_HEADS * A_V
    n_past = n_pages * page
    p = pages_per_step
    rows = 2 * A_HEADS * ts
    q5 = q.reshape(b, ts, A_HEADS, 2, HEAD_DIM).transpose(0, 3, 2, 1, 4)
    eye_h = jnp.eye(A_HEADS, dtype=F32).reshape(1, 1, A_HEADS, 1, A_HEADS, 1, 1)
    eye_c = jnp.eye(2, dtype=F32).reshape(1, 2, 1, 1, 1, 2, 1)
    qbd = (q5[:, :, :, :, None, None, :] * eye_h * eye_c).reshape(b, rows, kw)
    slope_r = jnp.broadcast_to(slopes.reshape(1, A_HEADS, 1), (2, A_HEADS, ts)).reshape(rows)
    qpos_r = jnp.broadcast_to((n_past + jnp.arange(ts, dtype=F32)).reshape(1, 1, ts), (2, A_HEADS, ts)).reshape(rows)
    info = jnp.pad(jnp.stack([slope_r, qpos_r], axis=1), ((0, 0), (0, LANES - 2)))
    const2 = lambda i, c, pt: (0, 0)
    per_b = lambda r, w: pl.BlockSpec((None, r, w), lambda i, c, pt: (i, 0, 0))
    vec = pl.BlockSpec((1, HEAD_DIM), const2)
    in_specs = ([per_b(rows, kw), pl.BlockSpec((rows, LANES), const2), per_b(NEW_PAD, kw), per_b(NEW_PAD, vw),
                 vec, vec, vec, vec, pl.BlockSpec((1, A_V), const2)]
                + _page_specs(p, kw, page, layer, lambda c, j: c * p + j)
                + _page_specs(p, page * A_HEADS, A_V, layer, lambda c, j: c * p + j))
    out = pl.pallas_call(
        functools.partial(_diff_dec_kernel, p=p, page=page, n_past=n_past, n_new=ts, lam_init=lam_init),
        grid_spec=pltpu.PrefetchScalarGridSpec(
            num_scalar_prefetch=1, grid=(b, n_pages // p), in_specs=in_specs,
            out_specs=pl.BlockSpec((None, rows // 2, A_V), lambda i, c, pt: (i, 0, 0)),
            scratch_shapes=[pltpu.VMEM((rows, 1), F32), pltpu.VMEM((rows, 1), F32), pltpu.VMEM((rows, vw), F32)]),
        out_shape=jax.ShapeDtypeStruct((b, rows // 2, A_V), F32),
        compiler_params=_cparams(("arbitrary", "arbitrary")), name="diff_dec")(
            page_table, qbd, info, _pad_new(kn, b, ts), _pad_new(vn, b, ts),
            *[x.reshape(1, HEAD_DIM) for x in lams], subg.reshape(1, A_V),
            *([cache_k] * p), *([cache_v] * p))
    return out.reshape(b, A_HEADS, ts, A_V).transpose(0, 2, 1, 3).reshape(b * ts, A_HEADS * A_V)


def _fox_dec_kernel(pt_ref, qbd_ref, pn_ref, pnt_ref, kn_ref, vn_ref, *rest, p, page, n_new):
    k_refs, v_refs, lf_refs = rest[:p], rest[p:2 * p], rest[2 * p:3 * p]
    o_ref, m_sc, l_sc, acc_sc, run_sc = rest[3 * p:]
    c = pl.program_id(1)
    qb = (qbd_ref[...] * SCALE).astype(BF16)
    rows = qb.shape[0]
    pn_col = pn_ref[:, 0:1]

    @pl.when(c == 0)
    def _():
        m_sc[...] = jnp.full(m_sc.shape, NEG_INF, F32)
        l_sc[...] = jnp.zeros(l_sc.shape, F32)
        acc_sc[...] = jnp.zeros(acc_sc.shape, F32)
        run_sc[...] = jnp.zeros(run_sc.shape, F32)
        idx = _iota((1, NEW_PAD), 1)
        tok = _iota((rows, 1), 0) // C_HEADS
        bias = pn_col - jnp.concatenate([pnt_ref[...]] * n_new, axis=0)
        s = _dot_nt(qb, kn_ref[...].astype(BF16)) + bias
        s = jnp.where((idx <= tok) & (idx < n_new), s, NEG_INF)
        _online_step(s, vn_ref[...].astype(BF16), m_sc, l_sc, acc_sc)

    later = (_iota((page, page), 0) > _iota((page, page), 1)).astype(BF16)
    run = run_sc[...]
    sufs = [None] * p
    for j in reversed(range(p)):
        lf = lf_refs[j][...]
        sufs[j] = _dot_exact_rhs(lf, later) + run
        run = run + jnp.sum(lf, axis=-1, keepdims=True)
    run_sc[...] = run
    suf = jnp.concatenate(sufs, axis=1)
    bias = jnp.concatenate([suf] * n_new, axis=0) + pn_col
    _flash_pages(qb, k_refs, v_refs, lambda s: s + bias, m_sc, l_sc, acc_sc)

    @pl.when(c == pl.num_programs(1) - 1)
    def _():
        full = acc_sc[...] / l_sc[...]
        rowh = _iota((rows, 1), 0) % C_HEADS
        o2 = jnp.zeros((rows, LANES), F32)
        for hp in range(C_HEADS // 2):
            o2 = o2 + jnp.where(rowh // 2 == hp, full[:, hp * LANES:(hp + 1) * LANES], 0.0)
        o_ref[...] = jnp.where(rowh % 2 == 0, o2[:, 0:HEAD_DIM], o2[:, HEAD_DIM:LANES])


def _fox_dec(q, kn, vn, cum_new, cache_k, cache_v, cache_lft, layer, page_table, pages_per_step):
    b, n_pages = page_table.shape
    ts = q.shape[0] // b
    w, page = cache_k.shape[2], cache_k.shape[3]
    p = pages_per_step
    nsteps = n_pages // p
    rows = ts * C_HEADS
    eye = jnp.eye(C_HEADS, dtype=F32).reshape(1, 1, C_HEADS, C_HEADS, 1)
    qbd = (q.reshape(b, ts, C_HEADS, 1, HEAD_DIM) * eye).reshape(b, rows, w)
    pn = jnp.pad(cum_new.reshape(b, rows, 1), ((0, 0), (0, 0), (0, LANES - 1)))
    pnt = jnp.pad(cum_new.transpose(0, 2, 1), ((0, 0), (0, 0), (0, NEW_PAD - ts)))
    per_b = lambda r, ww: pl.BlockSpec((None, r, ww), lambda i, c, pt: (i, 0, 0))
    rev = lambda c, j: (nsteps - 1 - c) * p + j
    in_specs = ([per_b(rows, w), per_b(rows, LANES), per_b(C_HEADS, NEW_PAD), per_b(NEW_PAD, w), per_b(NEW_PAD, w)]
                + _page_specs(p, w, page, layer, rev) + _page_specs(p, w, page, layer, rev)
                + _page_specs(p, C_HEADS, page, layer, rev))
    out = pl.pallas_call(
        functools.partial(_fox_dec_kernel, p=p, page=page, n_new=ts),
        grid_spec=pltpu.PrefetchScalarGridSpec(
            num_scalar_prefetch=1, grid=(b, nsteps), in_specs=in_specs,
            out_specs=pl.BlockSpec((None, rows, HEAD_DIM), lambda i, c, pt: (i, 0, 0)),
            scratch_shapes=[pltpu.VMEM((rows, 1), F32), pltpu.VMEM((rows, 1), F32), pltpu.VMEM((rows, w), F32),
                            pltpu.VMEM((C_HEADS, 1), F32)]),
        out_shape=jax.ShapeDtypeStruct((b, rows, HEAD_DIM), F32),
        compiler_params=_cparams(("arbitrary", "arbitrary")), name="fox_dec")(
            page_table, qbd, pn, pnt, _pad_new(kn, b, ts), _pad_new(vn, b, ts),
            *([cache_k] * p), *([cache_v] * p), *([cache_lft] * p))
    return out.reshape(b * ts, C_HEADS * HEAD_DIM)


def _nsa_dec_kernel(pt_ref, qbd_ref, info_ref, kc_ref, vc_ref, skn_ref, svn_ref, wkn_ref, wvn_ref, wk_ref, wv_ref,
                    *rest, p, page, n_past, n_new):
    k_refs, v_refs = rest[:p], rest[p:2 * p]
    o_ref, m_sc, l_sc, acc_sc, ocmp_sc, pick_sc = rest[2 * p:]
    c = pl.program_id(1)
    q32 = qbd_ref[...] * SCALE
    qb = q32.astype(BF16)
    rows = q32.shape[0]
    gt = rows // B_REP
    slope = info_ref[:, 0:1]
    qpos = info_ref[:, 1:2]
    in_g0 = (_iota((rows, 1), 0) % gt) < n_new
    pick_cols = lambda full: jnp.where(in_g0, full[:, 0:HEAD_DIM], full[:, HEAD_DIM:2 * HEAD_DIM])
    n_blocks = n_past // SEL_BLOCK
    bps = p * page // SEL_BLOCK

    @pl.when(c == 0)
    def _():
        m_sc[...] = jnp.full(m_sc.shape, NEG_INF, F32)
        l_sc[...] = jnp.zeros(l_sc.shape, F32)
        acc_sc[...] = jnp.zeros(acc_sc.shape, F32)
        n_cmp_rows = kc_ref.shape[0]
        n_cmp = n_cmp_rows - (CMP_BLOCK // CMP_STRIDE - 1)
        cidx = _iota((1, n_cmp_rows), 1)
        dc = qpos - (cidx * CMP_STRIDE + (CMP_BLOCK - 1)).astype(F32)
        sc = _dot_nt_precise(q32, kc_ref[...]) - slope * dc
        pc = _masked_softmax_rows(sc, (dc >= 0) & (cidx < n_cmp))
        ocmp_sc[...] = pick_cols(_dot(pc.astype(BF16), vc_ref[...].astype(BF16)))
        psum = pc[0:gt]
        for r in range(1, B_REP):
            psum = psum + pc[r * gt:(r + 1) * gt]
        nbp = -(-n_blocks // LANES) * LANES
        hi, mid, lo = _split3(jnp.concatenate([psum, jnp.zeros((LANES - gt, n_cmp_rows), F32)], axis=0))
        blk_c = _iota((nbp, n_cmp_rows), 0) * SEL_BLOCK
        cmp_c = _iota((nbp, n_cmp_rows), 1) * CMP_STRIDE
        overlap_t = ((cmp_c < blk_c + SEL_BLOCK) & (cmp_c + CMP_BLOCK > blk_c)).astype(BF16)
        imp_t = _dot_nt(overlap_t, hi) + _dot_nt(overlap_t, mid) + _dot_nt(overlap_t, lo)
        blk_t = _iota((nbp, LANES), 0)
        qpos_t = n_past + _iota((1, LANES), 1) % n_new
        forced_t = (blk_t == 0) | (blk_t == qpos_t // SEL_BLOCK)
        valid_t = (blk_t * SEL_BLOCK <= qpos_t) & (blk_t < n_blocks)
        score_t = jnp.where(forced_t, 1e30, jnp.where(valid_t, imp_t, -1.0))
        picked = _top_blocks(score_t, min(SEL_TOP, n_blocks + 1) - 1, 0).T[0:gt]
        for cc in range(pick_sc.shape[0]):
            pick_sc[cc] = picked[:, cc * bps:(cc + 1) * bps]

    keys = p * page
    kpos = c * keys + _iota((1, keys), 1)
    expand = (_iota((bps, keys), 0) == _iota((bps, keys), 1) // SEL_BLOCK).astype(BF16)
    sel1 = _dot(pick_sc[c].astype(BF16), expand)
    sel = jnp.concatenate([sel1] * B_REP, axis=0)
    ds = qpos - kpos.astype(F32)
    _flash_pages(qb, k_refs, v_refs, lambda s: jnp.where((sel > 0.5) & (ds >= 0), s - slope * ds, NEG_INF),
                 m_sc, l_sc, acc_sc)

    @pl.when(c == pl.num_programs(1) - 1)
    def _():
        idx = _iota((1, NEW_PAD), 1)
        dn = qpos - (n_past + idx).astype(F32)
        new_ok = (dn >= 0) & (idx < n_new)
        s = jnp.where(new_ok, _dot_nt(qb, skn_ref[...].astype(BF16)) - slope * dn, NEG_INF)
        _online_step(s, svn_ref[...].astype(BF16), m_sc, l_sc, acc_sc)
        o_sel = pick_cols(acc_sc[...] / jnp.maximum(l_sc[...], 1e-30))
        keep = wk_ref.shape[1]
        wpos = n_past - keep + _iota((1, keep), 1)
        dw = qpos - wpos.astype(F32)
        ok1 = (dw >= 0) & (dw < WINDOW) & (wpos >= 0)
        ok2 = new_ok & (dn < WINDOW)
        s1 = jnp.where(ok1, _dot(qb, wk_ref[...].astype(BF16)) - slope * dw, NEG_INF)
        s2 = jnp.where(ok2, _dot_nt(qb, wkn_ref[...].astype(BF16)) - slope * dn, NEG_INF)
        mx = jnp.maximum(jnp.max(s1, axis=-1, keepdims=True), jnp.max(s2, axis=-1, keepdims=True))
        e1 = jnp.where(ok1, jnp.exp(s1 - mx), 0.0)
        e2 = jnp.where(ok2, jnp.exp(s2 - mx), 0.0)
        den = jnp.maximum(jnp.sum(e1, axis=-1, keepdims=True) + jnp.sum(e2, axis=-1, keepdims=True), 1e-30)
        o_win = pick_cols((_dot_nt(e1.astype(BF16), wv_ref[...].astype(BF16))
                           + _dot(e2.astype(BF16), wvn_ref[...].astype(BF16))) / den)
        gates = jax.nn.sigmoid(info_ref[:, 2:5])
        o_ref[...] = gates[:, 0:1] * ocmp_sc[...] + gates[:, 1:2] * o_sel + gates[:, 2:3] * o_win


def _nsa_dec(q, gate, kc, vc, skn, svn, wkn, wvn, win_k, win_v, cache_sk, cache_sv, layer, page_table, slopes,
             pages_per_step):
    b, n_pages = page_table.shape
    ts = q.shape[0] // b
    w, page = cache_sk.shape[2], cache_sk.shape[3]
    n_past = n_pages * page
    keep = win_k.shape[3]
    p = pages_per_step
    rows = B_HEADS * ts
    q5 = q.reshape(b, ts, B_KV, B_REP, HEAD_DIM).transpose(0, 3, 2, 1, 4)
    eye = jnp.eye(B_KV, dtype=F32).reshape(1, 1, B_KV, 1, B_KV, 1)
    qbd = (q5[:, :, :, :, None, :] * eye).reshape(b, rows, w)
    slope_r = jnp.broadcast_to(slopes.reshape(B_KV, B_REP).T.reshape(B_REP, B_KV, 1), (B_REP, B_KV, ts)).reshape(rows)
    qpos_r = jnp.broadcast_to((n_past + jnp.arange(ts, dtype=F32)).reshape(1, 1, ts), (B_REP, B_KV, ts)).reshape(rows)
    glog = gate[:, :3 * B_HEADS].reshape(b, ts, B_KV, B_REP, 3).transpose(0, 3, 2, 1, 4).reshape(b, rows, 3)
    info = jnp.concatenate([jnp.broadcast_to(jnp.stack([slope_r, qpos_r], axis=1)[None], (b, rows, 2)), glog], axis=2)
    info = jnp.pad(info, ((0, 0), (0, 0), (0, LANES - 5)))
    per_b = lambda r, ww: pl.BlockSpec((None, r, ww), lambda i, c, pt: (i, 0, 0))
    win = pl.BlockSpec((None, None, w, keep), lambda i, c, pt: (layer, i, 0, 0))
    ncr = kc.shape[1]
    in_specs = ([per_b(rows, w), per_b(rows, LANES), per_b(ncr, w), per_b(ncr, w)] + [per_b(NEW_PAD, w)] * 4
                + [win, win]
                + _page_specs(p, w, page, layer, lambda c, j: c * p + j)
                + _page_specs(p, w, page, layer, lambda c, j: c * p + j))
    out = pl.pallas_call(
        functools.partial(_nsa_dec_kernel, p=p, page=page, n_past=n_past, n_new=ts),
        grid_spec=pltpu.PrefetchScalarGridSpec(
            num_scalar_prefetch=1, grid=(b, n_pages // p), in_specs=in_specs,
            out_specs=pl.BlockSpec((None, rows, HEAD_DIM), lambda i, c, pt: (i, 0, 0)),
            scratch_shapes=[pltpu.VMEM((rows, 1), F32), pltpu.VMEM((rows, 1), F32), pltpu.VMEM((rows, w), F32),
                            pltpu.VMEM((rows, HEAD_DIM), F32), pltpu.VMEM((n_pages // p, rows // B_REP, p * page // SEL_BLOCK), F32)]),
        out_shape=jax.ShapeDtypeStruct((b, rows, HEAD_DIM), F32),
        compiler_params=_cparams(("arbitrary", "arbitrary")), name="nsa_dec")(
            page_table, qbd, info, kc, vc, _pad_new(skn, b, ts), _pad_new(svn, b, ts), _pad_new(wkn, b, ts),
            _pad_new(wvn, b, ts), win_k, win_v, *([cache_sk] * p), *([cache_sv] * p))
    return out.reshape(b, B_REP, B_KV, ts, HEAD_DIM).transpose(0, 3, 2, 1, 4).reshape(b * ts, B_HEADS * HEAD_DIM)


def _alibi_slopes(n):
    return jnp.exp2(-8.0 * jnp.arange(1, n + 1, dtype=F32) / n)


def _pages_per_step(n_pages, want):
    p = min(want, n_pages)
    while n_pages % p:
        p -= 1
    return p


def kernel(x_prompt, x_sample, cache_diff_k, cache_diff_v, cache_nsa_cmp_k, cache_nsa_cmp_v, cache_nsa_sel_k, cache_nsa_sel_v, state_nsa_win_k, state_nsa_win_v, cache_fox_k, cache_fox_v, cache_fox_logf, page_table, norm_ffn1_g, ffn1_wg, ffn1_wu, ffn1_wd, norm_mix_g, norm_ffn2_g, ffn2_wg, ffn2_wu, ffn2_wd, even_w_in, even_w_out, diff_lambda_q1, diff_lambda_k1, diff_lambda_q2, diff_lambda_k2, diff_subln_g, cmp_pos_k, cmp_w1_k, cmp_w2_k, cmp_pos_v, cmp_w1_v, cmp_w2_v, odd_w_in, odd_b_f, odd_w_out, final_norm_g):
    b, t, d = x_prompt.shape
    bs, ts, _ = x_sample.shape
    depth = norm_ffn1_g.shape[0]
    n_pages = page_table.shape[1]
    n_pool, page = cache_diff_k.shape[1], cache_diff_k.shape[2]
    keep = state_nsa_win_k.shape[2]
    n_past = n_pages * page
    assert n_past % SEL_BLOCK == 0 and page % SEL_BLOCK == 0 and page % CMP_STRIDE == 0
    assert 0 < ts <= min(SEL_BLOCK, CMP_STRIDE - 1, NEW_PAD) and (A_HEADS * ts) % 8 == 0
    assert t % LANES == 0 and n_past >= WINDOW
    kvw = B_KV * HEAD_DIM

    slopes_a = _alibi_slopes(A_HEADS)
    slopes_b = _alibi_slopes(B_HEADS)
    keys_last = lambda c: jnp.moveaxis(c, 2, -1).reshape(c.shape[0], c.shape[1], -1, c.shape[2])
    c_diff_k = keys_last(cache_diff_k)
    c_diff_v = cache_diff_v.reshape(cache_diff_v.shape[0], n_pool, page * A_HEADS, A_V)
    c_cmp_k, c_cmp_v = keys_last(cache_nsa_cmp_k), keys_last(cache_nsa_cmp_v)
    c_sel_k, c_sel_v = keys_last(cache_nsa_sel_k), keys_last(cache_nsa_sel_v)
    c_fox_k, c_fox_v = keys_last(cache_fox_k), keys_last(cache_fox_v)
    c_fox_lft = keys_last(cache_fox_logf)
    win_k, win_v = keys_last(state_nsa_win_k), keys_last(state_nsa_win_v)

    a_w = A_HEADS * 2 * HEAD_DIM
    even_cuts = [a_w, a_w, A_HEADS * A_V, B_HEADS * HEAD_DIM] + [kvw] * 6 + [3 * B_HEADS]
    odd_cuts = [C_HEADS * HEAD_DIM] * 3 + [C_HEADS]

    def split_w(w, cuts):
        out, s = [], 0
        for c in cuts:
            out.append(_pad_cols(w[:, s:s + c], -(-c // LANES) * LANES).astype(BF16))
            s += c
        return out

    xp = x_prompt.reshape(b * t, d)
    xs = x_sample.reshape(bs * ts, d)
    even_p = [[] for _ in range(8)]
    even_s = [[] for _ in range(8)]
    odd_p = [[] for _ in range(3)]
    odd_s = [[] for _ in range(3)]

    for li in range(depth):
        w1 = [ffn1_wg[li].astype(BF16), ffn1_wu[li].astype(BF16), ffn1_wd[li].astype(BF16)]
        xp = _ffn(xp, norm_ffn1_g[li], *w1)
        xs = _ffn(xs, norm_ffn1_g[li], *w1)
        if li % 2 == 0:
            e = li // 2
            lam_init = 0.8 - 0.6 * math.exp(-0.3 * li)
            lams = (diff_lambda_q1[e], diff_lambda_k1[e], diff_lambda_q2[e], diff_lambda_k2[e])
            ws = split_w(even_w_in[e], even_cuts)
            wo = even_w_out[e].astype(BF16)
            wo_a, wo_b = wo[:A_HEADS * A_V], wo[A_HEADS * A_V:]
            aq, ak, av, bq, ck, cv, sk, sv, wk, wv, bg = _norm_proj(xp, norm_mix_g[li], ws)
            r3 = lambda a: a.reshape(b, t, a.shape[-1])
            oa = _diff_prompt(r3(aq), r3(ak), r3(av), slopes_a, lams, diff_subln_g[e], lam_init)
            kc = _compress_prompt(r3(ck), cmp_pos_k[e], cmp_w1_k[e], cmp_w2_k[e])
            vc = _compress_prompt(r3(cv), cmp_pos_v[e], cmp_w1_v[e], cmp_w2_v[e])
            ob = _nsa_prompt(r3(bq), r3(bg), kc, vc, r3(sk), r3(sv), r3(wk), r3(wv), slopes_b)
            xp = _out_proj(xp, [oa.reshape(b * t, -1), ob.reshape(b * t, -1)], [wo_a, wo_b])

            def last_rows(a):
                a = jnp.pad(r3(a), ((0, 0), (max(0, keep - t), 0), (0, 0)))
                return a[:, a.shape[1] - keep:].reshape(b, keep, B_KV, HEAD_DIM)

            rows_p = (ak.reshape(b, t, A_HEADS, 2, HEAD_DIM), av.reshape(b, t, A_HEADS, A_V),
                      ck.reshape(b, t, B_KV, HEAD_DIM), cv.reshape(b, t, B_KV, HEAD_DIM),
                      sk.reshape(b, t, B_KV, HEAD_DIM), sv.reshape(b, t, B_KV, HEAD_DIM), last_rows(wk), last_rows(wv))
            for lst, r in zip(even_p, rows_p):
                lst.append(r)
            aq, ak, av, bq, ck, cv, sk, sv, wk, wv, bg = _norm_proj(xs, norm_mix_g[li], ws)
            oa = _diff_dec(aq, ak, av, c_diff_k, c_diff_v, e, page_table, slopes_a, lams, diff_subln_g[e], lam_init,
                           _pages_per_step(n_pages, 16))
            pps = _pages_per_step(n_pages, 16)
            kc, vc = _compress_paged(c_cmp_k, c_cmp_v, e, page_table, cmp_pos_k[e], cmp_w1_k[e], cmp_w2_k[e],
                                     cmp_pos_v[e], cmp_w1_v[e], cmp_w2_v[e], pps)
            ob = _nsa_dec(bq, bg, kc, vc, sk, sv, wk, wv, win_k, win_v, c_sel_k, c_sel_v, e, page_table, slopes_b, pps)
            xs = _out_proj(xs, [oa, ob], [wo_a, wo_b])
            s3 = lambda a: a.reshape(bs, ts, a.shape[-1])
            new_win = lambda buf, a: jnp.concatenate(
                [buf[e], s3(a).reshape(bs, ts, B_KV, HEAD_DIM)], axis=1)[:, ts:]
            rows_s = (ak.reshape(bs, ts, A_HEADS, 2, HEAD_DIM), av.reshape(bs, ts, A_HEADS, A_V),
                      ck.reshape(bs, ts, B_KV, HEAD_DIM), cv.reshape(bs, ts, B_KV, HEAD_DIM),
                      sk.reshape(bs, ts, B_KV, HEAD_DIM), sv.reshape(bs, ts, B_KV, HEAD_DIM),
                      new_win(state_nsa_win_k, wk), new_win(state_nsa_win_v, wv))
            for lst, r in zip(even_s, rows_s):
                lst.append(r)
        else:
            o = li // 2
            ws = split_w(odd_w_in[o], odd_cuts)
            wo = odd_w_out[o].astype(BF16)
            q, k, v, f = _norm_proj(xp, norm_mix_g[li], ws)
            r3 = lambda a: a.reshape(b, t, a.shape[-1])
            logf, cum = _logf_cum(r3(f), odd_b_f[o])
            om = _fox_prompt(r3(q), r3(k), r3(v), cum)
            xp = _out_proj(xp, [om.reshape(b * t, -1)], [wo])
            for lst, r in zip(odd_p, (k.reshape(b, t, C_HEADS, HEAD_DIM), v.reshape(b, t, C_HEADS, HEAD_DIM), logf)):
                lst.append(r)
            q, k, v, f = _norm_proj(xs, norm_mix_g[li], ws)
            logf, cum = _logf_cum(f.reshape(bs, ts, -1), odd_b_f[o])
            om = _fox_dec(q, k, v, cum, c_fox_k, c_fox_v, c_fox_lft, o, page_table, _pages_per_step(n_pages, 8))
            xs = _out_proj(xs, [om], [wo])
            for lst, r in zip(odd_s, (k.reshape(bs, ts, C_HEADS, HEAD_DIM), v.reshape(bs, ts, C_HEADS, HEAD_DIM), logf)):
                lst.append(r)
        w2 = [ffn2_wg[li].astype(BF16), ffn2_wu[li].astype(BF16), ffn2_wd[li].astype(BF16)]
        fin = final_norm_g if li == depth - 1 else None
        xp = _ffn(xp, norm_ffn2_g[li], *w2, final_g=fin)
        xs = _ffn(xs, norm_ffn2_g[li], *w2, final_g=fin)

    stk = lambda lst: jnp.stack(lst, 0)
    ep = [stk(l) for l in even_p]
    es = [stk(l) for l in even_s]
    op = [stk(l) for l in odd_p]
    os_ = [stk(l) for l in odd_s]
    out = [xp.reshape(b, t, d), xs.reshape(bs, ts, d)]
    for p_, s_ in zip(ep, es):
        out += [p_, s_]
    for p_, s_ in zip(op, os_):
        out += [p_, s_]
    return tuple(out)
```

```python
import functools
import math

import jax
import jax.numpy as jnp
from jax import lax
from jax.experimental import pallas as pl
from jax.experimental.pallas import tpu as pltpu

HEAD_DIM = 64
A_HEADS = 4
A_V = 2 * HEAD_DIM
B_HEADS = 8
B_KV = 2
B_REP = B_HEADS // B_KV
CMP_BLOCK = 32
CMP_STRIDE = 16
CMP_HIDDEN = 128
SEL_BLOCK = 64
SEL_TOP = 16
WINDOW = 512
C_HEADS = 16
RMS_EPS = 1e-6
NEG_INF = -1e30
SCALE = HEAD_DIM ** -0.5

LANES = 128
V7X_VMEM_LIMIT_BYTES = 56 * 1024 * 1024

F32 = jnp.float32
BF16 = jnp.bfloat16


def _cparams(sem):
    return pltpu.CompilerParams(dimension_semantics=sem, vmem_limit_bytes=V7X_VMEM_LIMIT_BYTES)


def _dot(a, b):
    return jnp.dot(a, b, preferred_element_type=F32)


def _dot_nt(a, b):
    return lax.dot_general(a, b, (((1,), (1,)), ((), ())), preferred_element_type=F32)


def _split3(x):
    hi = x.astype(BF16)
    r1 = x - hi.astype(F32)
    mid = r1.astype(BF16)
    lo = (r1 - mid.astype(F32)).astype(BF16)
    return hi, mid, lo


def _dot_exact_rhs(x, m_bf16):
    hi, mid, lo = _split3(x)
    return _dot(hi, m_bf16) + _dot(mid, m_bf16) + _dot(lo, m_bf16)


def _dot_nt_precise(a, b):
    ah = a.astype(BF16)
    al = (a - ah.astype(F32)).astype(BF16)
    bh = b.astype(BF16)
    bl = (b - bh.astype(F32)).astype(BF16)
    return _dot_nt(ah, bh) + _dot_nt(ah, bl) + _dot_nt(al, bh)


def _rms(x, g):
    ms = jnp.mean(x * x, axis=-1, keepdims=True)
    return x * lax.rsqrt(ms + RMS_EPS) * g


def _iota(shape, dim):
    return lax.broadcasted_iota(jnp.int32, shape, dim)


def _masked_softmax_rows(s, mask):
    s = jnp.where(mask, s, NEG_INF)
    e = jnp.where(mask, jnp.exp(s - jnp.max(s, axis=-1, keepdims=True)), 0.0)
    return e / jnp.maximum(jnp.sum(e, axis=-1, keepdims=True), 1e-30)


def _online_step(s, v_bf16, m_ref, l_ref, acc_ref):
    m_old = m_ref[...]
    m_new = jnp.maximum(m_old, jnp.max(s, axis=-1, keepdims=True))
    alpha = jnp.exp(m_old - m_new)
    p = jnp.exp(s - m_new)
    l_ref[...] = alpha * l_ref[...] + jnp.sum(p, axis=-1, keepdims=True)
    acc_ref[...] = alpha * acc_ref[...] + _dot(p.astype(BF16), v_bf16)
    m_ref[...] = m_new


def _row_tile(n, pref):
    for t in pref:
        if n % t == 0:
            return t
    return n


def _ffn_kernel(x_ref, g_ref, wg_ref, wu_ref, wd_ref, *rest, fc, nfc, has_final):
    if has_final:
        gf_ref, o_ref = rest
    else:
        (o_ref,) = rest
    x = x_ref[...]
    h = _rms(x, g_ref[...]).astype(BF16)
    acc = jnp.zeros(x.shape, F32)
    for c in range(nfc):
        sl = slice(c * fc, (c + 1) * fc)
        a = _dot(h, wg_ref[:, sl])
        u = _dot(h, wu_ref[:, sl])
        act = (a * jax.nn.sigmoid(a) * u).astype(BF16)
        acc = acc + _dot(act, wd_ref[sl, :])
    y = x + 0.5 * acc
    if has_final:
        y = _rms(y, gf_ref[...])
    o_ref[...] = y


def _ffn(x, g, wg, wu, wd, final_g=None):
    n, d = x.shape
    f = wg.shape[1]
    tm = _row_tile(n, (512, 256, 128))
    fc = _row_tile(f, (256, 128))
    const = lambda i: (0, 0)
    in_specs = [pl.BlockSpec((tm, d), lambda i: (i, 0)), pl.BlockSpec((1, d), const),
                pl.BlockSpec((d, f), const), pl.BlockSpec((d, f), const), pl.BlockSpec((f, d), const)]
    args = [x, g.reshape(1, d), wg, wu, wd]
    if final_g is not None:
        in_specs.append(pl.BlockSpec((1, d), const))
        args.append(final_g.reshape(1, d))
    return pl.pallas_call(
        functools.partial(_ffn_kernel, fc=fc, nfc=f // fc, has_final=final_g is not None),
        grid=(n // tm,), in_specs=in_specs, out_specs=pl.BlockSpec((tm, d), lambda i: (i, 0)),
        out_shape=jax.ShapeDtypeStruct((n, d), F32), compiler_params=_cparams(("arbitrary",)),
        name="ffn")(*args)


def _norm_proj_kernel(x_ref, g_ref, *refs, n_out):
    h = _rms(x_ref[...], g_ref[...]).astype(BF16)
    for w_ref, o_ref in zip(refs[:n_out], refs[n_out:]):
        o_ref[...] = _dot(h, w_ref[...])


def _norm_proj(x, g, ws):
    n, d = x.shape
    tm = _row_tile(n, (512, 256, 128))
    const = lambda i: (0, 0)
    in_specs = [pl.BlockSpec((tm, d), lambda i: (i, 0)), pl.BlockSpec((1, d), const)]
    in_specs += [pl.BlockSpec(w.shape, const) for w in ws]
    return pl.pallas_call(
        functools.partial(_norm_proj_kernel, n_out=len(ws)),
        grid=(n // tm,), in_specs=in_specs,
        out_specs=[pl.BlockSpec((tm, w.shape[1]), lambda i: (i, 0)) for w in ws],
        out_shape=[jax.ShapeDtypeStruct((n, w.shape[1]), F32) for w in ws],
        compiler_params=_cparams(("arbitrary",)), name="norm_proj")(x, g.reshape(1, d), *ws)


def _out_proj_kernel(x_ref, *refs, n_in):
    y = x_ref[...]
    for o_ref, w_ref in zip(refs[:n_in], refs[n_in:2 * n_in]):
        y = y + _dot(o_ref[...].astype(BF16), w_ref[...])
    refs[2 * n_in][...] = y


def _out_proj(x, os_, ws):
    n, d = x.shape
    tm = _row_tile(n, (512, 256, 128))
    const = lambda i: (0, 0)
    in_specs = [pl.BlockSpec((tm, d), lambda i: (i, 0))]
    in_specs += [pl.BlockSpec((tm, o.shape[1]), lambda i: (i, 0)) for o in os_]
    in_specs += [pl.BlockSpec(w.shape, const) for w in ws]
    return pl.pallas_call(
        functools.partial(_out_proj_kernel, n_in=len(os_)),
        grid=(n // tm,), in_specs=in_specs, out_specs=pl.BlockSpec((tm, d), lambda i: (i, 0)),
        out_shape=jax.ShapeDtypeStruct((n, d), F32), compiler_params=_cparams(("arbitrary",)),
        name="out_proj")(x, *os_, *ws)


def _pad_cols(w, width):
    return jnp.pad(w, ((0, 0), (0, width - w.shape[1])))


def _logf_cum_kernel(f_ref, b_ref, lf_ref, cum_ref, carry_ref, *, tc):
    @pl.when(pl.program_id(1) == 0)
    def _():
        carry_ref[...] = jnp.zeros(carry_ref.shape, F32)

    z = f_ref[...] + b_ref[...]
    lf = jnp.minimum(z, 0.0) - jnp.log1p(jnp.exp(-jnp.abs(z)))
    tri = (_iota((tc, tc), 0) >= _iota((tc, tc), 1)).astype(BF16)
    cum = _dot_exact_rhs_left(tri, lf) + carry_ref[...]
    lf_ref[...] = lf[:, :C_HEADS]
    cum_ref[...] = cum[:, :C_HEADS]
    carry_ref[...] = cum[tc - 1:tc, :]


def _dot_exact_rhs_left(m_bf16, x):
    hi, mid, lo = _split3(x)
    return _dot(m_bf16, hi) + _dot(m_bf16, mid) + _dot(m_bf16, lo)


def _logf_cum(f, b_f):
    b, t, _ = f.shape
    tc = _row_tile(t, (256, 128))
    bias = jnp.pad(b_f, (0, LANES - C_HEADS)).reshape(1, LANES)
    out = jax.ShapeDtypeStruct((b, t, C_HEADS), F32)
    return pl.pallas_call(
        functools.partial(_logf_cum_kernel, tc=tc),
        grid=(b, t // tc),
        in_specs=[pl.BlockSpec((None, tc, LANES), lambda i, j: (i, j, 0)), pl.BlockSpec((1, LANES), lambda i, j: (0, 0))],
        out_specs=[pl.BlockSpec((None, tc, C_HEADS), lambda i, j: (i, j, 0))] * 2,
        out_shape=[out, out], scratch_shapes=[pltpu.VMEM((1, LANES), F32)],
        compiler_params=_cparams(("arbitrary", "arbitrary")), name="logf_cum")(f, bias)


LOG2E = 1.4426950408889634


def _lanes(x, n):
    if n % LANES == 0:
        return x if n == LANES else jnp.tile(x, (1, n // LANES))
    return x[:, :n]


def _flash_update(s, v_aug, m_ref, acc_ref):
    m_prev = m_ref[...]
    m_next = jnp.maximum(m_prev, jnp.max(s, axis=-1, keepdims=True))
    p = jnp.exp2(s - _lanes(m_next, s.shape[1]))
    alpha = jnp.exp2(m_prev - m_next)
    acc_ref[...] = acc_ref[...] * _lanes(alpha, acc_ref.shape[1]) + _dot(p.astype(BF16), v_aug)
    m_ref[...] = m_next


def _causal_tiles(q0, tq, tk, tile_fn):
    n_full = q0 // tk
    n_k = (q0 + tq + tk - 1) // tk

    def run(masked):
        def body(kj, carry):
            tile_fn(kj, masked)
            return carry
        return body

    lax.fori_loop(0, n_full, run(False), 0)
    lax.fori_loop(n_full, n_k, run(True), 0)


def _stage_bf16(src_ref, dst_ref, tk):
    for j in range(src_ref.shape[0] // tk):
        rows = slice(j * tk, (j + 1) * tk)
        dst_ref[rows, :] = src_ref[rows, :].astype(BF16)


def _stage_values(src_ref, dst_ref, tk, cols, width):
    ones = jnp.ones((tk, width), BF16)
    for j in range(src_ref.shape[0] // tk):
        rows = slice(j * tk, (j + 1) * tk)
        dst_ref[rows, :] = jnp.concatenate([src_ref[rows, cols].astype(BF16), ones], axis=1)


def _fox_prompt_kernel(q_ref, k_ref, v_ref, fq_ref, fk_ref, o_ref, m_sc, acc_sc, kb_sc, va_sc, *, tq, tk, nsub):
    subs = [slice(c * HEAD_DIM, (c + 1) * HEAD_DIM) for c in range(nsub)]

    @pl.when(pl.program_id(2) == 0)
    def _():
        _stage_bf16(k_ref, kb_sc, tk)
        for c, cs in enumerate(subs):
            _stage_values(v_ref, va_sc.at[c], tk, cs, HEAD_DIM)

    q0 = pl.program_id(2) * tq
    qpos = q0 + _iota((tq, 1), 0)
    m_sc[...] = jnp.full(m_sc.shape, NEG_INF, F32)
    acc_sc[...] = jnp.zeros(acc_sc.shape, F32)
    qs = [(q_ref[:, cs] * (SCALE * LOG2E)).astype(BF16) for cs in subs]
    f0 = [fq_ref[0:1, c:c + 1] for c in range(nsub)]

    def tile(kj, masked):
        k0 = pl.multiple_of(kj * tk, tk)
        if masked:
            ok = qpos >= k0 + _iota((1, tk), 1)
        for c, cs in enumerate(subs):
            s = _dot_nt(qs[c], kb_sc[pl.ds(k0, tk), cs]) + (f0[c] - fk_ref[kj, c:c + 1, :]) * LOG2E
            if masked:
                s = jnp.where(ok, s, NEG_INF)
            _flash_update(s, va_sc[c, pl.ds(k0, tk), :], m_sc.at[c], acc_sc.at[c])

    _causal_tiles(q0, tq, tk, tile)
    for c, cs in enumerate(subs):
        o_ref[:, cs] = acc_sc[c][:, 0:HEAD_DIM] / acc_sc[c][:, HEAD_DIM:LANES]


def _fox_prompt(q, k, v, cum):
    b, t, w = q.shape
    tq = _row_tile(t, (512, 256, 128))
    tk = tq
    nsub = 4
    ngrp = C_HEADS // nsub
    wb = nsub * HEAD_DIM
    fq = cum.reshape(b, t, ngrp, nsub).transpose(0, 2, 1, 3)
    fk = cum.reshape(b, t // tk, tk, ngrp, nsub).transpose(0, 3, 1, 4, 2)
    return pl.pallas_call(
        functools.partial(_fox_prompt_kernel, tq=tq, tk=tk, nsub=nsub),
        grid=(b, ngrp, t // tq),
        in_specs=[pl.BlockSpec((None, tq, wb), lambda i, h, j: (i, j, h)),
                  pl.BlockSpec((None, t, wb), lambda i, h, j: (i, 0, h)),
                  pl.BlockSpec((None, t, wb), lambda i, h, j: (i, 0, h)),
                  pl.BlockSpec((None, None, tq, nsub), lambda i, h, j: (i, h, j, 0)),
                  pl.BlockSpec((None, None, t // tk, nsub, tk), lambda i, h, j: (i, h, 0, 0, 0))],
        out_specs=pl.BlockSpec((None, tq, wb), lambda i, h, j: (i, j, h)),
        out_shape=jax.ShapeDtypeStruct((b, t, w), F32),
        scratch_shapes=[pltpu.VMEM((nsub, tq, LANES), F32), pltpu.VMEM((nsub, tq, LANES), F32),
                        pltpu.VMEM((t, wb), BF16), pltpu.VMEM((nsub, t, LANES), BF16)],
        compiler_params=_cparams(("arbitrary",) * 3), name="fox_prompt")(q, k, v, fq, fk)


def _diff_lambda(lq1, lk1, lq2, lk2, lam_init):
    return (jnp.exp(jnp.sum(lq1 * lk1, axis=-1, keepdims=True))
            - jnp.exp(jnp.sum(lq2 * lk2, axis=-1, keepdims=True)) + lam_init)


def _diff_prompt_kernel(slopes_ref, q_ref, k_ref, v_ref, lq1, lk1, lq2, lk2, subg_ref, o_ref,
                        m_sc, acc_sc, kb_sc, va_sc, *, tq, tk, lam_init, nh):
    @pl.when(pl.program_id(2) == 0)
    def _():
        _stage_bf16(k_ref, kb_sc, tk)
        for h in range(nh):
            _stage_values(v_ref, va_sc.at[h], tk, slice(h * A_V, (h + 1) * A_V), A_V)

    q0 = pl.program_id(2) * tq
    qpos = q0 + _iota((tq, 1), 0)
    m_sc[...] = jnp.full(m_sc.shape, NEG_INF, F32)
    acc_sc[...] = jnp.zeros(acc_sc.shape, F32)
    subs = [slice(c * HEAD_DIM, (c + 1) * HEAD_DIM) for c in range(2 * nh)]
    qs = [(q_ref[:, cs] * (SCALE * LOG2E)).astype(BF16) for cs in subs]
    slope2 = [slopes_ref[pl.program_id(1) * nh + h] * LOG2E for h in range(nh)]

    def tile(kj, masked):
        k0 = pl.multiple_of(kj * tk, tk)
        kpos = k0 + _iota((1, tk), 1)
        krel = (kpos - q0).astype(F32)
        if masked:
            ok = qpos >= kpos
        for c, cs in enumerate(subs):
            s = _dot_nt(qs[c], kb_sc[pl.ds(k0, tk), cs]) + slope2[c // 2] * krel
            if masked:
                s = jnp.where(ok, s, NEG_INF)
            _flash_update(s, va_sc[c // 2, pl.ds(k0, tk), :], m_sc.at[c], acc_sc.at[c])

    _causal_tiles(q0, tq, tk, tile)
    lam = _diff_lambda(lq1[...], lk1[...], lq2[...], lk2[...], lam_init)
    for h in range(nh):
        a0, a1 = acc_sc[2 * h], acc_sc[2 * h + 1]
        o = a0[:, 0:A_V] / a0[:, A_V:2 * A_V] - lam * (a1[:, 0:A_V] / a1[:, A_V:2 * A_V])
        o_ref[:, h * A_V:(h + 1) * A_V] = _rms(o, subg_ref[...]) * (1.0 - lam_init)


def _diff_prompt(q, k, v, slopes, lams, subg, lam_init):
    b, t, w = q.shape
    tq = _row_tile(t, (512, 256, 128))
    tk = tq
    nh = 2
    wb = nh * A_V
    vec = pl.BlockSpec((1, HEAD_DIM), lambda i, h, j: (0, 0))
    return pl.pallas_call(
        functools.partial(_diff_prompt_kernel, tq=tq, tk=tk, lam_init=lam_init, nh=nh),
        grid=(b, A_HEADS // nh, t // tq),
        in_specs=[pl.BlockSpec(memory_space=pltpu.SMEM),
                  pl.BlockSpec((None, tq, wb), lambda i, h, j: (i, j, h)),
                  pl.BlockSpec((None, t, wb), lambda i, h, j: (i, 0, h)),
                  pl.BlockSpec((None, t, wb), lambda i, h, j: (i, 0, h)),
                  vec, vec, vec, vec, pl.BlockSpec((1, A_V), lambda i, h, j: (0, 0))],
        out_specs=pl.BlockSpec((None, tq, wb), lambda i, h, j: (i, j, h)),
        out_shape=jax.ShapeDtypeStruct((b, t, w), F32),
        scratch_shapes=[pltpu.VMEM((2 * nh, tq, LANES), F32), pltpu.VMEM((2 * nh, tq, 2 * A_V), F32),
                        pltpu.VMEM((t, wb), BF16), pltpu.VMEM((nh, t, 2 * A_V), BF16)],
        compiler_params=_cparams(("arbitrary",) * 3), name="diff_prompt")(
            slopes, q, k, v, *[x.reshape(1, HEAD_DIM) for x in lams], subg.reshape(1, A_V))


def _cmp_weights(pos, w1):
    ratio = CMP_BLOCK // CMP_STRIDE
    w1r = w1.reshape(ratio, CMP_STRIDE, 1, HEAD_DIM, 1, CMP_HIDDEN)
    eye = jnp.eye(B_KV, dtype=w1.dtype).reshape(1, 1, B_KV, 1, B_KV, 1)
    big = (w1r * eye).reshape(ratio, CMP_STRIDE * B_KV * HEAD_DIM, B_KV * CMP_HIDDEN)
    w1big = jnp.concatenate([big[r] for r in range(ratio)], axis=1)
    posr = jnp.broadcast_to(pos.reshape(ratio, CMP_STRIDE, 1, HEAD_DIM), (ratio, CMP_STRIDE, B_KV, HEAD_DIM))
    posrows = jnp.pad(posr.reshape(ratio, CMP_STRIDE * B_KV * HEAD_DIM), ((0, 8 - ratio), (0, 0)))
    return w1big.astype(BF16), posrows


def _cmp_a_kernel(x_ref, w_ref, o_ref):
    o_ref[...] = _dot(x_ref[...].astype(BF16), w_ref[...])


def _cmp_b_kernel(a_ref, posrows_ref, w1_ref, w2_ref, o_ref):
    n = a_ref.shape[1]
    hw = B_KV * CMP_HIDDEN
    pc = _dot(posrows_ref[...].astype(BF16), w1_ref[...])
    posc = pc[0:1, 0:hw] + pc[1:2, hw:2 * hw]
    w2 = w2_ref[...]
    for i in range(a_ref.shape[0]):
        a = a_ref[i]
        nxt = pltpu.roll(a[:, hw:2 * hw], n - 1, 0)
        pre = a[:, 0:hw] + nxt + posc
        act = (pre * jax.nn.sigmoid(pre)).astype(BF16)
        for g in range(B_KV):
            o_ref[i, :, g * HEAD_DIM:(g + 1) * HEAD_DIM] = _dot(act[:, g * CMP_HIDDEN:(g + 1) * CMP_HIDDEN], w2)


def _cmp_b(a, posrows, w1big, w2):
    b, n, wa = a.shape
    sb = _row_tile(b, (8, 4, 2))
    const = lambda i: (0, 0)
    return pl.pallas_call(
        _cmp_b_kernel, grid=(b // sb,),
        in_specs=[pl.BlockSpec((sb, n, wa), lambda i: (i, 0, 0)), pl.BlockSpec(posrows.shape, const),
                  pl.BlockSpec(w1big.shape, const), pl.BlockSpec(w2.shape, const)],
        out_specs=pl.BlockSpec((sb, n, B_KV * HEAD_DIM), lambda i: (i, 0, 0)),
        out_shape=jax.ShapeDtypeStruct((b, n, B_KV * HEAD_DIM), F32),
        compiler_params=_cparams(("arbitrary",)), name="cmp_b")(a, posrows, w1big, w2)


def _compress_prompt(rows, pos, w1, w2):
    b, t, w = rows.shape
    n = t // CMP_STRIDE
    w1big, posrows = _cmp_weights(pos, w1)
    view = rows.reshape(b * n, CMP_STRIDE * w)
    tm = _row_tile(b * n, (256, 128))
    a = pl.pallas_call(
        _cmp_a_kernel, grid=(b * n // tm,),
        in_specs=[pl.BlockSpec((tm, CMP_STRIDE * w), lambda i: (i, 0)), pl.BlockSpec(w1big.shape, lambda i: (0, 0))],
        out_specs=pl.BlockSpec((tm, w1big.shape[1]), lambda i: (i, 0)),
        out_shape=jax.ShapeDtypeStruct((b * n, w1big.shape[1]), F32),
        compiler_params=_cparams(("arbitrary",)), name="cmp_a")(view, w1big)
    return _cmp_b(a.reshape(b, n, -1), posrows, w1big, w2.astype(BF16))


def _cmp_a_paged_kernel(pt_ref, *refs, n_pages):
    x_sc = refs[-1]
    for which in range(2):
        page_refs = refs[which * n_pages:(which + 1) * n_pages]
        w_ref, o_ref = refs[2 * n_pages + which], refs[2 * n_pages + 2 + which]
        lanes, page = page_refs[0].shape
        for i in range(n_pages):
            x_sc[i * page:(i + 1) * page, :] = page_refs[i][...].T
        n_chunks = n_pages * page // CMP_STRIDE
        acc = None
        for j in range(0, CMP_STRIDE, 2):
            xs = jnp.concatenate([x_sc[pl.ds(j, n_chunks, stride=CMP_STRIDE), :],
                                  x_sc[pl.ds(j + 1, n_chunks, stride=CMP_STRIDE), :]], axis=1).astype(BF16)
            term = _dot(xs, w_ref[j * lanes:(j + 2) * lanes, :])
            acc = term if acc is None else acc + term
        o_ref[...] = acc


def _compress_paged(cache_k, cache_v, layer, page_table, pos_k, w1_k, w2_k, pos_v, w1_v, w2_v, pages_per_step):
    _, n_pool, w, page = cache_k.shape
    b, n_pages = page_table.shape
    cpp = page // CMP_STRIDE
    w1big_k, posrows_k = _cmp_weights(pos_k, w1_k)
    w1big_v, posrows_v = _cmp_weights(pos_v, w1_v)
    p = pages_per_step
    pages = _page_specs(p, w, page, layer, lambda c, j: c * p + j)
    wspec = pl.BlockSpec(w1big_k.shape, lambda i, c, pt: (0, 0))
    ospec = pl.BlockSpec((None, p * cpp, w1big_k.shape[1]), lambda i, c, pt: (i, c, 0))
    oshape = jax.ShapeDtypeStruct((b, n_pages * cpp, w1big_k.shape[1]), F32)
    a_k, a_v = pl.pallas_call(
        functools.partial(_cmp_a_paged_kernel, n_pages=p),
        grid_spec=pltpu.PrefetchScalarGridSpec(
            num_scalar_prefetch=1, grid=(b, n_pages // p),
            in_specs=pages + pages + [wspec, wspec], out_specs=[ospec, ospec],
            scratch_shapes=[pltpu.VMEM((p * page, w), F32)]),
        out_shape=[oshape, oshape],
        compiler_params=_cparams(("arbitrary", "arbitrary")), name="cmp_a_paged")(
            page_table, *([cache_k] * p), *([cache_v] * p), w1big_k, w1big_v)
    return (_cmp_b(a_k, posrows_k, w1big_k, w2_k.astype(BF16)), _cmp_b(a_v, posrows_v, w1big_v, w2_v.astype(BF16)))


def _top_blocks(score, n_pick, axis):
    blk = _iota(score.shape, axis)
    n_blocks = score.shape[axis]

    def body(_, carry):
        cur, picked = carry
        best = jnp.max(cur, axis=axis, keepdims=True)
        idx = jnp.min(jnp.where(cur == best, blk, n_blocks), axis=axis, keepdims=True)
        hit = blk == idx
        return jnp.where(hit, -2.0, cur), jnp.where(hit, 1.0, picked)

    _, picked = lax.fori_loop(0, n_pick, body, (score, jnp.zeros(score.shape, F32)))
    return picked


def _nsa_prompt_kernel(slopes_ref, q_ref, gate_ref, kc_ref, vc_ref, sk_ref, sv_ref, wk_ref, wv_ref, o_ref,
                       m_sc, acc_sc, skb_sc, sva_sc, wkb_sc, wva_sc, *, tq, tk, t, lw, nbp):
    groups = [slice(g * HEAD_DIM, (g + 1) * HEAD_DIM) for g in range(B_KV)]

    @pl.when(pl.program_id(1) == 0)
    def _():
        _stage_bf16(sk_ref, skb_sc, tk)
        _stage_bf16(wk_ref, wkb_sc, tk)
        for g, gs in enumerate(groups):
            _stage_values(sv_ref, sva_sc.at[g], tk, gs, HEAD_DIM)
            _stage_values(wv_ref, wva_sc.at[g], tk, gs, HEAD_DIM)

    q0 = pl.multiple_of(pl.program_id(1) * tq, tq)
    qpos1 = q0 + _iota((tq, 1), 0)
    n_cmp_rows = kc_ref.shape[0]
    n_cmp = n_cmp_rows - (CMP_BLOCK // CMP_STRIDE - 1)
    n_blocks = t // SEL_BLOCK
    bpt = tk // SEL_BLOCK
    gates = jax.nn.sigmoid(gate_ref[...])
    cidx = _iota((1, n_cmp_rows), 1)
    c_end = cidx * CMP_STRIDE + (CMP_BLOCK - 1)
    ok_cmp = (c_end <= qpos1) & (cidx < n_cmp)
    crel = (c_end - q0).astype(F32)
    w0 = pl.multiple_of(jnp.maximum(q0 + tq - lw, 0), tq)
    wpos = w0 + _iota((1, lw), 1)
    dw = qpos1 - wpos
    ok_win = (dw >= 0) & (dw < WINDOW)
    wrel = (wpos - q0).astype(F32)
    blk_t = _iota((nbp, tq), 0)
    qpos_t = q0 + _iota((1, tq), 1)
    valid_t = (blk_t * SEL_BLOCK <= qpos_t) & (blk_t < n_blocks)
    forced_t = (blk_t == qpos_t // SEL_BLOCK) | (blk_t == 0)
    overlap_t = (_iota((nbp, n_cmp_rows), 1) * CMP_STRIDE < _iota((nbp, n_cmp_rows), 0) * SEL_BLOCK + SEL_BLOCK) & (
        _iota((nbp, n_cmp_rows), 1) * CMP_STRIDE + CMP_BLOCK > _iota((nbp, n_cmp_rows), 0) * SEL_BLOCK)
    overlap_t = overlap_t.astype(BF16)

    def biased(raw, ok, slopes2, rel):
        return jnp.concatenate(
            [jnp.where(ok, raw[r * tq:(r + 1) * tq] + slopes2[r] * rel, NEG_INF) for r in range(B_REP)], axis=0)

    per_group = []
    for g, gs in enumerate(groups):
        q32 = jnp.concatenate(
            [q_ref[:, (g * B_REP + r) * HEAD_DIM:(g * B_REP + r + 1) * HEAD_DIM] for r in range(B_REP)],
            axis=0) * (SCALE * LOG2E)
        qb = q32.astype(BF16)
        slopes2 = [slopes_ref[g * B_REP + r] * LOG2E for r in range(B_REP)]

        sc = biased(_dot_nt_precise(q32, kc_ref[:, gs]), ok_cmp, slopes2, crel)
        mx = jnp.max(sc, axis=-1, keepdims=True)
        e = jnp.exp2(sc - mx)
        den = jnp.sum(e, axis=-1, keepdims=True)
        pc = e * jnp.where(mx > 0.5 * NEG_INF, 1.0 / den, 0.0)
        o_cmp = _dot(pc.astype(BF16), vc_ref[:, gs].astype(BF16))

        psum = pc[0:tq]
        for r in range(1, B_REP):
            psum = psum + pc[r * tq:(r + 1) * tq]
        hi, mid, lo = _split3(psum)
        imp_t = _dot_nt(overlap_t, hi) + _dot_nt(overlap_t, mid) + _dot_nt(overlap_t, lo)
        score_t = jnp.where(forced_t, 1e30, jnp.where(valid_t, imp_t, -1.0))
        picked = _top_blocks(score_t, min(SEL_TOP, n_blocks), 0).T.astype(BF16)
        per_group.append((qb, slopes2, o_cmp, picked))

    m_sc[...] = jnp.full(m_sc.shape, NEG_INF, F32)
    acc_sc[...] = jnp.zeros(acc_sc.shape, F32)

    def tile(kj, masked):
        k0 = pl.multiple_of(kj * tk, tk)
        expand = (_iota((nbp, tk), 0) == kj * bpt + _iota((nbp, tk), 1) // SEL_BLOCK).astype(BF16)
        kpos = k0 + _iota((1, tk), 1)
        krel = (kpos - q0).astype(F32)
        for g, gs in enumerate(groups):
            qb, slopes2, _, picked = per_group[g]
            keep = _dot(picked, expand) > 0.5
            if masked:
                keep = keep & (qpos1 >= kpos)
            s = biased(_dot_nt(qb, skb_sc[pl.ds(k0, tk), gs]), keep, slopes2, krel)
            _flash_update(s, sva_sc[g, pl.ds(k0, tk), :], m_sc.at[g], acc_sc.at[g])

    _causal_tiles(q0, tq, tk, tile)

    for g, gs in enumerate(groups):
        qb, slopes2, o_cmp, _ = per_group[g]
        o_sel = acc_sc[g][:, 0:HEAD_DIM] / jnp.maximum(acc_sc[g][:, HEAD_DIM:LANES], 1e-30)

        sw = biased(_dot_nt(qb, wkb_sc[pl.ds(w0, lw), gs]), ok_win, slopes2, wrel)
        ew = jnp.exp2(sw - jnp.max(sw, axis=-1, keepdims=True))
        ow = _dot(ew.astype(BF16), wva_sc[g, pl.ds(w0, lw), :])
        o_win = ow[:, 0:HEAD_DIM] / ow[:, HEAD_DIM:LANES]

        for r in range(B_REP):
            h = g * B_REP + r
            rs = slice(r * tq, (r + 1) * tq)
            o_ref[:, h * HEAD_DIM:(h + 1) * HEAD_DIM] = (
                gates[:, 3 * h:3 * h + 1] * o_cmp[rs] + gates[:, 3 * h + 1:3 * h + 2] * o_sel[rs]
                + gates[:, 3 * h + 2:3 * h + 3] * o_win[rs])


def _nsa_prompt(q, gate, kc, vc, sk, sv, wk, wv, slopes):
    b, t, w = q.shape
    tq = _row_tile(t, (256, 128))
    tk = _row_tile(t, (512, 256, 128))
    lw = min(WINDOW + tq, t)
    kvw = B_KV * HEAD_DIM
    ncr = kc.shape[1]
    nbp = -(-(t // SEL_BLOCK) // LANES) * LANES
    full = lambda width, rows: pl.BlockSpec((None, rows, width), lambda i, j: (i, 0, 0))
    return pl.pallas_call(
        functools.partial(_nsa_prompt_kernel, tq=tq, tk=tk, t=t, lw=lw, nbp=nbp),
        grid=(b, t // tq),
        in_specs=[pl.BlockSpec(memory_space=pltpu.SMEM),
                  pl.BlockSpec((None, tq, w), lambda i, j: (i, j, 0)),
                  pl.BlockSpec((None, tq, LANES), lambda i, j: (i, j, 0)),
                  full(kvw, ncr), full(kvw, ncr), full(kvw, t), full(kvw, t), full(kvw, t), full(kvw, t)],
        out_specs=pl.BlockSpec((None, tq, w), lambda i, j: (i, j, 0)),
        out_shape=jax.ShapeDtypeStruct((b, t, w), F32),
        scratch_shapes=[pltpu.VMEM((B_KV, B_REP * tq, LANES), F32), pltpu.VMEM((B_KV, B_REP * tq, LANES), F32),
                        pltpu.VMEM((t, kvw), BF16), pltpu.VMEM((B_KV, t, LANES), BF16),
                        pltpu.VMEM((t, kvw), BF16), pltpu.VMEM((B_KV, t, LANES), BF16)],
        compiler_params=_cparams(("arbitrary", "arbitrary")), name="nsa_prompt")(
            slopes, q, gate, kc, vc, sk, sv, wk, wv)


NEW_PAD = 16


def _page_specs(n, rows, cols, layer, idx_fn):
    return [pl.BlockSpec((None, None, rows, cols),
                         lambda i, c, pt, j=j: (layer, pt[i, idx_fn(c, j)], 0, 0)) for j in range(n)]


def _flash_pages(qb, k_refs, v_refs, bias_fn, m_sc, l_sc, acc_sc, key_major_v=None):
    s = bias_fn(_dot(qb, jnp.concatenate([kr[...].astype(BF16) for kr in k_refs], axis=1)))
    m_old = m_sc[...]
    m_new = jnp.maximum(m_old, jnp.max(s, axis=-1, keepdims=True))
    alpha = jnp.exp(m_old - m_new)
    p = jnp.exp(s - m_new)
    l_sc[...] = alpha * l_sc[...] + jnp.sum(p, axis=-1, keepdims=True)
    if key_major_v is None:
        pv = _dot_nt(p.astype(BF16), jnp.concatenate([vr[...].astype(BF16) for vr in v_refs], axis=1))
    else:
        pv = _dot(p.astype(BF16), jnp.concatenate([key_major_v(vr) for vr in v_refs], axis=0))
    acc_sc[...] = alpha * acc_sc[...] + pv
    m_sc[...] = m_new


def _pad_new(x, b, ts):
    return jnp.pad(x.reshape(b, ts, x.shape[-1]), ((0, 0), (0, NEW_PAD - ts), (0, 0)))


def _diff_dec_kernel(pt_ref, qbd_ref, info_ref, kn_ref, vn_ref, lq1, lk1, lq2, lk2, subg_ref, *rest,
                     p, page, n_past, n_new, lam_init):
    k_refs, v_refs = rest[:p], rest[p:2 * p]
    o_ref, m_sc, l_sc, acc_sc = rest[2 * p:]
    c = pl.program_id(1)

    @pl.when(c == 0)
    def _():
        m_sc[...] = jnp.full(m_sc.shape, NEG_INF, F32)
        l_sc[...] = jnp.zeros(l_sc.shape, F32)
        acc_sc[...] = jnp.zeros(acc_sc.shape, F32)

    qb = (qbd_ref[...] * SCALE).astype(BF16)
    slope = info_ref[:, 0:1]
    qpos = info_ref[:, 1:2]
    kpos = (c * (p * page) + _iota((1, p * page), 1)).astype(F32)
    key_major = lambda vr: jnp.concatenate(
        [vr[pl.ds(h, page, stride=A_HEADS), :] for h in range(A_HEADS)], axis=1).astype(BF16)
    _flash_pages(qb, k_refs, v_refs, lambda s: s - slope * (qpos - kpos), m_sc, l_sc, acc_sc, key_major)

    @pl.when(c == pl.num_programs(1) - 1)
    def _():
        idx = _iota((1, NEW_PAD), 1)
        dist = qpos - (n_past + idx).astype(F32)
        s = _dot_nt(qb, kn_ref[...].astype(BF16))
        s = jnp.where((dist >= 0) & (idx < n_new), s - slope * dist, NEG_INF)
        _online_step(s, vn_ref[...].astype(BF16), m_sc, l_sc, acc_sc)
        full = acc_sc[...] / l_sc[...]
        rows = full.shape[0]
        rowh = (_iota((rows, 1), 0) // n_new) % A_HEADS
        o = jnp.zeros((rows, A_V), F32)
        for h in range(A_HEADS):
            o = o + jnp.where(rowh == h, full[:, h * A_V:(h + 1) * A_V], 0.0)
        half = rows // 2
        lam = _diff_lambda(lq1[...], lk1[...], lq2[...], lk2[...], lam_init)
        o = o[0:half] - lam * o[half:rows]
        o_ref[...] = _rms(o, subg_ref[...]) * (1.0 - lam_init)


def _diff_dec(q, kn, vn, cache_k, cache_v, layer, page_table, slopes, lams, subg, lam_init, pages_per_step):
    b, n_pages = page_table.shape
    ts = q.shape[0] // b
    kw, page = cache_k.shape[2], cache_k.shape[3]
    vw = A_HEADS * A_V
    n_past = n_pages * page
    p = pages_per_step
    rows = 2 * A_HEADS * ts
    q5 = q.reshape(b, ts, A_HEADS, 2, HEAD_DIM).transpose(0, 3, 2, 1, 4)
    eye_h = jnp.eye(A_HEADS, dtype=F32).reshape(1, 1, A_HEADS, 1, A_HEADS, 1, 1)
    eye_c = jnp.eye(2, dtype=F32).reshape(1, 2, 1, 1, 1, 2, 1)
    qbd = (q5[:, :, :, :, None, None, :] * eye_h * eye_c).reshape(b, rows, kw)
    slope_r = jnp.broadcast_to(slopes.reshape(1, A_HEADS, 1), (2, A_HEADS, ts)).reshape(rows)
    qpos_r = jnp.broadcast_to((n_past + jnp.arange(ts, dtype=F32)).reshape(1, 1, ts), (2, A_HEADS, ts)).reshape(rows)
    info = jnp.pad(jnp.stack([slope_r, qpos_r], axis=1), ((0, 0), (0, LANES - 2)))
    const2 = lambda i, c, pt: (0, 0)
    per_b = lambda r, w: pl.BlockSpec((None, r, w), lambda i, c, pt: (i, 0, 0))
    vec = pl.BlockSpec((1, HEAD_DIM), const2)
    in_specs = ([per_b(rows, kw), pl.BlockSpec((rows, LANES), const2), per_b(NEW_PAD, kw), per_b(NEW_PAD, vw),
                 vec, vec, vec, vec, pl.BlockSpec((1, A_V), const2)]
                + _page_specs(p, kw, page, layer, lambda c, j: c * p + j)
                + _page_specs(p, page * A_HEADS, A_V, layer, lambda c, j: c * p + j))
    out = pl.pallas_call(
        functools.partial(_diff_dec_kernel, p=p, page=page, n_past=n_past, n_new=ts, lam_init=lam_init),
        grid_spec=pltpu.PrefetchScalarGridSpec(
            num_scalar_prefetch=1, grid=(b, n_pages // p), in_specs=in_specs,
            out_specs=pl.BlockSpec((None, rows // 2, A_V), lambda i, c, pt: (i, 0, 0)),
            scratch_shapes=[pltpu.VMEM((rows, 1), F32), pltpu.VMEM((rows, 1), F32), pltpu.VMEM((rows, vw), F32)]),
        out_shape=jax.ShapeDtypeStruct((b, rows // 2, A_V), F32),
        compiler_params=_cparams(("arbitrary", "arbitrary")), name="diff_dec")(
            page_table, qbd, info, _pad_new(kn, b, ts), _pad_new(vn, b, ts),
            *[x.reshape(1, HEAD_DIM) for x in lams], subg.reshape(1, A_V),
            *([cache_k] * p), *([cache_v] * p))
    return out.reshape(b, A_HEADS, ts, A_V).transpose(0, 2, 1, 3).reshape(b * ts, A_HEADS * A_V)


def _fox_dec_kernel(pt_ref, qbd_ref, pn_ref, pnt_ref, kn_ref, vn_ref, *rest, p, page, n_new):
    k_refs, v_refs, lf_refs = rest[:p], rest[p:2 * p], rest[2 * p:3 * p]
    o_ref, m_sc, l_sc, acc_sc, run_sc = rest[3 * p:]
    c = pl.program_id(1)
    qb = (qbd_ref[...] * SCALE).astype(BF16)
    rows = qb.shape[0]
    pn_col = pn_ref[:, 0:1]

    @pl.when(c == 0)
    def _():
        m_sc[...] = jnp.full(m_sc.shape, NEG_INF, F32)
        l_sc[...] = jnp.zeros(l_sc.shape, F32)
        acc_sc[...] = jnp.zeros(acc_sc.shape, F32)
        run_sc[...] = jnp.zeros(run_sc.shape, F32)
        idx = _iota((1, NEW_PAD), 1)
        tok = _iota((rows, 1), 0) // C_HEADS
        bias = pn_col - jnp.concatenate([pnt_ref[...]] * n_new, axis=0)
        s = _dot_nt(qb, kn_ref[...].astype(BF16)) + bias
        s = jnp.where((idx <= tok) & (idx < n_new), s, NEG_INF)
        _online_step(s, vn_ref[...].astype(BF16), m_sc, l_sc, acc_sc)

    later = (_iota((page, page), 0) > _iota((page, page), 1)).astype(BF16)
    run = run_sc[...]
    sufs = [None] * p
    for j in reversed(range(p)):
        lf = lf_refs[j][...]
        sufs[j] = _dot_exact_rhs(lf, later) + run
        run = run + jnp.sum(lf, axis=-1, keepdims=True)
    run_sc[...] = run
    suf = jnp.concatenate(sufs, axis=1)
    bias = jnp.concatenate([suf] * n_new, axis=0) + pn_col
    _flash_pages(qb, k_refs, v_refs, lambda s: s + bias, m_sc, l_sc, acc_sc)

    @pl.when(c == pl.num_programs(1) - 1)
    def _():
        full = acc_sc[...] / l_sc[...]
        rowh = _iota((rows, 1), 0) % C_HEADS
        o2 = jnp.zeros((rows, LANES), F32)
        for hp in range(C_HEADS // 2):
            o2 = o2 + jnp.where(rowh // 2 == hp, full[:, hp * LANES:(hp + 1) * LANES], 0.0)
        o_ref[...] = jnp.where(rowh % 2 == 0, o2[:, 0:HEAD_DIM], o2[:, HEAD_DIM:LANES])


def _fox_dec(q, kn, vn, cum_new, cache_k, cache_v, cache_lft, layer, page_table, pages_per_step):
    b, n_pages = page_table.shape
    ts = q.shape[0] // b
    w, page = cache_k.shape[2], cache_k.shape[3]
    p = pages_per_step
    nsteps = n_pages // p
    rows = ts * C_HEADS
    eye = jnp.eye(C_HEADS, dtype=F32).reshape(1, 1, C_HEADS, C_HEADS, 1)
    qbd = (q.reshape(b, ts, C_HEADS, 1, HEAD_DIM) * eye).reshape(b, rows, w)
    pn = jnp.pad(cum_new.reshape(b, rows, 1), ((0, 0), (0, 0), (0, LANES - 1)))
    pnt = jnp.pad(cum_new.transpose(0, 2, 1), ((0, 0), (0, 0), (0, NEW_PAD - ts)))
    per_b = lambda r, ww: pl.BlockSpec((None, r, ww), lambda i, c, pt: (i, 0, 0))
    rev = lambda c, j: (nsteps - 1 - c) * p + j
    in_specs = ([per_b(rows, w), per_b(rows, LANES), per_b(C_HEADS, NEW_PAD), per_b(NEW_PAD, w), per_b(NEW_PAD, w)]
                + _page_specs(p, w, page, layer, rev) + _page_specs(p, w, page, layer, rev)
                + _page_specs(p, C_HEADS, page, layer, rev))
    out = pl.pallas_call(
        functools.partial(_fox_dec_kernel, p=p, page=page, n_new=ts),
        grid_spec=pltpu.PrefetchScalarGridSpec(
            num_scalar_prefetch=1, grid=(b, nsteps), in_specs=in_specs,
            out_specs=pl.BlockSpec((None, rows, HEAD_DIM), lambda i, c, pt: (i, 0, 0)),
            scratch_shapes=[pltpu.VMEM((rows, 1), F32), pltpu.VMEM((rows, 1), F32), pltpu.VMEM((rows, w), F32),
                            pltpu.VMEM((C_HEADS, 1), F32)]),
        out_shape=jax.ShapeDtypeStruct((b, rows, HEAD_DIM), F32),
        compiler_params=_cparams(("arbitrary", "arbitrary")), name="fox_dec")(
            page_table, qbd, pn, pnt, _pad_new(kn, b, ts), _pad_new(vn, b, ts),
            *([cache_k] * p), *([cache_v] * p), *([cache_lft] * p))
    return out.reshape(b * ts, C_HEADS * HEAD_DIM)


def _nsa_dec_kernel(pt_ref, qbd_ref, info_ref, kc_ref, vc_ref, skn_ref, svn_ref, wkn_ref, wvn_ref, wk_ref, wv_ref,
                    *rest, p, page, n_past, n_new):
    k_refs, v_refs = rest[:p], rest[p:2 * p]
    o_ref, m_sc, l_sc, acc_sc, ocmp_sc, pick_sc = rest[2 * p:]
    c = pl.program_id(1)
    q32 = qbd_ref[...] * SCALE
    qb = q32.astype(BF16)
    rows = q32.shape[0]
    gt = rows // B_REP
    slope = info_ref[:, 0:1]
    qpos = info_ref[:, 1:2]
    in_g0 = (_iota((rows, 1), 0) % gt) < n_new
    pick_cols = lambda full: jnp.where(in_g0, full[:, 0:HEAD_DIM], full[:, HEAD_DIM:2 * HEAD_DIM])
    n_blocks = n_past // SEL_BLOCK
    bps = p * page // SEL_BLOCK

    @pl.when(c == 0)
    def _():
        m_sc[...] = jnp.full(m_sc.shape, NEG_INF, F32)
        l_sc[...] = jnp.zeros(l_sc.shape, F32)
        acc_sc[...] = jnp.zeros(acc_sc.shape, F32)
        n_cmp_rows = kc_ref.shape[0]
        n_cmp = n_cmp_rows - (CMP_BLOCK // CMP_STRIDE - 1)
        cidx = _iota((1, n_cmp_rows), 1)
        dc = qpos - (cidx * CMP_STRIDE + (CMP_BLOCK - 1)).astype(F32)
        sc = _dot_nt_precise(q32, kc_ref[...]) - slope * dc
        pc = _masked_softmax_rows(sc, (dc >= 0) & (cidx < n_cmp))
        ocmp_sc[...] = pick_cols(_dot(pc.astype(BF16), vc_ref[...].astype(BF16)))
        psum = pc[0:gt]
        for r in range(1, B_REP):
            psum = psum + pc[r * gt:(r + 1) * gt]
        nbp = -(-n_blocks // LANES) * LANES
        hi, mid, lo = _split3(jnp.concatenate([psum, jnp.zeros((LANES - gt, n_cmp_rows), F32)], axis=0))
        blk_c = _iota((nbp, n_cmp_rows), 0) * SEL_BLOCK
        cmp_c = _iota((nbp, n_cmp_rows), 1) * CMP_STRIDE
        overlap_t = ((cmp_c < blk_c + SEL_BLOCK) & (cmp_c + CMP_BLOCK > blk_c)).astype(BF16)
        imp_t = _dot_nt(overlap_t, hi) + _dot_nt(overlap_t, mid) + _dot_nt(overlap_t, lo)
        blk_t = _iota((nbp, LANES), 0)
        qpos_t = n_past + _iota((1, LANES), 1) % n_new
        forced_t = (blk_t == 0) | (blk_t == qpos_t // SEL_BLOCK)
        valid_t = (blk_t * SEL_BLOCK <= qpos_t) & (blk_t < n_blocks)
        score_t = jnp.where(forced_t, 1e30, jnp.where(valid_t, imp_t, -1.0))
        picked = _top_blocks(score_t, min(SEL_TOP, n_blocks + 1) - 1, 0).T[0:gt]
        for cc in range(pick_sc.shape[0]):
            pick_sc[cc] = picked[:, cc * bps:(cc + 1) * bps]

    keys = p * page
    kpos = c * keys + _iota((1, keys), 1)
    expand = (_iota((bps, keys), 0) == _iota((bps, keys), 1) // SEL_BLOCK).astype(BF16)
    sel1 = _dot(pick_sc[c].astype(BF16), expand)
    sel = jnp.concatenate([sel1] * B_REP, axis=0)
    ds = qpos - kpos.astype(F32)
    _flash_pages(qb, k_refs, v_refs, lambda s: jnp.where((sel > 0.5) & (ds >= 0), s - slope * ds, NEG_INF),
                 m_sc, l_sc, acc_sc)

    @pl.when(c == pl.num_programs(1) - 1)
    def _():
        idx = _iota((1, NEW_PAD), 1)
        dn = qpos - (n_past + idx).astype(F32)
        new_ok = (dn >= 0) & (idx < n_new)
        s = jnp.where(new_ok, _dot_nt(qb, skn_ref[...].astype(BF16)) - slope * dn, NEG_INF)
        _online_step(s, svn_ref[...].astype(BF16), m_sc, l_sc, acc_sc)
        o_sel = pick_cols(acc_sc[...] / jnp.maximum(l_sc[...], 1e-30))
        keep = wk_ref.shape[1]
        wpos = n_past - keep + _iota((1, keep), 1)
        dw = qpos - wpos.astype(F32)
        ok1 = (dw >= 0) & (dw < WINDOW) & (wpos >= 0)
        ok2 = new_ok & (dn < WINDOW)
        s1 = jnp.where(ok1, _dot(qb, wk_ref[...].astype(BF16)) - slope * dw, NEG_INF)
        s2 = jnp.where(ok2, _dot_nt(qb, wkn_ref[...].astype(BF16)) - slope * dn, NEG_INF)
        mx = jnp.maximum(jnp.max(s1, axis=-1, keepdims=True), jnp.max(s2, axis=-1, keepdims=True))
        e1 = jnp.where(ok1, jnp.exp(s1 - mx), 0.0)
        e2 = jnp.where(ok2, jnp.exp(s2 - mx), 0.0)
        den = jnp.maximum(jnp.sum(e1, axis=-1, keepdims=True) + jnp.sum(e2, axis=-1, keepdims=True), 1e-30)
        o_win = pick_cols((_dot_nt(e1.astype(BF16), wv_ref[...].astype(BF16))
                           + _dot(e2.astype(BF16), wvn_ref[...].astype(BF16))) / den)
        gates = jax.nn.sigmoid(info_ref[:, 2:5])
        o_ref[...] = gates[:, 0:1] * ocmp_sc[...] + gates[:, 1:2] * o_sel + gates[:, 2:3] * o_win


def _nsa_dec(q, gate, kc, vc, skn, svn, wkn, wvn, win_k, win_v, cache_sk, cache_sv, layer, page_table, slopes,
             pages_per_step):
    b, n_pages = page_table.shape
    ts = q.shape[0] // b
    w, page = cache_sk.shape[2], cache_sk.shape[3]
    n_past = n_pages * page
    keep = win_k.shape[3]
    p = pages_per_step
    rows = B_HEADS * ts
    q5 = q.reshape(b, ts, B_KV, B_REP, HEAD_DIM).transpose(0, 3, 2, 1, 4)
    eye = jnp.eye(B_KV, dtype=F32).reshape(1, 1, B_KV, 1, B_KV, 1)
    qbd = (q5[:, :, :, :, None, :] * eye).reshape(b, rows, w)
    slope_r = jnp.broadcast_to(slopes.reshape(B_KV, B_REP).T.reshape(B_REP, B_KV, 1), (B_REP, B_KV, ts)).reshape(rows)
    qpos_r = jnp.broadcast_to((n_past + jnp.arange(ts, dtype=F32)).reshape(1, 1, ts), (B_REP, B_KV, ts)).reshape(rows)
    glog = gate[:, :3 * B_HEADS].reshape(b, ts, B_KV, B_REP, 3).transpose(0, 3, 2, 1, 4).reshape(b, rows, 3)
    info = jnp.concatenate([jnp.broadcast_to(jnp.stack([slope_r, qpos_r], axis=1)[None], (b, rows, 2)), glog], axis=2)
    info = jnp.pad(info, ((0, 0), (0, 0), (0, LANES - 5)))
    per_b = lambda r, ww: pl.BlockSpec((None, r, ww), lambda i, c, pt: (i, 0, 0))
    win = pl.BlockSpec((None, None, w, keep), lambda i, c, pt: (layer, i, 0, 0))
    ncr = kc.shape[1]
    in_specs = ([per_b(rows, w), per_b(rows, LANES), per_b(ncr, w), per_b(ncr, w)] + [per_b(NEW_PAD, w)] * 4
                + [win, win]
                + _page_specs(p, w, page, layer, lambda c, j: c * p + j)
                + _page_specs(p, w, page, layer, lambda c, j: c * p + j))
    out = pl.pallas_call(
        functools.partial(_nsa_dec_kernel, p=p, page=page, n_past=n_past, n_new=ts),
        grid_spec=pltpu.PrefetchScalarGridSpec(
            num_scalar_prefetch=1, grid=(b, n_pages // p), in_specs=in_specs,
            out_specs=pl.BlockSpec((None, rows, HEAD_DIM), lambda i, c, pt: (i, 0, 0)),
            scratch_shapes=[pltpu.VMEM((rows, 1), F32), pltpu.VMEM((rows, 1), F32), pltpu.VMEM((rows, w), F32),
                            pltpu.VMEM((rows, HEAD_DIM), F32), pltpu.VMEM((n_pages // p, rows // B_REP, p * page // SEL_BLOCK), F32)]),
        out_shape=jax.ShapeDtypeStruct((b, rows, HEAD_DIM), F32),
        compiler_params=_cparams(("arbitrary", "arbitrary")), name="nsa_dec")(
            page_table, qbd, info, kc, vc, _pad_new(skn, b, ts), _pad_new(svn, b, ts), _pad_new(wkn, b, ts),
            _pad_new(wvn, b, ts), win_k, win_v, *([cache_sk] * p), *([cache_sv] * p))
    return out.reshape(b, B_REP, B_KV, ts, HEAD_DIM).transpose(0, 3, 2, 1, 4).reshape(b * ts, B_HEADS * HEAD_DIM)


def _alibi_slopes(n):
    return jnp.exp2(-8.0 * jnp.arange(1, n + 1, dtype=F32) / n)


def _pages_per_step(n_pages, want):
    p = min(want, n_pages)
    while n_pages % p:
        p -= 1
    return p


def kernel(x_prompt, x_sample, cache_diff_k, cache_diff_v, cache_nsa_cmp_k, cache_nsa_cmp_v, cache_nsa_sel_k, cache_nsa_sel_v, state_nsa_win_k, state_nsa_win_v, cache_fox_k, cache_fox_v, cache_fox_logf, page_table, norm_ffn1_g, ffn1_wg, ffn1_wu, ffn1_wd, norm_mix_g, norm_ffn2_g, ffn2_wg, ffn2_wu, ffn2_wd, even_w_in, even_w_out, diff_lambda_q1, diff_lambda_k1, diff_lambda_q2, diff_lambda_k2, diff_subln_g, cmp_pos_k, cmp_w1_k, cmp_w2_k, cmp_pos_v, cmp_w1_v, cmp_w2_v, odd_w_in, odd_b_f, odd_w_out, final_norm_g):
    b, t, d = x_prompt.shape
    bs, ts, _ = x_sample.shape
    depth = norm_ffn1_g.shape[0]
    n_pages = page_table.shape[1]
    n_pool, page = cache_diff_k.shape[1], cache_diff_k.shape[2]
    keep = state_nsa_win_k.shape[2]
    n_past = n_pages * page
    assert n_past % SEL_BLOCK == 0 and page % SEL_BLOCK == 0 and page % CMP_STRIDE == 0
    assert 0 < ts <= min(SEL_BLOCK, CMP_STRIDE - 1, NEW_PAD) and (A_HEADS * ts) % 8 == 0
    assert t % LANES == 0 and n_past >= WINDOW
    kvw = B_KV * HEAD_DIM

    slopes_a = _alibi_slopes(A_HEADS)
    slopes_b = _alibi_slopes(B_HEADS)
    keys_last = lambda c: jnp.moveaxis(c, 2, -1).reshape(c.shape[0], c.shape[1], -1, c.shape[2])
    c_diff_k = keys_last(cache_diff_k)
    c_diff_v = cache_diff_v.reshape(cache_diff_v.shape[0], n_pool, page * A_HEADS, A_V)
    c_cmp_k, c_cmp_v = keys_last(cache_nsa_cmp_k), keys_last(cache_nsa_cmp_v)
    c_sel_k, c_sel_v = keys_last(cache_nsa_sel_k), keys_last(cache_nsa_sel_v)
    c_fox_k, c_fox_v = keys_last(cache_fox_k), keys_last(cache_fox_v)
    c_fox_lft = keys_last(cache_fox_logf)
    win_k, win_v = keys_last(state_nsa_win_k), keys_last(state_nsa_win_v)

    a_w = A_HEADS * 2 * HEAD_DIM
    even_cuts = [a_w, a_w, A_HEADS * A_V, B_HEADS * HEAD_DIM] + [kvw] * 6 + [3 * B_HEADS]
    odd_cuts = [C_HEADS * HEAD_DIM] * 3 + [C_HEADS]

    def split_w(w, cuts):
        out, s = [], 0
        for c in cuts:
            out.append(_pad_cols(w[:, s:s + c], -(-c // LANES) * LANES).astype(BF16))
            s += c
        return out

    xp = x_prompt.reshape(b * t, d)
    xs = x_sample.reshape(bs * ts, d)
    even_p = [[] for _ in range(8)]
    even_s = [[] for _ in range(8)]
    odd_p = [[] for _ in range(3)]
    odd_s = [[] for _ in range(3)]

    for li in range(depth):
        w1 = [ffn1_wg[li].astype(BF16), ffn1_wu[li].astype(BF16), ffn1_wd[li].astype(BF16)]
        xp = _ffn(xp, norm_ffn1_g[li], *w1)
        xs = _ffn(xs, norm_ffn1_g[li], *w1)
        if li % 2 == 0:
            e = li // 2
            lam_init = 0.8 - 0.6 * math.exp(-0.3 * li)
            lams = (diff_lambda_q1[e], diff_lambda_k1[e], diff_lambda_q2[e], diff_lambda_k2[e])
            ws = split_w(even_w_in[e], even_cuts)
            wo = even_w_out[e].astype(BF16)
            wo_a, wo_b = wo[:A_HEADS * A_V], wo[A_HEADS * A_V:]
            aq, ak, av, bq, ck, cv, sk, sv, wk, wv, bg = _norm_proj(xp, norm_mix_g[li], ws)
            r3 = lambda a: a.reshape(b, t, a.shape[-1])
            oa = _diff_prompt(r3(aq), r3(ak), r3(av), slopes_a, lams, diff_subln_g[e], lam_init)
            kc = _compress_prompt(r3(ck), cmp_pos_k[e], cmp_w1_k[e], cmp_w2_k[e])
            vc = _compress_prompt(r3(cv), cmp_pos_v[e], cmp_w1_v[e], cmp_w2_v[e])
            ob = _nsa_prompt(r3(bq), r3(bg), kc, vc, r3(sk), r3(sv), r3(wk), r3(wv), slopes_b)
            xp = _out_proj(xp, [oa.reshape(b * t, -1), ob.reshape(b * t, -1)], [wo_a, wo_b])

            def last_rows(a):
                a = jnp.pad(r3(a), ((0, 0), (max(0, keep - t), 0), (0, 0)))
                return a[:, a.shape[1] - keep:].reshape(b, keep, B_KV, HEAD_DIM)

            rows_p = (ak.reshape(b, t, A_HEADS, 2, HEAD_DIM), av.reshape(b, t, A_HEADS, A_V),
                      ck.reshape(b, t, B_KV, HEAD_DIM), cv.reshape(b, t, B_KV, HEAD_DIM),
                      sk.reshape(b, t, B_KV, HEAD_DIM), sv.reshape(b, t, B_KV, HEAD_DIM), last_rows(wk), last_rows(wv))
            for lst, r in zip(even_p, rows_p):
                lst.append(r)
            aq, ak, av, bq, ck, cv, sk, sv, wk, wv, bg = _norm_proj(xs, norm_mix_g[li], ws)
            oa = _diff_dec(aq, ak, av, c_diff_k, c_diff_v, e, page_table, slopes_a, lams, diff_subln_g[e], lam_init,
                           _pages_per_step(n_pages, 32))
            pps = _pages_per_step(n_pages, 32)
            kc, vc = _compress_paged(c_cmp_k, c_cmp_v, e, page_table, cmp_pos_k[e], cmp_w1_k[e], cmp_w2_k[e],
                                     cmp_pos_v[e], cmp_w1_v[e], cmp_w2_v[e], pps)
            ob = _nsa_dec(bq, bg, kc, vc, sk, sv, wk, wv, win_k, win_v, c_sel_k, c_sel_v, e, page_table, slopes_b, pps)
            xs = _out_proj(xs, [oa, ob], [wo_a, wo_b])
            s3 = lambda a: a.reshape(bs, ts, a.shape[-1])
            new_win = lambda buf, a: jnp.concatenate(
                [buf[e], s3(a).reshape(bs, ts, B_KV, HEAD_DIM)], axis=1)[:, ts:]
            rows_s = (ak.reshape(bs, ts, A_HEADS, 2, HEAD_DIM), av.reshape(bs, ts, A_HEADS, A_V),
                      ck.reshape(bs, ts, B_KV, HEAD_DIM), cv.reshape(bs, ts, B_KV, HEAD_DIM),
                      sk.reshape(bs, ts, B_KV, HEAD_DIM), sv.reshape(bs, ts, B_KV, HEAD_DIM),
                      new_win(state_nsa_win_k, wk), new_win(state_nsa_win_v, wv))
            for lst, r in zip(even_s, rows_s):
                lst.append(r)
        else:
            o = li // 2
            ws = split_w(odd_w_in[o], odd_cuts)
            wo = odd_w_out[o].astype(BF16)
            q, k, v, f = _norm_proj(xp, norm_mix_g[li], ws)
            r3 = lambda a: a.reshape(b, t, a.shape[-1])
            logf, cum = _logf_cum(r3(f), odd_b_f[o])
            om = _fox_prompt(r3(q), r3(k), r3(v), cum)
            xp = _out_proj(xp, [om.reshape(b * t, -1)], [wo])
            for lst, r in zip(odd_p, (k.reshape(b, t, C_HEADS, HEAD_DIM), v.reshape(b, t, C_HEADS, HEAD_DIM), logf)):
                lst.append(r)
            q, k, v, f = _norm_proj(xs, norm_mix_g[li], ws)
            logf, cum = _logf_cum(f.reshape(bs, ts, -1), odd_b_f[o])
            om = _fox_dec(q, k, v, cum, c_fox_k, c_fox_v, c_fox_lft, o, page_table, _pages_per_step(n_pages, 16))
            xs = _out_proj(xs, [om], [wo])
            for lst, r in zip(odd_s, (k.reshape(bs, ts, C_HEADS, HEAD_DIM), v.reshape(bs, ts, C_HEADS, HEAD_DIM), logf)):
                lst.append(r)
        w2 = [ffn2_wg[li].astype(BF16), ffn2_wu[li].astype(BF16), ffn2_wd[li].astype(BF16)]
        fin = final_norm_g if li == depth - 1 else None
        xp = _ffn(xp, norm_ffn2_g[li], *w2, final_g=fin)
        xs = _ffn(xs, norm_ffn2_g[li], *w2, final_g=fin)

    stk = lambda lst: jnp.stack(lst, 0)
    ep = [stk(l) for l in even_p]
    es = [stk(l) for l in even_s]
    op = [stk(l) for l in odd_p]
    os_ = [stk(l) for l in odd_s]
    out = [xp.reshape(b, t, d), xs.reshape(bs, ts, d)]
    for p_, s_ in zip(ep, es):
        out += [p_, s_]
    for p_, s_ in zip(op, os_):
        out += [p_, s_]
    return tuple(out)
```

```python
import functools
import math

import jax
import jax.numpy as jnp
from jax import lax
from jax.experimental import pallas as pl
from jax.experimental.pallas import tpu as pltpu

HEAD_DIM = 64
A_HEADS = 4
A_V = 2 * HEAD_DIM
B_HEADS = 8
B_KV = 2
B_REP = B_HEADS // B_KV
CMP_BLOCK = 32
CMP_STRIDE = 16
CMP_HIDDEN = 128
SEL_BLOCK = 64
SEL_TOP = 16
WINDOW = 512
C_HEADS = 16
RMS_EPS = 1e-6
NEG_INF = -1e30
SCALE = HEAD_DIM ** -0.5

LANES = 128
V7X_VMEM_LIMIT_BYTES = 56 * 1024 * 1024

F32 = jnp.float32
BF16 = jnp.bfloat16


def _cparams(sem):
    return pltpu.CompilerParams(dimension_semantics=sem, vmem_limit_bytes=V7X_VMEM_LIMIT_BYTES)


def _dot(a, b):
    return jnp.dot(a, b, preferred_element_type=F32)


def _dot_nt(a, b):
    return lax.dot_general(a, b, (((1,), (1,)), ((), ())), preferred_element_type=F32)


def _split3(x):
    hi = x.astype(BF16)
    r1 = x - hi.astype(F32)
    mid = r1.astype(BF16)
    lo = (r1 - mid.astype(F32)).astype(BF16)
    return hi, mid, lo


def _dot_exact_rhs(x, m_bf16):
    hi, mid, lo = _split3(x)
    return _dot(hi, m_bf16) + _dot(mid, m_bf16) + _dot(lo, m_bf16)


def _dot_nt_precise(a, b):
    ah = a.astype(BF16)
    al = (a - ah.astype(F32)).astype(BF16)
    bh = b.astype(BF16)
    bl = (b - bh.astype(F32)).astype(BF16)
    return _dot_nt(ah, bh) + _dot_nt(ah, bl) + _dot_nt(al, bh)


def _rms(x, g):
    ms = jnp.mean(x * x, axis=-1, keepdims=True)
    return x * lax.rsqrt(ms + RMS_EPS) * g


def _iota(shape, dim):
    return lax.broadcasted_iota(jnp.int32, shape, dim)


def _masked_softmax_rows(s, mask):
    s = jnp.where(mask, s, NEG_INF)
    e = jnp.where(mask, jnp.exp(s - jnp.max(s, axis=-1, keepdims=True)), 0.0)
    return e / jnp.maximum(jnp.sum(e, axis=-1, keepdims=True), 1e-30)


def _online_step(s, v_bf16, m_ref, l_ref, acc_ref):
    m_old = m_ref[...]
    m_new = jnp.maximum(m_old, jnp.max(s, axis=-1, keepdims=True))
    alpha = jnp.exp(m_old - m_new)
    p = jnp.exp(s - m_new)
    l_ref[...] = alpha * l_ref[...] + jnp.sum(p, axis=-1, keepdims=True)
    acc_ref[...] = alpha * acc_ref[...] + _dot(p.astype(BF16), v_bf16)
    m_ref[...] = m_new


def _row_tile(n, pref):
    for t in pref:
        if n % t == 0:
            return t
    return n


def _ffn_kernel(x_ref, g_ref, wg_ref, wu_ref, wd_ref, *rest, fc, nfc, has_final):
    if has_final:
        gf_ref, o_ref = rest
    else:
        (o_ref,) = rest
    x = x_ref[...]
    h = _rms(x, g_ref[...]).astype(BF16)
    acc = jnp.zeros(x.shape, F32)
    for c in range(nfc):
        sl = slice(c * fc, (c + 1) * fc)
        a = _dot(h, wg_ref[:, sl])
        u = _dot(h, wu_ref[:, sl])
        act = (a * jax.nn.sigmoid(a) * u).astype(BF16)
        acc = acc + _dot(act, wd_ref[sl, :])
    y = x + 0.5 * acc
    if has_final:
        y = _rms(y, gf_ref[...])
    o_ref[...] = y


def _ffn(x, g, wg, wu, wd, final_g=None):
    n, d = x.shape
    f = wg.shape[1]
    tm = _row_tile(n, (512, 256, 128))
    fc = _row_tile(f, (256, 128))
    const = lambda i: (0, 0)
    in_specs = [pl.BlockSpec((tm, d), lambda i: (i, 0)), pl.BlockSpec((1, d), const),
                pl.BlockSpec((d, f), const), pl.BlockSpec((d, f), const), pl.BlockSpec((f, d), const)]
    args = [x, g.reshape(1, d), wg, wu, wd]
    if final_g is not None:
        in_specs.append(pl.BlockSpec((1, d), const))
        args.append(final_g.reshape(1, d))
    return pl.pallas_call(
        functools.partial(_ffn_kernel, fc=fc, nfc=f // fc, has_final=final_g is not None),
        grid=(n // tm,), in_specs=in_specs, out_specs=pl.BlockSpec((tm, d), lambda i: (i, 0)),
        out_shape=jax.ShapeDtypeStruct((n, d), F32), compiler_params=_cparams(("arbitrary",)),
        name="ffn")(*args)


def _norm_proj_kernel(x_ref, g_ref, *refs, n_out):
    h = _rms(x_ref[...], g_ref[...]).astype(BF16)
    for w_ref, o_ref in zip(refs[:n_out], refs[n_out:]):
        o_ref[...] = _dot(h, w_ref[...])


def _norm_proj(x, g, ws):
    n, d = x.shape
    tm = _row_tile(n, (512, 256, 128))
    const = lambda i: (0, 0)
    in_specs = [pl.BlockSpec((tm, d), lambda i: (i, 0)), pl.BlockSpec((1, d), const)]
    in_specs += [pl.BlockSpec(w.shape, const) for w in ws]
    return pl.pallas_call(
        functools.partial(_norm_proj_kernel, n_out=len(ws)),
        grid=(n // tm,), in_specs=in_specs,
        out_specs=[pl.BlockSpec((tm, w.shape[1]), lambda i: (i, 0)) for w in ws],
        out_shape=[jax.ShapeDtypeStruct((n, w.shape[1]), F32) for w in ws],
        compiler_params=_cparams(("arbitrary",)), name="norm_proj")(x, g.reshape(1, d), *ws)


def _out_proj_kernel(x_ref, *refs, n_in):
    y = x_ref[...]
    for o_ref, w_ref in zip(refs[:n_in], refs[n_in:2 * n_in]):
        y = y + _dot(o_ref[...].astype(BF16), w_ref[...])
    refs[2 * n_in][...] = y


def _out_proj(x, os_, ws):
    n, d = x.shape
    tm = _row_tile(n, (512, 256, 128))
    const = lambda i: (0, 0)
    in_specs = [pl.BlockSpec((tm, d), lambda i: (i, 0))]
    in_specs += [pl.BlockSpec((tm, o.shape[1]), lambda i: (i, 0)) for o in os_]
    in_specs += [pl.BlockSpec(w.shape, const) for w in ws]
    return pl.pallas_call(
        functools.partial(_out_proj_kernel, n_in=len(os_)),
        grid=(n // tm,), in_specs=in_specs, out_specs=pl.BlockSpec((tm, d), lambda i: (i, 0)),
        out_shape=jax.ShapeDtypeStruct((n, d), F32), compiler_params=_cparams(("arbitrary",)),
        name="out_proj")(x, *os_, *ws)


def _pad_cols(w, width):
    return jnp.pad(w, ((0, 0), (0, width - w.shape[1])))


def _logf_cum_kernel(f_ref, b_ref, lf_ref, cum_ref, carry_ref, *, tc):
    @pl.when(pl.program_id(1) == 0)
    def _():
        carry_ref[...] = jnp.zeros(carry_ref.shape, F32)

    z = f_ref[...] + b_ref[...]
    lf = jnp.minimum(z, 0.0) - jnp.log1p(jnp.exp(-jnp.abs(z)))
    tri = (_iota((tc, tc), 0) >= _iota((tc, tc), 1)).astype(BF16)
    cum = _dot_exact_rhs_left(tri, lf) + carry_ref[...]
    lf_ref[...] = lf[:, :C_HEADS]
    cum_ref[...] = cum[:, :C_HEADS]
    carry_ref[...] = cum[tc - 1:tc, :]


def _dot_exact_rhs_left(m_bf16, x):
    hi, mid, lo = _split3(x)
    return _dot(m_bf16, hi) + _dot(m_bf16, mid) + _dot(m_bf16, lo)


def _logf_cum(f, b_f):
    b, t, _ = f.shape
    tc = _row_tile(t, (256, 128))
    bias = jnp.pad(b_f, (0, LANES - C_HEADS)).reshape(1, LANES)
    out = jax.ShapeDtypeStruct((b, t, C_HEADS), F32)
    return pl.pallas_call(
        functools.partial(_logf_cum_kernel, tc=tc),
        grid=(b, t // tc),
        in_specs=[pl.BlockSpec((None, tc, LANES), lambda i, j: (i, j, 0)), pl.BlockSpec((1, LANES), lambda i, j: (0, 0))],
        out_specs=[pl.BlockSpec((None, tc, C_HEADS), lambda i, j: (i, j, 0))] * 2,
        out_shape=[out, out], scratch_shapes=[pltpu.VMEM((1, LANES), F32)],
        compiler_params=_cparams(("arbitrary", "arbitrary")), name="logf_cum")(f, bias)


LOG2E = 1.4426950408889634


def _lanes(x, n):
    if n % LANES == 0:
        return x if n == LANES else jnp.tile(x, (1, n // LANES))
    return x[:, :n]


def _flash_update(s, v_aug, m_ref, acc_ref):
    m_prev = m_ref[...]
    m_next = jnp.maximum(m_prev, jnp.max(s, axis=-1, keepdims=True))
    p = jnp.exp2(s - _lanes(m_next, s.shape[1]))
    alpha = jnp.exp2(m_prev - m_next)
    acc_ref[...] = acc_ref[...] * _lanes(alpha, acc_ref.shape[1]) + _dot(p.astype(BF16), v_aug)
    m_ref[...] = m_next


def _causal_tiles(q0, tq, tk, tile_fn):
    n_full = q0 // tk
    n_k = (q0 + tq + tk - 1) // tk

    def run(masked):
        def body(kj, carry):
            tile_fn(kj, masked)
            return carry
        return body

    lax.fori_loop(0, n_full, run(False), 0)
    lax.fori_loop(n_full, n_k, run(True), 0)


def _stage_bf16(src_ref, dst_ref, tk):
    for j in range(src_ref.shape[0] // tk):
        rows = slice(j * tk, (j + 1) * tk)
        dst_ref[rows, :] = src_ref[rows, :].astype(BF16)


def _stage_values(src_ref, dst_ref, tk, cols, width):
    ones = jnp.ones((tk, width), BF16)
    for j in range(src_ref.shape[0] // tk):
        rows = slice(j * tk, (j + 1) * tk)
        dst_ref[rows, :] = jnp.concatenate([src_ref[rows, cols].astype(BF16), ones], axis=1)


def _fox_prompt_kernel(q_ref, k_ref, v_ref, fq_ref, fk_ref, o_ref, m_sc, acc_sc, kb_sc, va_sc, *, tq, tk, nsub):
    subs = [slice(c * HEAD_DIM, (c + 1) * HEAD_DIM) for c in range(nsub)]

    @pl.when(pl.program_id(2) == 0)
    def _():
        _stage_bf16(k_ref, kb_sc, tk)
        for c, cs in enumerate(subs):
            _stage_values(v_ref, va_sc.at[c], tk, cs, HEAD_DIM)

    q0 = pl.program_id(2) * tq
    qpos = q0 + _iota((tq, 1), 0)
    m_sc[...] = jnp.full(m_sc.shape, NEG_INF, F32)
    acc_sc[...] = jnp.zeros(acc_sc.shape, F32)
    qs = [(q_ref[:, cs] * (SCALE * LOG2E)).astype(BF16) for cs in subs]
    f0 = [fq_ref[0:1, c:c + 1] for c in range(nsub)]

    def tile(kj, masked):
        k0 = pl.multiple_of(kj * tk, tk)
        if masked:
            ok = qpos >= k0 + _iota((1, tk), 1)
        for c, cs in enumerate(subs):
            s = _dot_nt(qs[c], kb_sc[pl.ds(k0, tk), cs]) + (f0[c] - fk_ref[kj, c:c + 1, :]) * LOG2E
            if masked:
                s = jnp.where(ok, s, NEG_INF)
            _flash_update(s, va_sc[c, pl.ds(k0, tk), :], m_sc.at[c], acc_sc.at[c])

    _causal_tiles(q0, tq, tk, tile)
    for c, cs in enumerate(subs):
        o_ref[:, cs] = acc_sc[c][:, 0:HEAD_DIM] / acc_sc[c][:, HEAD_DIM:LANES]


def _fox_prompt(q, k, v, cum):
    b, t, w = q.shape
    tq = _row_tile(t, (512, 256, 128))
    tk = tq
    nsub = 8
    ngrp = C_HEADS // nsub
    wb = nsub * HEAD_DIM
    fq = cum.reshape(b, t, ngrp, nsub).transpose(0, 2, 1, 3)
    fk = cum.reshape(b, t // tk, tk, ngrp, nsub).transpose(0, 3, 1, 4, 2)
    return pl.pallas_call(
        functools.partial(_fox_prompt_kernel, tq=tq, tk=tk, nsub=nsub),
        grid=(b, ngrp, t // tq),
        in_specs=[pl.BlockSpec((None, tq, wb), lambda i, h, j: (i, j, h)),
                  pl.BlockSpec((None, t, wb), lambda i, h, j: (i, 0, h), pipeline_mode=pl.Buffered(1)),
                  pl.BlockSpec((None, t, wb), lambda i, h, j: (i, 0, h), pipeline_mode=pl.Buffered(1)),
                  pl.BlockSpec((None, None, tq, nsub), lambda i, h, j: (i, h, j, 0)),
                  pl.BlockSpec((None, None, t // tk, nsub, tk), lambda i, h, j: (i, h, 0, 0, 0))],
        out_specs=pl.BlockSpec((None, tq, wb), lambda i, h, j: (i, j, h)),
        out_shape=jax.ShapeDtypeStruct((b, t, w), F32),
        scratch_shapes=[pltpu.VMEM((nsub, tq, LANES), F32), pltpu.VMEM((nsub, tq, LANES), F32),
                        pltpu.VMEM((t, wb), BF16), pltpu.VMEM((nsub, t, LANES), BF16)],
        compiler_params=_cparams(("arbitrary",) * 3), name="fox_prompt")(q, k, v, fq, fk)


def _diff_lambda(lq1, lk1, lq2, lk2, lam_init):
    return (jnp.exp(jnp.sum(lq1 * lk1, axis=-1, keepdims=True))
            - jnp.exp(jnp.sum(lq2 * lk2, axis=-1, keepdims=True)) + lam_init)


def _diff_prompt_kernel(slopes_ref, q_ref, k_ref, v_ref, lq1, lk1, lq2, lk2, subg_ref, o_ref,
                        m_sc, acc_sc, kb_sc, va_sc, *, tq, tk, lam_init, nh):
    @pl.when(pl.program_id(2) == 0)
    def _():
        _stage_bf16(k_ref, kb_sc, tk)
        for h in range(nh):
            _stage_values(v_ref, va_sc.at[h], tk, slice(h * A_V, (h + 1) * A_V), A_V)

    q0 = pl.program_id(2) * tq
    qpos = q0 + _iota((tq, 1), 0)
    m_sc[...] = jnp.full(m_sc.shape, NEG_INF, F32)
    acc_sc[...] = jnp.zeros(acc_sc.shape, F32)
    subs = [slice(c * HEAD_DIM, (c + 1) * HEAD_DIM) for c in range(2 * nh)]
    qs = [(q_ref[:, cs] * (SCALE * LOG2E)).astype(BF16) for cs in subs]
    slope2 = [slopes_ref[pl.program_id(1) * nh + h] * LOG2E for h in range(nh)]

    def tile(kj, masked):
        k0 = pl.multiple_of(kj * tk, tk)
        kpos = k0 + _iota((1, tk), 1)
        krel = (kpos - q0).astype(F32)
        if masked:
            ok = qpos >= kpos
        for c, cs in enumerate(subs):
            s = _dot_nt(qs[c], kb_sc[pl.ds(k0, tk), cs]) + slope2[c // 2] * krel
            if masked:
                s = jnp.where(ok, s, NEG_INF)
            _flash_update(s, va_sc[c // 2, pl.ds(k0, tk), :], m_sc.at[c], acc_sc.at[c])

    _causal_tiles(q0, tq, tk, tile)
    lam = _diff_lambda(lq1[...], lk1[...], lq2[...], lk2[...], lam_init)
    for h in range(nh):
        a0, a1 = acc_sc[2 * h], acc_sc[2 * h + 1]
        o = a0[:, 0:A_V] / a0[:, A_V:2 * A_V] - lam * (a1[:, 0:A_V] / a1[:, A_V:2 * A_V])
        o_ref[:, h * A_V:(h + 1) * A_V] = _rms(o, subg_ref[...]) * (1.0 - lam_init)


def _diff_prompt(q, k, v, slopes, lams, subg, lam_init):
    b, t, w = q.shape
    tq = _row_tile(t, (512, 256, 128))
    tk = tq
    nh = 2
    wb = nh * A_V
    vec = pl.BlockSpec((1, HEAD_DIM), lambda i, h, j: (0, 0))
    return pl.pallas_call(
        functools.partial(_diff_prompt_kernel, tq=tq, tk=tk, lam_init=lam_init, nh=nh),
        grid=(b, A_HEADS // nh, t // tq),
        in_specs=[pl.BlockSpec(memory_space=pltpu.SMEM),
                  pl.BlockSpec((None, tq, wb), lambda i, h, j: (i, j, h)),
                  pl.BlockSpec((None, t, wb), lambda i, h, j: (i, 0, h)),
                  pl.BlockSpec((None, t, wb), lambda i, h, j: (i, 0, h)),
                  vec, vec, vec, vec, pl.BlockSpec((1, A_V), lambda i, h, j: (0, 0))],
        out_specs=pl.BlockSpec((None, tq, wb), lambda i, h, j: (i, j, h)),
        out_shape=jax.ShapeDtypeStruct((b, t, w), F32),
        scratch_shapes=[pltpu.VMEM((2 * nh, tq, LANES), F32), pltpu.VMEM((2 * nh, tq, 2 * A_V), F32),
                        pltpu.VMEM((t, wb), BF16), pltpu.VMEM((nh, t, 2 * A_V), BF16)],
        compiler_params=_cparams(("arbitrary",) * 3), name="diff_prompt")(
            slopes, q, k, v, *[x.reshape(1, HEAD_DIM) for x in lams], subg.reshape(1, A_V))


def _cmp_weights(pos, w1):
    ratio = CMP_BLOCK // CMP_STRIDE
    w1r = w1.reshape(ratio, CMP_STRIDE, 1, HEAD_DIM, 1, CMP_HIDDEN)
    eye = jnp.eye(B_KV, dtype=w1.dtype).reshape(1, 1, B_KV, 1, B_KV, 1)
    big = (w1r * eye).reshape(ratio, CMP_STRIDE * B_KV * HEAD_DIM, B_KV * CMP_HIDDEN)
    w1big = jnp.concatenate([big[r] for r in range(ratio)], axis=1)
    posr = jnp.broadcast_to(pos.reshape(ratio, CMP_STRIDE, 1, HEAD_DIM), (ratio, CMP_STRIDE, B_KV, HEAD_DIM))
    posrows = jnp.pad(posr.reshape(ratio, CMP_STRIDE * B_KV * HEAD_DIM), ((0, 8 - ratio), (0, 0)))
    return w1big.astype(BF16), posrows


def _cmp_a_kernel(x_ref, w_ref, o_ref):
    o_ref[...] = _dot(x_ref[...].astype(BF16), w_ref[...])


def _cmp_b_kernel(a_ref, posrows_ref, w1_ref, w2_ref, o_ref):
    n = a_ref.shape[1]
    hw = B_KV * CMP_HIDDEN
    pc = _dot(posrows_ref[...].astype(BF16), w1_ref[...])
    posc = pc[0:1, 0:hw] + pc[1:2, hw:2 * hw]
    w2 = w2_ref[...]
    for i in range(a_ref.shape[0]):
        a = a_ref[i]
        nxt = pltpu.roll(a[:, hw:2 * hw], n - 1, 0)
        pre = a[:, 0:hw] + nxt + posc
        act = (pre * jax.nn.sigmoid(pre)).astype(BF16)
        for g in range(B_KV):
            o_ref[i, :, g * HEAD_DIM:(g + 1) * HEAD_DIM] = _dot(act[:, g * CMP_HIDDEN:(g + 1) * CMP_HIDDEN], w2)


def _cmp_b(a, posrows, w1big, w2):
    b, n, wa = a.shape
    sb = _row_tile(b, (8, 4, 2))
    const = lambda i: (0, 0)
    return pl.pallas_call(
        _cmp_b_kernel, grid=(b // sb,),
        in_specs=[pl.BlockSpec((sb, n, wa), lambda i: (i, 0, 0)), pl.BlockSpec(posrows.shape, const),
                  pl.BlockSpec(w1big.shape, const), pl.BlockSpec(w2.shape, const)],
        out_specs=pl.BlockSpec((sb, n, B_KV * HEAD_DIM), lambda i: (i, 0, 0)),
        out_shape=jax.ShapeDtypeStruct((b, n, B_KV * HEAD_DIM), F32),
        compiler_params=_cparams(("arbitrary",)), name="cmp_b")(a, posrows, w1big, w2)


def _compress_prompt(rows, pos, w1, w2):
    b, t, w = rows.shape
    n = t // CMP_STRIDE
    w1big, posrows = _cmp_weights(pos, w1)
    view = rows.reshape(b * n, CMP_STRIDE * w)
    tm = _row_tile(b * n, (256, 128))
    a = pl.pallas_call(
        _cmp_a_kernel, grid=(b * n // tm,),
        in_specs=[pl.BlockSpec((tm, CMP_STRIDE * w), lambda i: (i, 0)), pl.BlockSpec(w1big.shape, lambda i: (0, 0))],
        out_specs=pl.BlockSpec((tm, w1big.shape[1]), lambda i: (i, 0)),
        out_shape=jax.ShapeDtypeStruct((b * n, w1big.shape[1]), F32),
        compiler_params=_cparams(("arbitrary",)), name="cmp_a")(view, w1big)
    return _cmp_b(a.reshape(b, n, -1), posrows, w1big, w2.astype(BF16))


def _cmp_a_paged_kernel(pt_ref, *refs, n_pages):
    x_sc = refs[-1]
    for which in range(2):
        page_refs = refs[which * n_pages:(which + 1) * n_pages]
        w_ref, o_ref = refs[2 * n_pages + which], refs[2 * n_pages + 2 + which]
        lanes, page = page_refs[0].shape
        for i in range(n_pages):
            x_sc[i * page:(i + 1) * page, :] = page_refs[i][...].T
        n_chunks = n_pages * page // CMP_STRIDE
        acc = None
        for j in range(0, CMP_STRIDE, 2):
            xs = jnp.concatenate([x_sc[pl.ds(j, n_chunks, stride=CMP_STRIDE), :],
                                  x_sc[pl.ds(j + 1, n_chunks, stride=CMP_STRIDE), :]], axis=1).astype(BF16)
            term = _dot(xs, w_ref[j * lanes:(j + 2) * lanes, :])
            acc = term if acc is None else acc + term
        o_ref[...] = acc


def _compress_paged(cache_k, cache_v, layer, page_table, pos_k, w1_k, w2_k, pos_v, w1_v, w2_v, pages_per_step):
    _, n_pool, w, page = cache_k.shape
    b, n_pages = page_table.shape
    cpp = page // CMP_STRIDE
    w1big_k, posrows_k = _cmp_weights(pos_k, w1_k)
    w1big_v, posrows_v = _cmp_weights(pos_v, w1_v)
    p = pages_per_step
    pages = _page_specs(p, w, page, layer, lambda c, j: c * p + j)
    wspec = pl.BlockSpec(w1big_k.shape, lambda i, c, pt: (0, 0))
    ospec = pl.BlockSpec((None, p * cpp, w1big_k.shape[1]), lambda i, c, pt: (i, c, 0))
    oshape = jax.ShapeDtypeStruct((b, n_pages * cpp, w1big_k.shape[1]), F32)
    a_k, a_v = pl.pallas_call(
        functools.partial(_cmp_a_paged_kernel, n_pages=p),
        grid_spec=pltpu.PrefetchScalarGridSpec(
            num_scalar_prefetch=1, grid=(b, n_pages // p),
            in_specs=pages + pages + [wspec, wspec], out_specs=[ospec, ospec],
            scratch_shapes=[pltpu.VMEM((p * page, w), F32)]),
        out_shape=[oshape, oshape],
        compiler_params=_cparams(("arbitrary", "arbitrary")), name="cmp_a_paged")(
            page_table, *([cache_k] * p), *([cache_v] * p), w1big_k, w1big_v)
    return (_cmp_b(a_k, posrows_k, w1big_k, w2_k.astype(BF16)), _cmp_b(a_v, posrows_v, w1big_v, w2_v.astype(BF16)))


def _top_blocks(score, n_pick, axis):
    blk = _iota(score.shape, axis)
    n_blocks = score.shape[axis]

    def body(_, carry):
        cur, picked = carry
        best = jnp.max(cur, axis=axis, keepdims=True)
        idx = jnp.min(jnp.where(cur == best, blk, n_blocks), axis=axis, keepdims=True)
        hit = blk == idx
        return jnp.where(hit, -2.0, cur), jnp.where(hit, 1.0, picked)

    _, picked = lax.fori_loop(0, n_pick, body, (score, jnp.zeros(score.shape, F32)))
    return picked


def _nsa_prompt_kernel(slopes_ref, q_ref, gate_ref, kc_ref, vc_ref, sk_ref, sv_ref, wk_ref, wv_ref, o_ref,
                       m_sc, acc_sc, skb_sc, sva_sc, wkb_sc, wva_sc, *, tq, tk, t, lw, nbp):
    groups = [slice(g * HEAD_DIM, (g + 1) * HEAD_DIM) for g in range(B_KV)]

    @pl.when(pl.program_id(1) == 0)
    def _():
        _stage_bf16(sk_ref, skb_sc, tk)
        _stage_bf16(wk_ref, wkb_sc, tk)
        for g, gs in enumerate(groups):
            _stage_values(sv_ref, sva_sc.at[g], tk, gs, HEAD_DIM)
            _stage_values(wv_ref, wva_sc.at[g], tk, gs, HEAD_DIM)

    q0 = pl.multiple_of(pl.program_id(1) * tq, tq)
    qpos1 = q0 + _iota((tq, 1), 0)
    n_cmp_rows = kc_ref.shape[0]
    n_cmp = n_cmp_rows - (CMP_BLOCK // CMP_STRIDE - 1)
    n_blocks = t // SEL_BLOCK
    bpt = tk // SEL_BLOCK
    gates = jax.nn.sigmoid(gate_ref[...])
    cidx = _iota((1, n_cmp_rows), 1)
    c_end = cidx * CMP_STRIDE + (CMP_BLOCK - 1)
    ok_cmp = (c_end <= qpos1) & (cidx < n_cmp)
    crel = (c_end - q0).astype(F32)
    w0 = pl.multiple_of(jnp.maximum(q0 + tq - lw, 0), tq)
    wpos = w0 + _iota((1, lw), 1)
    dw = qpos1 - wpos
    ok_win = (dw >= 0) & (dw < WINDOW)
    wrel = (wpos - q0).astype(F32)
    blk_t = _iota((nbp, tq), 0)
    qpos_t = q0 + _iota((1, tq), 1)
    valid_t = (blk_t * SEL_BLOCK <= qpos_t) & (blk_t < n_blocks)
    forced_t = (blk_t == qpos_t // SEL_BLOCK) | (blk_t == 0)
    overlap_t = (_iota((nbp, n_cmp_rows), 1) * CMP_STRIDE < _iota((nbp, n_cmp_rows), 0) * SEL_BLOCK + SEL_BLOCK) & (
        _iota((nbp, n_cmp_rows), 1) * CMP_STRIDE + CMP_BLOCK > _iota((nbp, n_cmp_rows), 0) * SEL_BLOCK)
    overlap_t = overlap_t.astype(BF16)

    def biased(raw, ok, slopes2, rel):
        return jnp.concatenate(
            [jnp.where(ok, raw[r * tq:(r + 1) * tq] + slopes2[r] * rel, NEG_INF) for r in range(B_REP)], axis=0)

    per_group = []
    for g, gs in enumerate(groups):
        q32 = jnp.concatenate(
            [q_ref[:, (g * B_REP + r) * HEAD_DIM:(g * B_REP + r + 1) * HEAD_DIM] for r in range(B_REP)],
            axis=0) * (SCALE * LOG2E)
        qb = q32.astype(BF16)
        slopes2 = [slopes_ref[g * B_REP + r] * LOG2E for r in range(B_REP)]

        sc = biased(_dot_nt_precise(q32, kc_ref[:, gs]), ok_cmp, slopes2, crel)
        mx = jnp.max(sc, axis=-1, keepdims=True)
        e = jnp.exp2(sc - mx)
        den = jnp.sum(e, axis=-1, keepdims=True)
        pc = e * jnp.where(mx > 0.5 * NEG_INF, 1.0 / den, 0.0)
        o_cmp = _dot(pc.astype(BF16), vc_ref[:, gs].astype(BF16))

        psum = pc[0:tq]
        for r in range(1, B_REP):
            psum = psum + pc[r * tq:(r + 1) * tq]
        hi, mid, lo = _split3(psum)
        imp_t = _dot_nt(overlap_t, hi) + _dot_nt(overlap_t, mid) + _dot_nt(overlap_t, lo)
        score_t = jnp.where(forced_t, 1e30, jnp.where(valid_t, imp_t, -1.0))
        picked = _top_blocks(score_t, min(SEL_TOP, n_blocks), 0).T.astype(BF16)
        per_group.append((qb, slopes2, o_cmp, picked))

    m_sc[...] = jnp.full(m_sc.shape, NEG_INF, F32)
    acc_sc[...] = jnp.zeros(acc_sc.shape, F32)

    def tile(kj, masked):
        k0 = pl.multiple_of(kj * tk, tk)
        expand = (_iota((nbp, tk), 0) == kj * bpt + _iota((nbp, tk), 1) // SEL_BLOCK).astype(BF16)
        kpos = k0 + _iota((1, tk), 1)
        krel = (kpos - q0).astype(F32)
        for g, gs in enumerate(groups):
            qb, slopes2, _, picked = per_group[g]
            keep = _dot(picked, expand) > 0.5
            if masked:
                keep = keep & (qpos1 >= kpos)
            s = biased(_dot_nt(qb, skb_sc[pl.ds(k0, tk), gs]), keep, slopes2, krel)
            _flash_update(s, sva_sc[g, pl.ds(k0, tk), :], m_sc.at[g], acc_sc.at[g])

    _causal_tiles(q0, tq, tk, tile)

    for g, gs in enumerate(groups):
        qb, slopes2, o_cmp, _ = per_group[g]
        o_sel = acc_sc[g][:, 0:HEAD_DIM] / jnp.maximum(acc_sc[g][:, HEAD_DIM:LANES], 1e-30)

        sw = biased(_dot_nt(qb, wkb_sc[pl.ds(w0, lw), gs]), ok_win, slopes2, wrel)
        ew = jnp.exp2(sw - jnp.max(sw, axis=-1, keepdims=True))
        ow = _dot(ew.astype(BF16), wva_sc[g, pl.ds(w0, lw), :])
        o_win = ow[:, 0:HEAD_DIM] / ow[:, HEAD_DIM:LANES]

        for r in range(B_REP):
            h = g * B_REP + r
            rs = slice(r * tq, (r + 1) * tq)
            o_ref[:, h * HEAD_DIM:(h + 1) * HEAD_DIM] = (
                gates[:, 3 * h:3 * h + 1] * o_cmp[rs] + gates[:, 3 * h + 1:3 * h + 2] * o_sel[rs]
                + gates[:, 3 * h + 2:3 * h + 3] * o_win[rs])


def _nsa_prompt(q, gate, kc, vc, sk, sv, wk, wv, slopes):
    b, t, w = q.shape
    tq = _row_tile(t, (512, 256, 128))
    tk = _row_tile(t, (512, 256, 128))
    lw = min(WINDOW + tq, t)
    kvw = B_KV * HEAD_DIM
    ncr = kc.shape[1]
    nbp = -(-(t // SEL_BLOCK) // LANES) * LANES
    full = lambda width, rows: pl.BlockSpec((None, rows, width), lambda i, j: (i, 0, 0))
    return pl.pallas_call(
        functools.partial(_nsa_prompt_kernel, tq=tq, tk=tk, t=t, lw=lw, nbp=nbp),
        grid=(b, t // tq),
        in_specs=[pl.BlockSpec(memory_space=pltpu.SMEM),
                  pl.BlockSpec((None, tq, w), lambda i, j: (i, j, 0)),
                  pl.BlockSpec((None, tq, LANES), lambda i, j: (i, j, 0)),
                  full(kvw, ncr), full(kvw, ncr), full(kvw, t), full(kvw, t), full(kvw, t), full(kvw, t)],
        out_specs=pl.BlockSpec((None, tq, w), lambda i, j: (i, j, 0)),
        out_shape=jax.ShapeDtypeStruct((b, t, w), F32),
        scratch_shapes=[pltpu.VMEM((B_KV, B_REP * tq, LANES), F32), pltpu.VMEM((B_KV, B_REP * tq, LANES), F32),
                        pltpu.VMEM((t, kvw), BF16), pltpu.VMEM((B_KV, t, LANES), BF16),
                        pltpu.VMEM((t, kvw), BF16), pltpu.VMEM((B_KV, t, LANES), BF16)],
        compiler_params=_cparams(("arbitrary", "arbitrary")), name="nsa_prompt")(
            slopes, q, gate, kc, vc, sk, sv, wk, wv)


NEW_PAD = 16


def _page_specs(n, rows, cols, layer, idx_fn):
    return [pl.BlockSpec((None, None, rows, cols),
                         lambda i, c, pt, j=j: (layer, pt[i, idx_fn(c, j)], 0, 0)) for j in range(n)]


def _flash_pages(qb, k_refs, v_refs, bias_fn, m_sc, l_sc, acc_sc, key_major_v=None):
    s = bias_fn(_dot(qb, jnp.concatenate([kr[...].astype(BF16) for kr in k_refs], axis=1)))
    m_old = m_sc[...]
    m_new = jnp.maximum(m_old, jnp.max(s, axis=-1, keepdims=True))
    alpha = jnp.exp(m_old - m_new)
    p = jnp.exp(s - m_new)
    l_sc[...] = alpha * l_sc[...] + jnp.sum(p, axis=-1, keepdims=True)
    if key_major_v is None:
        pv = _dot_nt(p.astype(BF16), jnp.concatenate([vr[...].astype(BF16) for vr in v_refs], axis=1))
    else:
        pv = _dot(p.astype(BF16), jnp.concatenate([key_major_v(vr) for vr in v_refs], axis=0))
    acc_sc[...] = alpha * acc_sc[...] + pv
    m_sc[...] = m_new


def _pad_new(x, b, ts):
    return jnp.pad(x.reshape(b, ts, x.shape[-1]), ((0, 0), (0, NEW_PAD - ts), (0, 0)))


def _diff_dec_kernel(pt_ref, qbd_ref, info_ref, kn_ref, vn_ref, lq1, lk1, lq2, lk2, subg_ref, *rest,
                     p, page, n_past, n_new, lam_init):
    k_refs, v_refs = rest[:p], rest[p:2 * p]
    o_ref, m_sc, l_sc, acc_sc = rest[2 * p:]
    c = pl.program_id(1)

    @pl.when(c == 0)
    def _():
        m_sc[...] = jnp.full(m_sc.shape, NEG_INF, F32)
        l_sc[...] = jnp.zeros(l_sc.shape, F32)
        acc_sc[...] = jnp.zeros(acc_sc.shape, F32)

    qb = (qbd_ref[...] * SCALE).astype(BF16)
    slope = info_ref[:, 0:1]
    qpos = info_ref[:, 1:2]
    kpos = (c * (p * page) + _iota((1, p * page), 1)).astype(F32)
    key_major = lambda vr: jnp.concatenate(
        [vr[pl.ds(h, page, stride=A_HEADS), :] for h in range(A_HEADS)], axis=1).astype(BF16)
    _flash_pages(qb, k_refs, v_refs, lambda s: s - slope * (qpos - kpos), m_sc, l_sc, acc_sc, key_major)

    @pl.when(c == pl.num_programs(1) - 1)
    def _():
        idx = _iota((1, NEW_PAD), 1)
        dist = qpos - (n_past + idx).astype(F32)
        s = _dot_nt(qb, kn_ref[...].astype(BF16))
        s = jnp.where((dist >= 0) & (idx < n_new), s - slope * dist, NEG_INF)
        _online_step(s, vn_ref[...].astype(BF16), m_sc, l_sc, acc_sc)
        full = acc_sc[...] / l_sc[...]
        rows = full.shape[0]
        rowh = (_iota((rows, 1), 0) // n_new) % A_HEADS
        o = jnp.zeros((rows, A_V), F32)
        for h in range(A_HEADS):
            o = o + jnp.where(rowh == h, full[:, h * A_V:(h + 1) * A_V], 0.0)
        half = rows // 2
        lam = _diff_lambda(lq1[...], lk1[...], lq2[...], lk2[...], lam_init)
        o = o[0:half] - lam * o[half:rows]
        o_ref[...] = _rms(o, subg_ref[...]) * (1.0 - lam_init)


def _diff_dec(q, kn, vn, cache_k, cache_v, layer, page_table, slopes, lams, subg, lam_init, pages_per_step):
    b, n_pages = page_table.shape
    ts = q.shape[0] // b
    kw, page = cache_k.shape[2], cache_k.shape[3]
    vw = A_HEADS * A_V
    n_past = n_pages * page
    p = pages_per_step
    rows = 2 * A_HEADS * ts
    q5 = q.reshape(b, ts, A_HEADS, 2, HEAD_DIM).transpose(0, 3, 2, 1, 4)
    eye_h = jnp.eye(A_HEADS, dtype=F32).reshape(1, 1, A_HEADS, 1, A_HEADS, 1, 1)
    eye_c = jnp.eye(2, dtype=F32).reshape(1, 2, 1, 1, 1, 2, 1)
    qbd = (q5[:, :, :, :, None, None, :] * eye_h * eye_c).reshape(b, rows, kw)
    slope_r = jnp.broadcast_to(slopes.reshape(1, A_HEADS, 1), (2, A_HEADS, ts)).reshape(rows)
    qpos_r = jnp.broadcast_to((n_past + jnp.arange(ts, dtype=F32)).reshape(1, 1, ts), (2, A_HEADS, ts)).reshape(rows)
    info = jnp.pad(jnp.stack([slope_r, qpos_r], axis=1), ((0, 0), (0, LANES - 2)))
    const2 = lambda i, c, pt: (0, 0)
    per_b = lambda r, w: pl.BlockSpec((None, r, w), lambda i, c, pt: (i, 0, 0))
    vec = pl.BlockSpec((1, HEAD_DIM), const2)
    in_specs = ([per_b(rows, kw), pl.BlockSpec((rows, LANES), const2), per_b(NEW_PAD, kw), per_b(NEW_PAD, vw),
                 vec, vec, vec, vec, pl.BlockSpec((1, A_V), const2)]
                + _page_specs(p, kw, page, layer, lambda c, j: c * p + j)
                + _page_specs(p, page * A_HEADS, A_V, layer, lambda c, j: c * p + j))
    out = pl.pallas_call(
        functools.partial(_diff_dec_kernel, p=p, page=page, n_past=n_past, n_new=ts, lam_init=lam_init),
        grid_spec=pltpu.PrefetchScalarGridSpec(
            num_scalar_prefetch=1, grid=(b, n_pages // p), in_specs=in_specs,
            out_specs=pl.BlockSpec((None, rows // 2, A_V), lambda i, c, pt: (i, 0, 0)),
            scratch_shapes=[pltpu.VMEM((rows, 1), F32), pltpu.VMEM((rows, 1), F32), pltpu.VMEM((rows, vw), F32)]),
        out_shape=jax.ShapeDtypeStruct((b, rows // 2, A_V), F32),
        compiler_params=_cparams(("arbitrary", "arbitrary")), name="diff_dec")(
            page_table, qbd, info, _pad_new(kn, b, ts), _pad_new(vn, b, ts),
            *[x.reshape(1, HEAD_DIM) for x in lams], subg.reshape(1, A_V),
            *([cache_k] * p), *([cache_v] * p))
    return out.reshape(b, A_HEADS, ts, A_V).transpose(0, 2, 1, 3).reshape(b * ts, A_HEADS * A_V)


def _fox_dec_kernel(pt_ref, qbd_ref, pn_ref, pnt_ref, kn_ref, vn_ref, *rest, p, page, n_new):
    k_refs, v_refs, lf_refs = rest[:p], rest[p:2 * p], rest[2 * p:3 * p]
    o_ref, m_sc, l_sc, acc_sc, run_sc = rest[3 * p:]
    c = pl.program_id(1)
    qb = (qbd_ref[...] * SCALE).astype(BF16)
    rows = qb.shape[0]
    pn_col = pn_ref[:, 0:1]

    @pl.when(c == 0)
    def _():
        m_sc[...] = jnp.full(m_sc.shape, NEG_INF, F32)
        l_sc[...] = jnp.zeros(l_sc.shape, F32)
        acc_sc[...] = jnp.zeros(acc_sc.shape, F32)
        run_sc[...] = jnp.zeros(run_sc.shape, F32)
        idx = _iota((1, NEW_PAD), 1)
        tok = _iota((rows, 1), 0) // C_HEADS
        bias = pn_col - jnp.concatenate([pnt_ref[...]] * n_new, axis=0)
        s = _dot_nt(qb, kn_ref[...].astype(BF16)) + bias
        s = jnp.where((idx <= tok) & (idx < n_new), s, NEG_INF)
        _online_step(s, vn_ref[...].astype(BF16), m_sc, l_sc, acc_sc)

    later = (_iota((page, page), 0) > _iota((page, page), 1)).astype(BF16)
    run = run_sc[...]
    sufs = [None] * p
    for j in reversed(range(p)):
        lf = lf_refs[j][...]
        sufs[j] = _dot_exact_rhs(lf, later) + run
        run = run + jnp.sum(lf, axis=-1, keepdims=True)
    run_sc[...] = run
    suf = jnp.concatenate(sufs, axis=1)
    bias = jnp.concatenate([suf] * n_new, axis=0) + pn_col
    _flash_pages(qb, k_refs, v_refs, lambda s: s + bias, m_sc, l_sc, acc_sc)

    @pl.when(c == pl.num_programs(1) - 1)
    def _():
        full = acc_sc[...] / l_sc[...]
        rowh = _iota((rows, 1), 0) % C_HEADS
        o2 = jnp.zeros((rows, LANES), F32)
        for hp in range(C_HEADS // 2):
            o2 = o2 + jnp.where(rowh // 2 == hp, full[:, hp * LANES:(hp + 1) * LANES], 0.0)
        o_ref[...] = jnp.where(rowh % 2 == 0, o2[:, 0:HEAD_DIM], o2[:, HEAD_DIM:LANES])


def _fox_dec(q, kn, vn, cum_new, cache_k, cache_v, cache_lft, layer, page_table, pages_per_step):
    b, n_pages = page_table.shape
    ts = q.shape[0] // b
    w, page = cache_k.shape[2], cache_k.shape[3]
    p = pages_per_step
    nsteps = n_pages // p
    rows = ts * C_HEADS
    eye = jnp.eye(C_HEADS, dtype=F32).reshape(1, 1, C_HEADS, C_HEADS, 1)
    qbd = (q.reshape(b, ts, C_HEADS, 1, HEAD_DIM) * eye).reshape(b, rows, w)
    pn = jnp.pad(cum_new.reshape(b, rows, 1), ((0, 0), (0, 0), (0, LANES - 1)))
    pnt = jnp.pad(cum_new.transpose(0, 2, 1), ((0, 0), (0, 0), (0, NEW_PAD - ts)))
    per_b = lambda r, ww: pl.BlockSpec((None, r, ww), lambda i, c, pt: (i, 0, 0))
    rev = lambda c, j: (nsteps - 1 - c) * p + j
    in_specs = ([per_b(rows, w), per_b(rows, LANES), per_b(C_HEADS, NEW_PAD), per_b(NEW_PAD, w), per_b(NEW_PAD, w)]
                + _page_specs(p, w, page, layer, rev) + _page_specs(p, w, page, layer, rev)
                + _page_specs(p, C_HEADS, page, layer, rev))
    out = pl.pallas_call(
        functools.partial(_fox_dec_kernel, p=p, page=page, n_new=ts),
        grid_spec=pltpu.PrefetchScalarGridSpec(
            num_scalar_prefetch=1, grid=(b, nsteps), in_specs=in_specs,
            out_specs=pl.BlockSpec((None, rows, HEAD_DIM), lambda i, c, pt: (i, 0, 0)),
            scratch_shapes=[pltpu.VMEM((rows, 1), F32), pltpu.VMEM((rows, 1), F32), pltpu.VMEM((rows, w), F32),
                            pltpu.VMEM((C_HEADS, 1), F32)]),
        out_shape=jax.ShapeDtypeStruct((b, rows, HEAD_DIM), F32),
        compiler_params=_cparams(("arbitrary", "arbitrary")), name="fox_dec")(
            page_table, qbd, pn, pnt, _pad_new(kn, b, ts), _pad_new(vn, b, ts),
            *([cache_k] * p), *([cache_v] * p), *([cache_lft] * p))
    return out.reshape(b * ts, C_HEADS * HEAD_DIM)


def _nsa_dec_kernel(pt_ref, qbd_ref, info_ref, kc_ref, vc_ref, skn_ref, svn_ref, wkn_ref, wvn_ref, wk_ref, wv_ref,
                    *rest, p, page, n_past, n_new):
    k_refs, v_refs = rest[:p], rest[p:2 * p]
    o_ref, m_sc, l_sc, acc_sc, ocmp_sc, pick_sc = rest[2 * p:]
    c = pl.program_id(1)
    q32 = qbd_ref[...] * SCALE
    qb = q32.astype(BF16)
    rows = q32.shape[0]
    gt = rows // B_REP
    slope = info_ref[:, 0:1]
    qpos = info_ref[:, 1:2]
    in_g0 = (_iota((rows, 1), 0) % gt) < n_new
    pick_cols = lambda full: jnp.where(in_g0, full[:, 0:HEAD_DIM], full[:, HEAD_DIM:2 * HEAD_DIM])
    n_blocks = n_past // SEL_BLOCK
    bps = p * page // SEL_BLOCK

    @pl.when(c == 0)
    def _():
        m_sc[...] = jnp.full(m_sc.shape, NEG_INF, F32)
        l_sc[...] = jnp.zeros(l_sc.shape, F32)
        acc_sc[...] = jnp.zeros(acc_sc.shape, F32)
        n_cmp_rows = kc_ref.shape[0]
        n_cmp = n_cmp_rows - (CMP_BLOCK // CMP_STRIDE - 1)
        cidx = _iota((1, n_cmp_rows), 1)
        dc = qpos - (cidx * CMP_STRIDE + (CMP_BLOCK - 1)).astype(F32)
        sc = _dot_nt_precise(q32, kc_ref[...]) - slope * dc
        pc = _masked_softmax_rows(sc, (dc >= 0) & (cidx < n_cmp))
        ocmp_sc[...] = pick_cols(_dot(pc.astype(BF16), vc_ref[...].astype(BF16)))
        psum = pc[0:gt]
        for r in range(1, B_REP):
            psum = psum + pc[r * gt:(r + 1) * gt]
        nbp = -(-n_blocks // LANES) * LANES
        hi, mid, lo = _split3(jnp.concatenate([psum, jnp.zeros((LANES - gt, n_cmp_rows), F32)], axis=0))
        blk_c = _iota((nbp, n_cmp_rows), 0) * SEL_BLOCK
        cmp_c = _iota((nbp, n_cmp_rows), 1) * CMP_STRIDE
        overlap_t = ((cmp_c < blk_c + SEL_BLOCK) & (cmp_c + CMP_BLOCK > blk_c)).astype(BF16)
        imp_t = _dot_nt(overlap_t, hi) + _dot_nt(overlap_t, mid) + _dot_nt(overlap_t, lo)
        blk_t = _iota((nbp, LANES), 0)
        qpos_t = n_past + _iota((1, LANES), 1) % n_new
        forced_t = (blk_t == 0) | (blk_t == qpos_t // SEL_BLOCK)
        valid_t = (blk_t * SEL_BLOCK <= qpos_t) & (blk_t < n_blocks)
        score_t = jnp.where(forced_t, 1e30, jnp.where(valid_t, imp_t, -1.0))
        picked = _top_blocks(score_t, min(SEL_TOP, n_blocks + 1) - 1, 0).T[0:gt]
        for cc in range(pick_sc.shape[0]):
            pick_sc[cc] = picked[:, cc * bps:(cc + 1) * bps]

    keys = p * page
    kpos = c * keys + _iota((1, keys), 1)
    expand = (_iota((bps, keys), 0) == _iota((bps, keys), 1) // SEL_BLOCK).astype(BF16)
    sel1 = _dot(pick_sc[c].astype(BF16), expand)
    sel = jnp.concatenate([sel1] * B_REP, axis=0)
    ds = qpos - kpos.astype(F32)
    _flash_pages(qb, k_refs, v_refs, lambda s: jnp.where((sel > 0.5) & (ds >= 0), s - slope * ds, NEG_INF),
                 m_sc, l_sc, acc_sc)

    @pl.when(c == pl.num_programs(1) - 1)
    def _():
        idx = _iota((1, NEW_PAD), 1)
        dn = qpos - (n_past + idx).astype(F32)
        new_ok = (dn >= 0) & (idx < n_new)
        s = jnp.where(new_ok, _dot_nt(qb, skn_ref[...].astype(BF16)) - slope * dn, NEG_INF)
        _online_step(s, svn_ref[...].astype(BF16), m_sc, l_sc, acc_sc)
        o_sel = pick_cols(acc_sc[...] / jnp.maximum(l_sc[...], 1e-30))
        keep = wk_ref.shape[1]
        wpos = n_past - keep + _iota((1, keep), 1)
        dw = qpos - wpos.astype(F32)
        ok1 = (dw >= 0) & (dw < WINDOW) & (wpos >= 0)
        ok2 = new_ok & (dn < WINDOW)
        s1 = jnp.where(ok1, _dot(qb, wk_ref[...].astype(BF16)) - slope * dw, NEG_INF)
        s2 = jnp.where(ok2, _dot_nt(qb, wkn_ref[...].astype(BF16)) - slope * dn, NEG_INF)
        mx = jnp.maximum(jnp.max(s1, axis=-1, keepdims=True), jnp.max(s2, axis=-1, keepdims=True))
        e1 = jnp.where(ok1, jnp.exp(s1 - mx), 0.0)
        e2 = jnp.where(ok2, jnp.exp(s2 - mx), 0.0)
        den = jnp.maximum(jnp.sum(e1, axis=-1, keepdims=True) + jnp.sum(e2, axis=-1, keepdims=True), 1e-30)
        o_win = pick_cols((_dot_nt(e1.astype(BF16), wv_ref[...].astype(BF16))
                           + _dot(e2.astype(BF16), wvn_ref[...].astype(BF16))) / den)
        gates = jax.nn.sigmoid(info_ref[:, 2:5])
        o_ref[...] = gates[:, 0:1] * ocmp_sc[...] + gates[:, 1:2] * o_sel + gates[:, 2:3] * o_win


def _nsa_dec(q, gate, kc, vc, skn, svn, wkn, wvn, win_k, win_v, cache_sk, cache_sv, layer, page_table, slopes,
             pages_per_step):
    b, n_pages = page_table.shape
    ts = q.shape[0] // b
    w, page = cache_sk.shape[2], cache_sk.shape[3]
    n_past = n_pages * page
    keep = win_k.shape[3]
    p = pages_per_step
    rows = B_HEADS * ts
    q5 = q.reshape(b, ts, B_KV, B_REP, HEAD_DIM).transpose(0, 3, 2, 1, 4)
    eye = jnp.eye(B_KV, dtype=F32).reshape(1, 1, B_KV, 1, B_KV, 1)
    qbd = (q5[:, :, :, :, None, :] * eye).reshape(b, rows, w)
    slope_r = jnp.broadcast_to(slopes.reshape(B_KV, B_REP).T.reshape(B_REP, B_KV, 1), (B_REP, B_KV, ts)).reshape(rows)
    qpos_r = jnp.broadcast_to((n_past + jnp.arange(ts, dtype=F32)).reshape(1, 1, ts), (B_REP, B_KV, ts)).reshape(rows)
    glog = gate[:, :3 * B_HEADS].reshape(b, ts, B_KV, B_REP, 3).transpose(0, 3, 2, 1, 4).reshape(b, rows, 3)
    info = jnp.concatenate([jnp.broadcast_to(jnp.stack([slope_r, qpos_r], axis=1)[None], (b, rows, 2)), glog], axis=2)
    info = jnp.pad(info, ((0, 0), (0, 0), (0, LANES - 5)))
    per_b = lambda r, ww: pl.BlockSpec((None, r, ww), lambda i, c, pt: (i, 0, 0))
    win = pl.BlockSpec((None, None, w, keep), lambda i, c, pt: (layer, i, 0, 0))
    ncr = kc.shape[1]
    in_specs = ([per_b(rows, w), per_b(rows, LANES), per_b(ncr, w), per_b(ncr, w)] + [per_b(NEW_PAD, w)] * 4
                + [win, win]
                + _page_specs(p, w, page, layer, lambda c, j: c * p + j)
                + _page_specs(p, w, page, layer, lambda c, j: c * p + j))
    out = pl.pallas_call(
        functools.partial(_nsa_dec_kernel, p=p, page=page, n_past=n_past, n_new=ts),
        grid_spec=pltpu.PrefetchScalarGridSpec(
            num_scalar_prefetch=1, grid=(b, n_pages // p), in_specs=in_specs,
            out_specs=pl.BlockSpec((None, rows, HEAD_DIM), lambda i, c, pt: (i, 0, 0)),
            scratch_shapes=[pltpu.VMEM((rows, 1), F32), pltpu.VMEM((rows, 1), F32), pltpu.VMEM((rows, w), F32),
                            pltpu.VMEM((rows, HEAD_DIM), F32), pltpu.VMEM((n_pages // p, rows // B_REP, p * page // SEL_BLOCK), F32)]),
        out_shape=jax.ShapeDtypeStruct((b, rows, HEAD_DIM), F32),
        compiler_params=_cparams(("arbitrary", "arbitrary")), name="nsa_dec")(
            page_table, qbd, info, kc, vc, _pad_new(skn, b, ts), _pad_new(svn, b, ts), _pad_new(wkn, b, ts),
            _pad_new(wvn, b, ts), win_k, win_v, *([cache_sk] * p), *([cache_sv] * p))
    return out.reshape(b, B_REP, B_KV, ts, HEAD_DIM).transpose(0, 3, 2, 1, 4).reshape(b * ts, B_HEADS * HEAD_DIM)


def _alibi_slopes(n):
    return jnp.exp2(-8.0 * jnp.arange(1, n + 1, dtype=F32) / n)


def _pages_per_step(n_pages, want):
    p = min(want, n_pages)
    while n_pages % p:
        p -= 1
    return p


def kernel(x_prompt, x_sample, cache_diff_k, cache_diff_v, cache_nsa_cmp_k, cache_nsa_cmp_v, cache_nsa_sel_k, cache_nsa_sel_v, state_nsa_win_k, state_nsa_win_v, cache_fox_k, cache_fox_v, cache_fox_logf, page_table, norm_ffn1_g, ffn1_wg, ffn1_wu, ffn1_wd, norm_mix_g, norm_ffn2_g, ffn2_wg, ffn2_wu, ffn2_wd, even_w_in, even_w_out, diff_lambda_q1, diff_lambda_k1, diff_lambda_q2, diff_lambda_k2, diff_subln_g, cmp_pos_k, cmp_w1_k, cmp_w2_k, cmp_pos_v, cmp_w1_v, cmp_w2_v, odd_w_in, odd_b_f, odd_w_out, final_norm_g):
    b, t, d = x_prompt.shape
    bs, ts, _ = x_sample.shape
    depth = norm_ffn1_g.shape[0]
    n_pages = page_table.shape[1]
    n_pool, page = cache_diff_k.shape[1], cache_diff_k.shape[2]
    keep = state_nsa_win_k.shape[2]
    n_past = n_pages * page
    assert n_past % SEL_BLOCK == 0 and page % SEL_BLOCK == 0 and page % CMP_STRIDE == 0
    assert 0 < ts <= min(SEL_BLOCK, CMP_STRIDE - 1, NEW_PAD) and (A_HEADS * ts) % 8 == 0
    assert t % LANES == 0 and n_past >= WINDOW
    kvw = B_KV * HEAD_DIM

    slopes_a = _alibi_slopes(A_HEADS)
    slopes_b = _alibi_slopes(B_HEADS)
    keys_last = lambda c: jnp.moveaxis(c, 2, -1).reshape(c.shape[0], c.shape[1], -1, c.shape[2])
    c_diff_k = keys_last(cache_diff_k)
    c_diff_v = cache_diff_v.reshape(cache_diff_v.shape[0], n_pool, page * A_HEADS, A_V)
    c_cmp_k, c_cmp_v = keys_last(cache_nsa_cmp_k), keys_last(cache_nsa_cmp_v)
    c_sel_k, c_sel_v = keys_last(cache_nsa_sel_k), keys_last(cache_nsa_sel_v)
    c_fox_k, c_fox_v = keys_last(cache_fox_k), keys_last(cache_fox_v)
    c_fox_lft = keys_last(cache_fox_logf)
    win_k, win_v = keys_last(state_nsa_win_k), keys_last(state_nsa_win_v)

    a_w = A_HEADS * 2 * HEAD_DIM
    even_cuts = [a_w, a_w, A_HEADS * A_V, B_HEADS * HEAD_DIM] + [kvw] * 6 + [3 * B_HEADS]
    odd_cuts = [C_HEADS * HEAD_DIM] * 3 + [C_HEADS]

    def split_w(w, cuts):
        out, s = [], 0
        for c in cuts:
            out.append(_pad_cols(w[:, s:s + c], -(-c // LANES) * LANES).astype(BF16))
            s += c
        return out

    xp = x_prompt.reshape(b * t, d)
    xs = x_sample.reshape(bs * ts, d)
    even_p = [[] for _ in range(8)]
    even_s = [[] for _ in range(8)]
    odd_p = [[] for _ in range(3)]
    odd_s = [[] for _ in range(3)]

    for li in range(depth):
        w1 = [ffn1_wg[li].astype(BF16), ffn1_wu[li].astype(BF16), ffn1_wd[li].astype(BF16)]
        xp = _ffn(xp, norm_ffn1_g[li], *w1)
        xs = _ffn(xs, norm_ffn1_g[li], *w1)
        if li % 2 == 0:
            e = li // 2
            lam_init = 0.8 - 0.6 * math.exp(-0.3 * li)
            lams = (diff_lambda_q1[e], diff_lambda_k1[e], diff_lambda_q2[e], diff_lambda_k2[e])
            ws = split_w(even_w_in[e], even_cuts)
            wo = even_w_out[e].astype(BF16)
            wo_a, wo_b = wo[:A_HEADS * A_V], wo[A_HEADS * A_V:]
            aq, ak, av, bq, ck, cv, sk, sv, wk, wv, bg = _norm_proj(xp, norm_mix_g[li], ws)
            r3 = lambda a: a.reshape(b, t, a.shape[-1])
            oa = _diff_prompt(r3(aq), r3(ak), r3(av), slopes_a, lams, diff_subln_g[e], lam_init)
            kc = _compress_prompt(r3(ck), cmp_pos_k[e], cmp_w1_k[e], cmp_w2_k[e])
            vc = _compress_prompt(r3(cv), cmp_pos_v[e], cmp_w1_v[e], cmp_w2_v[e])
            ob = _nsa_prompt(r3(bq), r3(bg), kc, vc, r3(sk), r3(sv), r3(wk), r3(wv), slopes_b)
            xp = _out_proj(xp, [oa.reshape(b * t, -1), ob.reshape(b * t, -1)], [wo_a, wo_b])

            def last_rows(a):
                a = jnp.pad(r3(a), ((0, 0), (max(0, keep - t), 0), (0, 0)))
                return a[:, a.shape[1] - keep:].reshape(b, keep, B_KV, HEAD_DIM)

            rows_p = (ak.reshape(b, t, A_HEADS, 2, HEAD_DIM), av.reshape(b, t, A_HEADS, A_V),
                      ck.reshape(b, t, B_KV, HEAD_DIM), cv.reshape(b, t, B_KV, HEAD_DIM),
                      sk.reshape(b, t, B_KV, HEAD_DIM), sv.reshape(b, t, B_KV, HEAD_DIM), last_rows(wk), last_rows(wv))
            for lst, r in zip(even_p, rows_p):
                lst.append(r)
            aq, ak, av, bq, ck, cv, sk, sv, wk, wv, bg = _norm_proj(xs, norm_mix_g[li], ws)
            oa = _diff_dec(aq, ak, av, c_diff_k, c_diff_v, e, page_table, slopes_a, lams, diff_subln_g[e], lam_init,
                           _pages_per_step(n_pages, 32))
            pps = _pages_per_step(n_pages, 32)
            kc, vc = _compress_paged(c_cmp_k, c_cmp_v, e, page_table, cmp_pos_k[e], cmp_w1_k[e], cmp_w2_k[e],
                                     cmp_pos_v[e], cmp_w1_v[e], cmp_w2_v[e], pps)
            ob = _nsa_dec(bq, bg, kc, vc, sk, sv, wk, wv, win_k, win_v, c_sel_k, c_sel_v, e, page_table, slopes_b, pps)
            xs = _out_proj(xs, [oa, ob], [wo_a, wo_b])
            s3 = lambda a: a.reshape(bs, ts, a.shape[-1])
            new_win = lambda buf, a: jnp.concatenate(
                [buf[e], s3(a).reshape(bs, ts, B_KV, HEAD_DIM)], axis=1)[:, ts:]
            rows_s = (ak.reshape(bs, ts, A_HEADS, 2, HEAD_DIM), av.reshape(bs, ts, A_HEADS, A_V),
                      ck.reshape(bs, ts, B_KV, HEAD_DIM), cv.reshape(bs, ts, B_KV, HEAD_DIM),
                      sk.reshape(bs, ts, B_KV, HEAD_DIM), sv.reshape(bs, ts, B_KV, HEAD_DIM),
                      new_win(state_nsa_win_k, wk), new_win(state_nsa_win_v, wv))
            for lst, r in zip(even_s, rows_s):
                lst.append(r)
        else:
            o = li // 2
            ws = split_w(odd_w_in[o], odd_cuts)
            wo = odd_w_out[o].astype(BF16)
            q, k, v, f = _norm_proj(xp, norm_mix_g[li], ws)
            r3 = lambda a: a.reshape(b, t, a.shape[-1])
            logf, cum = _logf_cum(r3(f), odd_b_f[o])
            om = _fox_prompt(r3(q), r3(k), r3(v), cum)
            xp = _out_proj(xp, [om.reshape(b * t, -1)], [wo])
            for lst, r in zip(odd_p, (k.reshape(b, t, C_HEADS, HEAD_DIM), v.reshape(b, t, C_HEADS, HEAD_DIM), logf)):
                lst.append(r)
            q, k, v, f = _norm_proj(xs, norm_mix_g[li], ws)
            logf, cum = _logf_cum(f.reshape(bs, ts, -1), odd_b_f[o])
            om = _fox_dec(q, k, v, cum, c_fox_k, c_fox_v, c_fox_lft, o, page_table, _pages_per_step(n_pages, 16))
            xs = _out_proj(xs, [om], [wo])
            for lst, r in zip(odd_s, (k.reshape(bs, ts, C_HEADS, HEAD_DIM), v.reshape(bs, ts, C_HEADS, HEAD_DIM), logf)):
                lst.append(r)
        w2 = [ffn2_wg[li].astype(BF16), ffn2_wu[li].astype(BF16), ffn2_wd[li].astype(BF16)]
        fin = final_norm_g if li == depth - 1 else None
        xp = _ffn(xp, norm_ffn2_g[li], *w2, final_g=fin)
        xs = _ffn(xs, norm_ffn2_g[li], *w2, final_g=fin)

    stk = lambda lst: jnp.stack(lst, 0)
    ep = [stk(l) for l in even_p]
    es = [stk(l) for l in even_s]
    op = [stk(l) for l in odd_p]
    os_ = [stk(l) for l in odd_s]
    out = [xp.reshape(b, t, d), xs.reshape(bs, ts, d)]
    for p_, s_ in zip(ep, es):
        out += [p_, s_]
    for p_, s_ in zip(op, os_):
        out += [p_, s_]
    return tuple(out)
```
